```python
import math
import jax, jax.numpy as jnp
from jax import lax
import numpy as np

D_MODEL = 2048
BATCH = 32
SEQ = 256
DEPTH = 2
DEC_BATCH = 2
DEC_SEQ = 4096
PAST_LEN = 256

GRID_W = 64
NA_HEADS = 8
NA_HEAD_DIM = 64
NA_WIDTH = NA_HEADS * NA_HEAD_DIM
NA_WIN_ROWS = 8
NA_WIN_COLS = 16
SG_GROUPS = 4
SG_GROUP_DIM = 128
SG_WIDTH = SG_GROUPS * SG_GROUP_DIM
SG_CHUNK = 128
RW_HEADS = 16
RW_HEAD_DIM = 64
RW_WIDTH = RW_HEADS * RW_HEAD_DIM
DECAY_LORA = 64
AAA_LORA = 64
GATE_LORA = 128
N_EXPERTS = 16
D_EXPERT = 1024
CAPACITY_FACTOR = 2
N_MOD = 6
NORM_EPS = 1e-6
GN_EPS = 64e-5
IN_COLS = 3 * NA_WIDTH + 2 * SG_WIDTH + 3 * RW_WIDTH + DECAY_LORA + AAA_LORA + GATE_LORA

kernel_name = 'hybrid_diffusion_natten_sgu_rwkv7_ec_step'


def rms_norm(x, g):
    xf = x.astype(jnp.float32)
    y = xf * lax.rsqrt(jnp.mean(xf * xf, axis=-1, keepdims=True) + NORM_EPS)
    return (y * g.astype(jnp.float32)).astype(x.dtype)


def modulation(cond, w_ada, b_ada):
    m = jax.nn.silu(cond) @ w_ada + b_ada
    return jnp.split(m[..., None, :], N_MOD, axis=-1)


def split_proj(z):
    sizes = (NA_WIDTH, NA_WIDTH, NA_WIDTH, SG_WIDTH, SG_WIDTH, RW_WIDTH, RW_WIDTH, RW_WIDTH,
             DECAY_LORA, AAA_LORA, GATE_LORA)
    cuts, acc = [], 0
    for s in sizes[:-1]:
        acc += s
        cuts.append(acc)
    return jnp.split(z, cuts, axis=-1)


def na_context(q, k, v):
    s = jnp.einsum('bqhd,bkhd->bhqk', q * NA_HEAD_DIM ** -0.5, k).astype(jnp.float32)
    p = jax.nn.softmax(s, axis=-1).astype(v.dtype)
    return jnp.einsum('bhqk,bkhd->bqhd', p, v)


def na_latent(q, k, v, k_ctx, v_ctx, rpb):
    B, T, H, dh = q.shape
    rows = T // GRID_W
    kr = min(NA_WIN_ROWS, rows)
    kc = NA_WIN_COLS
    qg = (q * NA_HEAD_DIM ** -0.5).reshape(B, rows, GRID_W, H, dh)
    kg = k.reshape(B, rows, GRID_W, H, dh)
    vg = v.reshape(B, rows, GRID_W, H, dh)
    cols = jnp.arange(GRID_W)
    col_idx = jnp.clip(cols - kc // 2, 0, GRID_W - kc)[:, None] + jnp.arange(kc)
    col_off = col_idx - cols[:, None] + (NA_WIN_COLS - 1)
    rpb = rpb.astype(jnp.float32)

    def row_fn(r):
        r0 = jnp.clip(r - kr // 2, 0, rows - kr)
        q_r = lax.dynamic_index_in_dim(qg, r, axis=1, keepdims=False)
        k_win = lax.dynamic_slice_in_dim(kg, r0, kr, axis=1)[:, :, col_idx]
        v_win = lax.dynamic_slice_in_dim(vg, r0, kr, axis=1)[:, :, col_idx]
        row_off = r0 + jnp.arange(kr) - r + (NA_WIN_ROWS - 1)
        bias = rpb[:, row_off][:, :, col_off].transpose(0, 2, 1, 3)
        s_loc = jnp.einsum('bqhd,brqchd->bhqrc', q_r, k_win).astype(jnp.float32) + bias[None]
        s_loc = s_loc.reshape(B, H, GRID_W, kr * kc)
        s_ctx = jnp.einsum('bqhd,bkhd->bhqk', q_r, k_ctx).astype(jnp.float32)
        p = jax.nn.softmax(jnp.concatenate([s_loc, s_ctx], axis=-1), axis=-1).astype(v.dtype)
        p_loc = p[..., :kr * kc].reshape(B, H, GRID_W, kr, kc)
        p_ctx = p[..., kr * kc:]
        return (jnp.einsum('bhqrc,brqchd->bqhd', p_loc, v_win)
                + jnp.einsum('bhqk,bkhd->bqhd', p_ctx, v_ctx))

    out = lax.map(row_fn, jnp.arange(rows))
    return out.transpose(1, 0, 2, 3, 4).reshape(B, T, H * dh)


def spatial_gating(u, v, gain, w_s, b_s):
    B, T, _ = u.shape
    n = T // SG_CHUNK
    vn = rms_norm(v.reshape(B, T, SG_GROUPS, SG_GROUP_DIM), gain.reshape(SG_GROUPS, SG_GROUP_DIM))
    vn = vn.reshape(B, n, SG_CHUNK, SG_GROUPS, SG_GROUP_DIM)
    mixed = jnp.einsum('gpq,bnqgc->bnpgc', w_s, vn) + b_s.T[:, :, None]
    return u * mixed.reshape(B, T, SG_WIDTH)


def wkv_scan(r, w, k, v, a, b, s0, reverse):
    def step(S, xs):
        r_t, w_t, k_t, v_t, a_t, b_t = xs
        sa = jnp.einsum('bhvk,bhk->bhv', S, a_t)
        S = S * w_t[:, :, None, :] + sa[..., None] * b_t[:, :, None, :] + v_t[..., None] * k_t[:, :, None, :]
        return S, jnp.einsum('bhvk,bhk->bhv', S, r_t)
    xs = tuple(jnp.swapaxes(t, 0, 1) for t in (r, w, k, v, a, b))
    s_T, o = lax.scan(step, s0.astype(jnp.float32), xs, reverse=reverse)
    return jnp.swapaxes(o, 0, 1), s_T


def rwkv7_mix(r, k, v, x_w, x_a, x_g, s0_f, s0_b, lp):
    B, T, _ = r.shape
    f32 = jnp.float32
    shp = (B, T, RW_HEADS, RW_HEAD_DIM)
    rf, kf, vf = r.astype(f32), k.astype(f32), v.astype(f32)
    kk = (kf * lp['rw_kk']).reshape(shp)
    kk = kk * lax.rsqrt(jnp.sum(kk * kk, axis=-1, keepdims=True) + 1e-12)
    tw = jnp.tanh(x_w.astype(f32))
    xa = x_a.astype(f32)
    outs, finals = [], []
    for d, (s0, rev) in enumerate(((s0_f, False), (s0_b, True))):
        w_log = -jax.nn.softplus(-(lp['rw_w0'][d] + tw @ lp['rw_w2'][d])) - 0.5
        decay = jnp.exp(-jnp.exp(w_log))
        a_rate = jax.nn.sigmoid(lp['rw_a0'][d] + xa @ lp['rw_a2'][d])
        k_d = kf * (1.0 + (a_rate - 1.0) * lp['rw_ka'])
        a_h = a_rate.reshape(shp)
        o_d, s_d = wkv_scan(rf.reshape(shp), decay.reshape(shp), k_d.reshape(shp), vf.reshape(shp),
                            -kk, kk * a_h, s0, rev)
        outs.append(o_d)
        finals.append(s_d)
    o = outs[0] + outs[1]
    mu = jnp.mean(o, axis=-1, keepdims=True)
    var = jnp.mean(jnp.square(o - mu), axis=-1, keepdims=True)
    o = ((o - mu) * lax.rsqrt(var + GN_EPS)).reshape(B, T, RW_WIDTH) * lp['rw_ln_w'] + lp['rw_ln_b']
    bonus = jnp.sum(rf.reshape(shp) * kf.reshape(shp) * lp['rw_rk'], axis=-1, keepdims=True) * vf.reshape(shp)
    o = o + bonus.reshape(B, T, RW_WIDTH)
    gate = jax.nn.sigmoid(x_g.astype(f32)) @ lp['rw_g2']
    return (o * gate).astype(r.dtype), finals[0], finals[1]


def token_mixers(h, lp, k_ctx=None, v_ctx=None, s_ctx=None):
    B, T, _ = h.shape
    qa, ka, va, u_s, v_s, r_c, k_c, v_c, x_w, x_a, x_g = split_proj(h @ lp['w_in'])
    q = qa.reshape(B, T, NA_HEADS, NA_HEAD_DIM)
    k = ka.reshape(B, T, NA_HEADS, NA_HEAD_DIM)
    v = va.reshape(B, T, NA_HEADS, NA_HEAD_DIM)
    if k_ctx is None:
        o_a = na_context(q, k, v).reshape(B, T, NA_WIDTH)
        s0_f = jnp.zeros((B, RW_HEADS, RW_HEAD_DIM, RW_HEAD_DIM), jnp.float32)
        s0_b = s0_f
    else:
        o_a = na_latent(q, k, v, k_ctx, v_ctx, lp['na_rpb'])
        s0_f, s0_b = s_ctx[:, 0], s_ctx[:, 1]
    o_b = spatial_gating(u_s, v_s, lp['sg_gain'], lp['sg_w'], lp['sg_b'])
    o_c, s_f, s_b = rwkv7_mix(r_c, k_c, v_c, x_w, x_a, x_g, s0_f, s0_b, lp)
    mixed = jnp.concatenate([o_a, o_b, o_c], axis=-1) @ lp['w_out']
    return mixed, k, v, jnp.stack([s_f, s_b], axis=1)


def expert_choice_ffn(h, w_router, w_gate, w_up, w_down):
    B, T, D = h.shape
    cap = CAPACITY_FACTOR * T // N_EXPERTS
    aff = jax.nn.softmax((h @ w_router).astype(jnp.float32), axis=-1)
    g, idx = lax.top_k(jnp.swapaxes(aff, 1, 2), cap)
    xs = jax.vmap(lambda hb, ib: hb[ib])(h, idx)
    hid = jax.nn.silu(jnp.einsum('becd,edf->becf', xs, w_gate)) * jnp.einsum('becd,edf->becf', xs, w_up)
    y = jnp.einsum('becf,efd->becd', hid, w_down) * g[..., None].astype(h.dtype)
    return jax.vmap(lambda yb, ib: jnp.zeros((T, D), yb.dtype).at[ib.reshape(-1)].add(yb.reshape(-1, D)))(y, idx)


def trunk_layer(x, cond, lp, k_ctx=None, v_ctx=None, s_ctx=None):
    sh1, sc1, g1, sh2, sc2, g2 = modulation(cond, lp['w_ada'], lp['b_ada'])
    h = rms_norm(x, lp['g_pre_mix']) * (1 + sc1) + sh1
    mixed, k, v, s = token_mixers(h, lp, k_ctx, v_ctx, s_ctx)
    x = x + g1 * rms_norm(mixed, lp['g_post_mix'])
    h = rms_norm(x, lp['g_pre_ffn']) * (1 + sc2) + sh2
    ffn = expert_choice_ffn(h, lp['w_router'], lp['w_gate'], lp['w_up'], lp['w_down'])
    x = x + g2 * rms_norm(ffn, lp['g_post_ffn'])
    return x, k, v, s


def setup_inputs(seed: int = 0) -> dict:
    key = jax.random.key(seed)
    ks = iter(jax.random.split(key, 40))
    L, D = DEPTH, D_MODEL

    def nrm(shape, scale):
        return scale * jax.random.normal(next(ks), shape, jnp.float32)

    return {
        'x_prompt': nrm((BATCH, SEQ, D), 1.0),
        'x_sample': nrm((DEC_BATCH, DEC_SEQ, D), 1.0),
        'cache_k': nrm((DEC_BATCH, L, PAST_LEN, NA_HEADS, NA_HEAD_DIM), 1.0),
        'cache_v': nrm((DEC_BATCH, L, PAST_LEN, NA_HEADS, NA_HEAD_DIM), 1.0),
        'state_wkv': nrm((DEC_BATCH, L, 2, RW_HEADS, RW_HEAD_DIM, RW_HEAD_DIM), 1.0),
        'c': nrm((DEC_BATCH, D), 1.0),
        'c_ctx': nrm((D,), 1.0),
        'w_ada': nrm((L, D, N_MOD * D), 0.5 * D ** -0.5),
        'b_ada': nrm((L, N_MOD * D), 0.02),
        'g_pre_mix': 1.0 + nrm((L, D), 0.02),
        'g_post_mix': 1.0 + nrm((L, D), 0.02),
        'g_pre_ffn': 1.0 + nrm((L, D), 0.02),
        'g_post_ffn': 1.0 + nrm((L, D), 0.02),
        'w_in': nrm((L, D, IN_COLS), D ** -0.5),
        'na_rpb': nrm((L, NA_HEADS, 2 * NA_WIN_ROWS - 1, 2 * NA_WIN_COLS - 1), 0.1),
        'sg_gain': 1.0 + nrm((L, SG_WIDTH), 0.02),
        'sg_w': nrm((L, SG_GROUPS, SG_CHUNK, SG_CHUNK), SG_CHUNK ** -0.5),
        'sg_b': 1.0 + nrm((L, SG_GROUPS, SG_CHUNK), 0.02),
        'rw_w0': jax.random.uniform(next(ks), (L, 2, RW_WIDTH), jnp.float32, -5.0, 0.0),
        'rw_w2': nrm((L, 2, DECAY_LORA, RW_WIDTH), 0.1 * DECAY_LORA ** -0.5),
        'rw_a0': nrm((L, 2, RW_WIDTH), 0.1),
        'rw_a2': nrm((L, 2, AAA_LORA, RW_WIDTH), 0.5 * AAA_LORA ** -0.5),
        'rw_g2': nrm((L, GATE_LORA, RW_WIDTH), GATE_LORA ** -0.5),
        'rw_kk': 0.85 + nrm((L, RW_WIDTH), 0.02),
        'rw_ka': 1.0 + nrm((L, RW_WIDTH), 0.02),
        'rw_rk': nrm((L, RW_HEADS, RW_HEAD_DIM), 0.1),
        'rw_ln_w': 1.0 + nrm((L, RW_WIDTH), 0.02),
        'rw_ln_b': nrm((L, RW_WIDTH), 0.02),
        'w_out': nrm((L, D, D), D ** -0.5),
        'w_router': nrm((L, D, N_EXPERTS), D ** -0.5),
        'w_gate': nrm((L, N_EXPERTS, D, D_EXPERT), D ** -0.5),
        'w_up': nrm((L, N_EXPERTS, D, D_EXPERT), D ** -0.5),
        'w_down': nrm((L, N_EXPERTS, D_EXPERT, D), D_EXPERT ** -0.5),
    }


def reference(x_prompt, x_sample, cache_k, cache_v, state_wkv, c, c_ctx, w_ada, b_ada,
              g_pre_mix, g_post_mix, g_pre_ffn, g_post_ffn, w_in, na_rpb, sg_gain, sg_w, sg_b,
              rw_w0, rw_w2, rw_a0, rw_a2, rw_g2, rw_kk, rw_ka, rw_rk, rw_ln_w, rw_ln_b,
              w_out, w_router, w_gate, w_up, w_down):
    xp, xs = x_prompt, x_sample
    ks, vs, ss = [], [], []
    for l in range(DEPTH):
        lp = dict(w_ada=w_ada[l], b_ada=b_ada[l], g_pre_mix=g_pre_mix[l], g_post_mix=g_post_mix[l],
                  g_pre_ffn=g_pre_ffn[l], g_post_ffn=g_post_ffn[l], w_in=w_in[l], na_rpb=na_rpb[l],
                  sg_gain=sg_gain[l], sg_w=sg_w[l], sg_b=sg_b[l], rw_w0=rw_w0[l], rw_w2=rw_w2[l],
                  rw_a0=rw_a0[l], rw_a2=rw_a2[l], rw_g2=rw_g2[l], rw_kk=rw_kk[l], rw_ka=rw_ka[l],
                  rw_rk=rw_rk[l], rw_ln_w=rw_ln_w[l], rw_ln_b=rw_ln_b[l], w_out=w_out[l],
                  w_router=w_router[l], w_gate=w_gate[l], w_up=w_up[l], w_down=w_down[l])
        xp, k_l, v_l, s_l = trunk_layer(xp, c_ctx, lp)
        ks.append(k_l)
        vs.append(v_l)
        ss.append(s_l.astype(x_prompt.dtype))
        xs, _, _, _ = trunk_layer(xs, c, lp, cache_k[:, l], cache_v[:, l], state_wkv[:, l])
    new_cache_k = jnp.stack(ks, axis=1)
    new_cache_v = jnp.stack(vs, axis=1)
    new_state_wkv = jnp.stack(ss, axis=1)
    return (xp, xs, new_cache_k, new_cache_v, new_state_wkv)
```

```python
import functools
import math

import numpy as np
import jax
import jax.numpy as jnp
from jax import lax
from jax.experimental import pallas as pl
from jax.experimental.pallas import tpu as pltpu

F32 = jnp.float32
BF16 = jnp.bfloat16
HIGHEST = lax.Precision.HIGHEST

HEAD_DIM = 64
LANES = 128
GRID_W = 64
WIN_ROWS = 8
WIN_COLS = 16
SG_CHUNK = 128
SG_GROUP_DIM = 128
N_MOD = 6
CAPACITY_FACTOR = 2
NORM_EPS = 1e-6
GN_EPS = 64e-5
KK_EPS = 1e-12
RW_CHUNK = 64
MASK_NEG = -1e30
MOD_ROWS = 8
VMEM_LIMIT_BYTES = 56 * 1024 * 1024


def _pick(n, prefs):
    for p in prefs:
        if n % p == 0:
            return p
    raise ValueError(f"no tile in {prefs} divides {n}")


def _params(*sem):
    return pltpu.CompilerParams(dimension_semantics=sem, vmem_limit_bytes=VMEM_LIMIT_BYTES)


def _rms(x):
    return x * lax.rsqrt(jnp.mean(x * x, axis=-1, keepdims=True) + NORM_EPS)


def _dot(a, b):
    return jnp.dot(a, b, preferred_element_type=F32)


def _dot_nt(a, b):
    return lax.dot_general(a, b, (((1,), (1,)), ((), ())), preferred_element_type=F32)


def _dot_tn(a, b):
    return lax.dot_general(a, b, (((0,), (0,)), ((), ())), preferred_element_type=F32)


def _dot_hi(a, b):
    return jnp.dot(a, b, preferred_element_type=F32, precision=HIGHEST)


def _iota(shape, axis):
    return lax.broadcasted_iota(jnp.int32, shape, axis)


def _mod_kernel(c_ref, w_ref, b_ref, o_ref):
    c = c_ref[...]
    s = (c * jax.nn.sigmoid(c)).astype(BF16)
    o_ref[...] = _dot(s, w_ref[...].astype(BF16)) + b_ref[...]


def _modulation(cond, w_ada, b_ada):
    L, D, N = w_ada.shape
    tn = _pick(N, (1024, 512, 256, 128))
    return pl.pallas_call(
        _mod_kernel,
        grid=(L, N // tn),
        in_specs=[
            pl.BlockSpec((MOD_ROWS, D), lambda l, j: (0, 0)),
            pl.BlockSpec((None, D, tn), lambda l, j: (l, 0, j)),
            pl.BlockSpec((None, 1, tn), lambda l, j: (l, 0, j)),
        ],
        out_specs=pl.BlockSpec((None, MOD_ROWS, tn), lambda l, j: (l, 0, j)),
        out_shape=jax.ShapeDtypeStruct((L, MOD_ROWS, N), F32),
        compiler_params=_params("arbitrary", "arbitrary"),
        name="modulation",
    )(cond, w_ada, b_ada)


def _inproj_kernel(x_ref, sh_ref, sc_ref, g_ref, w_ref, z_ref, h_scr):
    @pl.when(pl.program_id(1) == 0)
    def _():
        h = _rms(x_ref[...]) * g_ref[...] * (1.0 + sc_ref[...]) + sh_ref[...]
        h_scr[...] = h.astype(BF16)

    z_ref[...] = _dot(h_scr[...], w_ref[...].astype(BF16))


def _in_projection(x, mod3, mod_row, g, w, tm):
    T, D = x.shape
    N = w.shape[1]
    tn = _pick(N, (512, 256, 128))
    return pl.pallas_call(
        _inproj_kernel,
        grid=(T // tm, N // tn),
        in_specs=[
            pl.BlockSpec((tm, D), lambda i, j: (i, 0)),
            pl.BlockSpec((None, 1, D), lambda i, j: (mod_row(i), 0, 0)),
            pl.BlockSpec((None, 1, D), lambda i, j: (mod_row(i), 0, 1)),
            pl.BlockSpec((1, D), lambda i, j: (0, 0)),
            pl.BlockSpec((D, tn), lambda i, j: (0, j)),
        ],
        out_specs=pl.BlockSpec((tm, tn), lambda i, j: (i, j)),
        out_shape=jax.ShapeDtypeStruct((T, N), F32),
        scratch_shapes=[pltpu.VMEM((tm, D), BF16)],
        compiler_params=_params("arbitrary", "arbitrary"),
        name="in_projection",
    )(x, mod3, mod3, g, w)


def _attn_ctx_kernel(q_ref, k_ref, v_ref, o_ref, *, heads):
    scale = HEAD_DIM ** -0.5
    outs = []
    for h in range(heads):
        sl = slice(h * HEAD_DIM, (h + 1) * HEAD_DIM)
        q = (q_ref[:, sl] * scale).astype(BF16)
        s = _dot_nt(q, k_ref[:, sl].astype(BF16))
        p = jnp.exp(s - jnp.max(s, axis=-1, keepdims=True))
        l = jnp.sum(p, axis=-1, keepdims=True)
        outs.append(_dot(p.astype(BF16), v_ref[:, sl].astype(BF16)) / l)
    o_ref[...] = jnp.concatenate(outs, axis=-1)


def _attention_context(q, k, v, seq):
    T, W = q.shape
    spec = pl.BlockSpec((seq, W), lambda b: (b, 0))
    return pl.pallas_call(
        functools.partial(_attn_ctx_kernel, heads=W // HEAD_DIM),
        grid=(T // seq,),
        in_specs=[spec, spec, spec],
        out_specs=spec,
        out_shape=jax.ShapeDtypeStruct((T, W), F32),
        compiler_params=_params("arbitrary"),
        name="attention_context",
    )(q, k, v)


def _attn_lat_kernel(q_ref, k_ref, v_ref, kc_ref, vc_ref, bias_ref, o_ref, *, heads, rows, kr):
    scale = HEAD_DIM ** -0.5
    r = pl.program_id(1)
    r0 = jnp.clip(r - kr // 2, 0, rows - kr)
    start = pl.multiple_of(r0 * GRID_W, GRID_W)
    k_win = k_ref[pl.ds(start, kr * GRID_W), :]
    v_win = v_ref[pl.ds(start, kr * GRID_W), :]
    outs = []
    for h in range(heads):
        sl = slice(h * HEAD_DIM, (h + 1) * HEAD_DIM)
        q = (q_ref[:, sl] * scale).astype(BF16)
        s_loc = _dot_nt(q, k_win[:, sl].astype(BF16)) + bias_ref[h]
        s_ctx = _dot_nt(q, kc_ref[:, sl].astype(BF16))
        m = jnp.maximum(jnp.max(s_loc, axis=-1, keepdims=True), jnp.max(s_ctx, axis=-1, keepdims=True))
        p_loc = jnp.exp(s_loc - m)
        p_ctx = jnp.exp(s_ctx - m)
        l = jnp.sum(p_loc, axis=-1, keepdims=True) + jnp.sum(p_ctx, axis=-1, keepdims=True)
        o = _dot(p_loc.astype(BF16), v_win[:, sl].astype(BF16)) + _dot(p_ctx.astype(BF16), vc_ref[:, sl].astype(BF16))
        outs.append(o / l)
    o_ref[...] = jnp.concatenate(outs, axis=-1)


def _window_bias(rpb, kr):
    delta = np.arange(kr)[:, None]
    i = np.arange(kr)[None, :]
    row_off = (WIN_ROWS - 1) - delta + i
    q = np.arange(GRID_W)[:, None]
    kc = np.arange(GRID_W)[None, :]
    c0 = np.clip(q - WIN_COLS // 2, 0, GRID_W - WIN_COLS)
    valid = (kc >= c0) & (kc < c0 + WIN_COLS)
    col_off = np.clip(kc - q + (WIN_COLS - 1), 0, 2 * WIN_COLS - 2)
    t = rpb.astype(F32)[:, row_off[:, :, None, None], col_off[None, None, :, :]]
    t = jnp.where(valid[None, None, None], t, MASK_NEG)
    t = t.transpose(1, 0, 3, 2, 4)
    return t.reshape(kr, rpb.shape[0], GRID_W, kr * GRID_W)


def _attention_latent(q, k, v, k_ctx, v_ctx, bias):
    B, T, W = q.shape
    P = k_ctx.shape[1]
    rows = T // GRID_W
    kr = bias.shape[0]
    heads = W // HEAD_DIM

    def bias_index(b, r):
        return (r - jnp.clip(r - kr // 2, 0, rows - kr), 0, 0, 0)

    return pl.pallas_call(
        functools.partial(_attn_lat_kernel, heads=heads, rows=rows, kr=kr),
        grid=(B, rows),
        in_specs=[
            pl.BlockSpec((None, GRID_W, W), lambda b, r: (b, r, 0)),
            pl.BlockSpec((None, T, W), lambda b, r: (b, 0, 0)),
            pl.BlockSpec((None, T, W), lambda b, r: (b, 0, 0)),
            pl.BlockSpec((None, P, W), lambda b, r: (b, 0, 0)),
            pl.BlockSpec((None, P, W), lambda b, r: (b, 0, 0)),
            pl.BlockSpec((None, heads, GRID_W, kr * GRID_W), bias_index),
        ],
        out_specs=pl.BlockSpec((None, GRID_W, W), lambda b, r: (b, r, 0)),
        out_shape=jax.ShapeDtypeStruct((B, T, W), F32),
        compiler_params=_params("arbitrary", "arbitrary"),
        name="attention_latent",
    )(q, k, v, k_ctx, v_ctx, bias)


def _sgu_kernel(u_ref, v_ref, gain_ref, w_ref, b_ref, o_ref, *, groups, chunks):
    for c in range(chunks):
        rows = slice(c * SG_CHUNK, (c + 1) * SG_CHUNK)
        for g in range(groups):
            sl = slice(g * SG_GROUP_DIM, (g + 1) * SG_GROUP_DIM)
            vn = _rms(v_ref[rows, sl]) * gain_ref[:, sl]
            mixed = _dot(w_ref[g].astype(BF16), vn.astype(BF16)) + b_ref[:, g:g + 1]
            o_ref[rows, sl] = u_ref[rows, sl] * mixed


def _spatial_gating(u, v, gain, w_s, b_t):
    T, W = u.shape
    G = w_s.shape[0]
    tm = _pick(T, (512, 256, 128))
    spec = pl.BlockSpec((tm, W), lambda i: (i, 0))
    return pl.pallas_call(
        functools.partial(_sgu_kernel, groups=G, chunks=tm // SG_CHUNK),
        grid=(T // tm,),
        in_specs=[spec, spec,
                  pl.BlockSpec((1, W), lambda i: (0, 0)),
                  pl.BlockSpec((G, SG_CHUNK, SG_CHUNK), lambda i: (0, 0, 0)),
                  pl.BlockSpec((SG_CHUNK, G), lambda i: (0, 0))],
        out_specs=spec,
        out_shape=jax.ShapeDtypeStruct((T, W), F32),
        compiler_params=_params("arbitrary"),
        name="spatial_gating",
    )(u, v, gain, w_s, b_t)


def _softplus(y):
    return jnp.maximum(y, 0.0) + jnp.log(1.0 + jnp.exp(-jnp.abs(y)))


def _rwkv_pre_kernel(r_ref, k_ref, v_ref, xw_ref, xa_ref, w0_ref, w2_ref, a0_ref, a2_ref, kkw_ref, ka_ref,
                     bd_ref, tri_ref,
                     at_f, rt_f, bh_f, kh_f, wc_f, at_b, rt_b, bh_b, kh_b, wc_b, v_o, *, pairs, chunks):
    C = RW_CHUNK
    r = r_ref[...]
    k = k_ref[...]
    bd = bd_ref[...]
    kk = k * kkw_ref[...]
    kk2 = kk * kk
    ssq = jnp.concatenate([_dot_hi(kk2[:, p * LANES:(p + 1) * LANES], bd) for p in range(pairs)], axis=-1)
    kk = kk * lax.rsqrt(ssq + KK_EPS)
    tw = jnp.tanh(xw_ref[...]).astype(BF16)
    xa = xa_ref[...].astype(BF16)
    v_bf = v_ref[...].astype(BF16)
    for p in range(pairs):
        v_o[p] = v_bf[:, p * LANES:(p + 1) * LANES]
    outs = ((at_f, rt_f, bh_f, kh_f, wc_f), (at_b, rt_b, bh_b, kh_b, wc_b))
    for d in range(2):
        at_o, rt_o, bh_o, kh_o, wc_o = outs[d]
        w_log = -_softplus(-(w0_ref[d:d + 1, :] + _dot(tw, w2_ref[d].astype(BF16)))) - 0.5
        logw = -jnp.exp(w_log)
        a_rate = jax.nn.sigmoid(a0_ref[d:d + 1, :] + _dot(xa, a2_ref[d].astype(BF16)))
        k_d = k * (1.0 + (a_rate - 1.0) * ka_ref[...])
        b = kk * a_rate
        tri = tri_ref[d]
        tot_row = C - 1 if d == 0 else 0
        for c in range(chunks):
            rows = slice(c * C, (c + 1) * C)
            lw = logw[rows]
            cum = _dot_hi(tri, lw)
            tot = cum[tot_row:tot_row + 1, :]
            e_dn = jnp.exp(tot - cum)
            at = -kk[rows] * jnp.exp(cum - lw - tot)
            rt = r[rows] * jnp.exp(cum - tot)
            bh = b[rows] * e_dn
            kh = k_d[rows] * e_dn
            wc = jnp.exp(tot)
            for p in range(pairs):
                sl = slice(p * LANES, (p + 1) * LANES)
                at_o[p, rows, :] = at[:, sl].astype(BF16)
                rt_o[p, rows, :] = rt[:, sl].astype(BF16)
                bh_o[p, rows, :] = bh[:, sl].astype(BF16)
                kh_o[p, rows, :] = kh[:, sl].astype(BF16)
                wc_o[c, p] = wc[:, sl]


def _rwkv_prepare(r, k, v, xw, xa, w0, w2, a0, a2, kkw, ka, bd, tri):
    T, RW = r.shape
    pairs = RW // LANES
    C = RW_CHUNK
    tm = _pick(T, (256, 128, 64))
    tok = pl.BlockSpec((tm, RW), lambda i: (i, 0))
    lw_w = xw.shape[1]
    la_w = xa.shape[1]
    packed = pl.BlockSpec((pairs, tm, LANES), lambda i: (0, i, 0))
    wc_spec = pl.BlockSpec((tm // C, pairs, 1, LANES), lambda i: (i, 0, 0, 0))
    packed_shape = jax.ShapeDtypeStruct((pairs, T, LANES), BF16)
    wc_shape = jax.ShapeDtypeStruct((T // C, pairs, 1, LANES), F32)
    full = lambda a: pl.BlockSpec(a.shape, lambda i: (0,) * a.ndim)
    return pl.pallas_call(
        functools.partial(_rwkv_pre_kernel, pairs=pairs, chunks=tm // C),
        grid=(T // tm,),
        in_specs=[tok, tok, tok,
                  pl.BlockSpec((tm, lw_w), lambda i: (i, 0)),
                  pl.BlockSpec((tm, la_w), lambda i: (i, 0)),
                  full(w0), full(w2), full(a0), full(a2), full(kkw), full(ka), full(bd), full(tri)],
        out_specs=[packed, packed, packed, packed, wc_spec, packed, packed, packed, packed, wc_spec, packed],
        out_shape=[packed_shape] * 4 + [wc_shape] + [packed_shape] * 4 + [wc_shape] + [packed_shape],
        compiler_params=_params("arbitrary"),
        name="rwkv_prepare",
    )(r, k, v, xw, xa, w0, w2, a0, a2, kkw, ka, bd, tri)


def _rwkv_head_chunk(at, rt, bh, kh, v, hs, strict, incl, eye):
    C = RW_CHUNK
    ar = jnp.concatenate([at, rt], axis=0)
    bk = jnp.concatenate([bh, kh], axis=0)
    a = _dot_nt(ar, bk)
    l = jnp.where(strict, a[:C, :C], 0.0)
    a_ak = jnp.where(strict, a[:C, C:], 0.0).astype(BF16)
    a_rb = jnp.where(incl, a[C:, :C], 0.0).astype(BF16)
    a_rk = jnp.where(incl, a[C:, C:], 0.0).astype(BF16)
    t = eye + l
    pw = l.astype(BF16)
    for _ in range(int(math.log2(C)) - 1):
        pw = _dot(pw, pw).astype(BF16)
        t = t + _dot(t.astype(BF16), pw)
    hs_b = hs.astype(BF16)
    x2 = _dot(jnp.concatenate([at, a_ak], axis=1), jnp.concatenate([hs_b, v], axis=0))
    u = _dot(t.astype(BF16), x2.astype(BF16))
    uv = jnp.concatenate([u.astype(BF16), v], axis=0)
    o = _dot(jnp.concatenate([rt, a_rb, a_rk], axis=1), jnp.concatenate([hs_b, uv], axis=0))
    h_new = hs + _dot_tn(bk, uv)
    return o, h_new


def _rwkv_core_kernel(at_f, rt_f, bh_f, kh_f, v_f, wc_f, at_b, rt_b, bh_b, kh_b, v_b, wc_b, h0_ref,
                      of_ref, ob_ref, hs_ref, *, pairs):
    C = RW_CHUNK

    @pl.when(pl.program_id(1) == 0)
    def _():
        hs_ref[...] = h0_ref[...]

    row = _iota((C, C), 0)
    col = _iota((C, C), 1)
    eye = jnp.where(row == col, 1.0, 0.0).astype(F32)
    eye_l = _iota((LANES, LANES), 0) == _iota((LANES, LANES), 1)
    dirs = ((at_f, rt_f, bh_f, kh_f, v_f, wc_f, of_ref, col < row, col <= row),
            (at_b, rt_b, bh_b, kh_b, v_b, wc_b, ob_ref, col > row, col >= row))
    for d in range(2):
        at_r, rt_r, bh_r, kh_r, v_r, wc_r, o_r, strict, incl = dirs[d]

        def pair_body(p, carry, at_r=at_r, rt_r=rt_r, bh_r=bh_r, kh_r=kh_r, v_r=v_r, wc_r=wc_r, o_r=o_r,
                      strict=strict, incl=incl, d=d):
            at, rt, bh, kh, v = at_r[p], rt_r[p], bh_r[p], kh_r[p], v_r[p]
            wcol = jnp.sum(jnp.where(eye_l, jnp.broadcast_to(wc_r[p], (LANES, LANES)), 0.0), axis=1, keepdims=True)
            outs = []
            for j in range(LANES // HEAD_DIM):
                sl = slice(j * HEAD_DIM, (j + 1) * HEAD_DIM)
                hd = 2 * p + j
                hs = hs_ref[d, hd] * wcol[sl, :]
                o, h_new = _rwkv_head_chunk(at[:, sl], rt[:, sl], bh[:, sl], kh[:, sl], v[:, sl], hs, strict, incl, eye)
                hs_ref[d, hd] = h_new
                outs.append(o)
            o_r[p] = jnp.concatenate(outs, axis=-1)
            return carry

        lax.fori_loop(0, pairs, pair_body, 0)


def _rwkv_core(pre, h0, n_seq, seq_len, tok_off):
    at_f, rt_f, bh_f, kh_f, wc_f, at_b, rt_b, bh_b, kh_b, wc_b, v = pre
    pairs, T, _ = at_f.shape
    C = RW_CHUNK
    nch = seq_len // C
    off = tok_off // C
    heads = h0.shape[2]
    fwd = lambda s, c: off + s * nch + c
    bwd = lambda s, c: off + s * nch + (nch - 1 - c)
    pk = lambda f: pl.BlockSpec((pairs, C, LANES), lambda s, c: (0, f(s, c), 0))
    wc = lambda f: pl.BlockSpec((None, pairs, 1, LANES), lambda s, c: (f(s, c), 0, 0, 0))
    st = pl.BlockSpec((None, 2, heads, HEAD_DIM, HEAD_DIM), lambda s, c: (s, 0, 0, 0, 0))
    o_shape = jax.ShapeDtypeStruct((pairs, n_seq * seq_len, LANES), F32)
    o_f = pl.BlockSpec((pairs, C, LANES), lambda s, c: (0, s * nch + c, 0))
    o_b = pl.BlockSpec((pairs, C, LANES), lambda s, c: (0, s * nch + (nch - 1 - c), 0))
    return pl.pallas_call(
        functools.partial(_rwkv_core_kernel, pairs=pairs),
        grid=(n_seq, nch),
        in_specs=[pk(fwd), pk(fwd), pk(fwd), pk(fwd), pk(fwd), wc(fwd),
                  pk(bwd), pk(bwd), pk(bwd), pk(bwd), pk(bwd), wc(bwd), st],
        out_specs=[o_f, o_b, st],
        out_shape=[o_shape, o_shape, jax.ShapeDtypeStruct(h0.shape, F32)],
        compiler_params=_params("arbitrary", "arbitrary"),
        name="rwkv_scan",
    )(at_f, rt_f, bh_f, kh_f, v, wc_f, at_b, rt_b, bh_b, kh_b, v, wc_b, h0)


def _rwkv_post_kernel(of_ref, ob_ref, r_ref, k_ref, v_ref, xg_ref, g2_ref, lnw_ref, lnb_ref, rk_ref, bd_ref,
                      o_ref, *, pairs):
    bd = bd_ref[...]
    sg = jax.nn.sigmoid(xg_ref[...]).astype(BF16)
    inv = 1.0 / HEAD_DIM
    for p in range(pairs):
        sl = slice(p * LANES, (p + 1) * LANES)
        o = of_ref[p] + ob_ref[p]
        mu = _dot_hi(o, bd) * inv
        dlt = o - mu
        var = _dot_hi(dlt * dlt, bd) * inv
        on = dlt * lax.rsqrt(var + GN_EPS) * lnw_ref[:, sl] + lnb_ref[:, sl]
        bonus = _dot_hi(r_ref[:, sl] * k_ref[:, sl] * rk_ref[:, sl], bd) * v_ref[:, sl]
        gate = _dot(sg, g2_ref[:, sl].astype(BF16))
        o_ref[:, sl] = (on + bonus) * gate


def _rwkv_finish(o_f, o_b, r, k, v, xg, g2, lnw, lnb, rk, bd):
    T, RW = r.shape
    pairs = RW // LANES
    tm = _pick(T, (256, 128, 64))
    tok = pl.BlockSpec((tm, RW), lambda i: (i, 0))
    packed = pl.BlockSpec((pairs, tm, LANES), lambda i: (0, i, 0))
    full = lambda a: pl.BlockSpec(a.shape, lambda i: (0,) * a.ndim)
    return pl.pallas_call(
        functools.partial(_rwkv_post_kernel, pairs=pairs),
        grid=(T // tm,),
        in_specs=[packed, packed, tok, tok, tok, pl.BlockSpec((tm, xg.shape[1]), lambda i: (i, 0)),
                  full(g2), full(lnw), full(lnb), full(rk), full(bd)],
        out_specs=tok,
        out_shape=jax.ShapeDtypeStruct((T, RW), F32),
        compiler_params=_params("arbitrary"),
        name="rwkv_finish",
    )(o_f, o_b, r, k, v, xg, g2, lnw, lnb, rk, bd)


def _outproj_kernel(oa_ref, ob_ref, oc_ref, x_ref, w_ref, g1_ref, sh2_ref, sc2_ref, gpost_ref, gpre_ref, wr_ref,
                    xo_ref, h2_ref, aff_ref, cat_scr, acc_scr, *, tn):
    j = pl.program_id(1)

    @pl.when(j == 0)
    def _():
        cat_scr[...] = jnp.concatenate([oa_ref[...], ob_ref[...], oc_ref[...]], axis=-1).astype(BF16)

    acc_scr[j] = _dot(cat_scr[...], w_ref[...].astype(BF16))

    @pl.when(j == pl.num_programs(1) - 1)
    def _():
        mixed = jnp.concatenate([acc_scr[n] for n in range(acc_scr.shape[0])], axis=-1)
        x = x_ref[...] + g1_ref[...] * (_rms(mixed) * gpost_ref[...])
        xo_ref[...] = x
        h2 = _rms(x) * gpre_ref[...] * (1.0 + sc2_ref[...]) + sh2_ref[...]
        h2_ref[...] = h2.astype(BF16)
        logits = _dot_hi(h2, wr_ref[...])
        e = jnp.exp(logits - jnp.max(logits, axis=-1, keepdims=True))
        aff_ref[...] = e / jnp.sum(e, axis=-1, keepdims=True)


def _out_projection(o_a, o_b, o_c, x, w_out, mod3, mod_row, g_post, g_pre, w_router, tm):
    T, D = x.shape
    E = w_router.shape[1]
    tn = _pick(D, (512, 256, 128))
    row = lambda a: pl.BlockSpec((tm, a.shape[1]), lambda i, j: (i, 0))
    mod = lambda col: pl.BlockSpec((None, 1, D), lambda i, j: (mod_row(i), 0, col))
    vec = pl.BlockSpec((1, D), lambda i, j: (0, 0))
    return pl.pallas_call(
        functools.partial(_outproj_kernel, tn=tn),
        grid=(T // tm, D // tn),
        in_specs=[row(o_a), row(o_b), row(o_c), row(x),
                  pl.BlockSpec((D, tn), lambda i, j: (0, j)),
                  mod(2), mod(3), mod(4), vec, vec,
                  pl.BlockSpec((D, E), lambda i, j: (0, 0))],
        out_specs=[pl.BlockSpec((tm, D), lambda i, j: (i, 0)),
                   pl.BlockSpec((tm, D), lambda i, j: (i, 0)),
                   pl.BlockSpec((tm, E), lambda i, j: (i, 0))],
        out_shape=[jax.ShapeDtypeStruct((T, D), F32), jax.ShapeDtypeStruct((T, D), BF16),
                   jax.ShapeDtypeStruct((T, E), F32)],
        scratch_shapes=[pltpu.VMEM((tm, D), BF16), pltpu.VMEM((D // tn, tm, tn), F32)],
        compiler_params=_params("arbitrary", "arbitrary"),
        name="out_projection",
    )(o_a, o_b, o_c, x, w_out, mod3, mod3, mod3, g_post, g_pre, w_router)


def _select_kernel(a_ref, ut_ref, pos_ref, *, cap):
    a = a_ref[...]
    R, T = a.shape
    bits = lax.bitcast_convert_type(a, jnp.int32)
    lo = jnp.zeros((R, 1), jnp.int32)
    for bit in range(30, -1, -1):
        cand = lo | (1 << bit)
        cnt = jnp.sum(jnp.where(bits >= cand, 1.0, 0.0), axis=1, keepdims=True)
        lo = jnp.where(cnt >= cap, cand, lo)
    gt = bits > lo
    eq = bits == lo
    need = cap - jnp.sum(jnp.where(gt, 1.0, 0.0), axis=1, keepdims=True)
    ut = ut_ref[...]
    nblk = T // LANES
    tie_carry = jnp.zeros((R, 1), F32)
    pos_carry = jnp.zeros((R, 1), F32)
    for n in range(nblk):
        sl = slice(n * LANES, (n + 1) * LANES)
        eq_b = jnp.where(eq[:, sl], 1.0, 0.0)
        tie_rank = _dot(eq_b.astype(BF16), ut) + tie_carry
        tie_carry = tie_carry + jnp.sum(eq_b, axis=1, keepdims=True)
        sel = jnp.where(gt[:, sl], 1.0, jnp.where(tie_rank < need, eq_b, 0.0))
        pos = _dot(sel.astype(BF16), ut) + pos_carry
        pos_carry = pos_carry + jnp.sum(sel, axis=1, keepdims=True)
        pos_ref[:, sl] = jnp.where(sel > 0.5, pos, -1.0).astype(jnp.int32)


def _select(aff_rows, cap, ut):
    R, T = aff_rows.shape
    tr = _pick(R, (128, 64, 32, 16, 8))
    return pl.pallas_call(
        functools.partial(_select_kernel, cap=cap),
        grid=(R // tr,),
        in_specs=[pl.BlockSpec((tr, T), lambda i: (i, 0)), pl.BlockSpec((LANES, LANES), lambda i: (0, 0))],
        out_specs=pl.BlockSpec((tr, T), lambda i: (i, 0)),
        out_shape=jax.ShapeDtypeStruct((R, T), jnp.int32),
        compiler_params=_params("arbitrary"),
        name="expert_select",
    )(aff_rows, ut)


def _one_hot_rows(pos_row, cap):
    T = pos_row.shape[1]
    return jnp.where(_iota((cap, T), 0) == pos_row, 1.0, 0.0).astype(BF16)


def _gather_ctx_kernel(h_ref, pos_ref, xs_ref, *, experts, cap):
    h = h_ref[...]
    for e in range(experts):
        xs_ref[e] = _dot(_one_hot_rows(pos_ref[e:e + 1, :], cap), h).astype(BF16)


def _gather_lat_kernel(h_ref, pos_ref, xs_ref, *, cap):
    xs_ref[...] = _dot(_one_hot_rows(pos_ref[...], cap), h_ref[...]).astype(BF16)


def _expert_up_kernel(xs_ref, wg_ref, wu_ref, hid_ref):
    x = xs_ref[...]
    g = _dot(x, wg_ref[...].astype(BF16))
    u = _dot(x, wu_ref[...].astype(BF16))
    hid_ref[...] = (g * jax.nn.sigmoid(g) * u).astype(BF16)


def _expert_down_kernel(hid_ref, wd_ref, y_ref):
    y_ref[...] = _dot(hid_ref[...], wd_ref[...].astype(BF16)).astype(BF16)


def _scatter_term(pos_col, gate_col, y, cap):
    tm = pos_col.shape[0]
    pt = jnp.where(_iota((tm, cap), 1).astype(F32) == pos_col, 1.0, 0.0).astype(BF16)
    return gate_col * _dot(pt, y)


def _ffn_residual(x, ffn, g2, g_post):
    return x + g2 * (_rms(ffn) * g_post)


def _combine_ctx_kernel(x_ref, y_ref, pos_ref, aff_ref, g2_ref, gpost_ref, o_ref, *, experts, cap):
    pos = pos_ref[...].astype(F32)
    aff = aff_ref[...]
    acc = jnp.zeros(x_ref.shape, F32)
    for e in range(experts):
        acc = acc + _scatter_term(pos[:, e:e + 1], aff[:, e:e + 1], y_ref[e], cap)
    o_ref[...] = _ffn_residual(x_ref[...], acc, g2_ref[...], gpost_ref[...])


def _combine_lat_kernel(x_ref, y_ref, pos_ref, aff_ref, g2_ref, gpost_ref, o_ref, acc_scr, *, cap):
    e = pl.program_id(2)

    @pl.when(e == 0)
    def _():
        acc_scr[...] = jnp.zeros(acc_scr.shape, F32)

    mine = _iota(pos_ref.shape, 1) == e
    pos_col = jnp.sum(jnp.where(mine, pos_ref[...].astype(F32), 0.0), axis=1, keepdims=True)
    gate_col = jnp.sum(jnp.where(mine, aff_ref[...], 0.0), axis=1, keepdims=True)
    acc_scr[...] += _scatter_term(pos_col, gate_col, y_ref[...], cap)

    @pl.when(e == pl.num_programs(2) - 1)
    def _():
        o_ref[...] = _ffn_residual(x_ref[...], acc_scr[...], g2_ref[...], gpost_ref[...])


def _expert_ffn(x, h2, aff, mod3, g_post, w_gate, w_up, w_down, ut, dims):
    batch, seq, dec_batch, dec_seq = dims
    T, D = x.shape
    E, _, F = w_gate.shape
    n_ctx = batch * seq
    cap_c = CAPACITY_FACTOR * seq // E
    cap_l = CAPACITY_FACTOR * dec_seq // E
    rows_c = batch * cap_c
    rows = rows_c + dec_batch * cap_l
    assert n_ctx % dec_seq == 0 and rows_c % cap_l == 0
    lat_blk = n_ctx // dec_seq
    lat_row_blk = rows_c // cap_l

    aff_c = aff[:n_ctx].reshape(batch, seq, E).transpose(0, 2, 1)
    aff_l = aff[n_ctx:].reshape(dec_batch, dec_seq, E).transpose(0, 2, 1)
    pos_c = _select(aff_c.reshape(batch * E, seq), cap_c, ut).reshape(batch, E, seq)
    pos_l = _select(aff_l.reshape(dec_batch * E, dec_seq), cap_l, ut).reshape(dec_batch, E, 1, dec_seq)
    pos_t = jnp.concatenate([pos_c.transpose(0, 2, 1).reshape(n_ctx, E),
                             pos_l.reshape(dec_batch, E, dec_seq).transpose(0, 2, 1).reshape(T - n_ctx, E)], axis=0)

    xs_shape = jax.ShapeDtypeStruct((E, rows, D), BF16)
    xs_c = pl.pallas_call(
        functools.partial(_gather_ctx_kernel, experts=E, cap=cap_c),
        grid=(batch,),
        in_specs=[pl.BlockSpec((seq, D), lambda b: (b, 0)), pl.BlockSpec((None, E, seq), lambda b: (b, 0, 0))],
        out_specs=pl.BlockSpec((E, cap_c, D), lambda b: (0, b, 0)),
        out_shape=jax.ShapeDtypeStruct((E, rows_c, D), BF16),
        compiler_params=_params("arbitrary"),
        name="expert_gather_context",
    )(h2, pos_c)
    xs_l = pl.pallas_call(
        functools.partial(_gather_lat_kernel, cap=cap_l),
        grid=(dec_batch, E),
        in_specs=[pl.BlockSpec((dec_seq, D), lambda b, e: (lat_blk + b, 0)),
                  pl.BlockSpec((None, None, 1, dec_seq), lambda b, e: (b, e, 0, 0))],
        out_specs=pl.BlockSpec((None, cap_l, D), lambda b, e: (e, b, 0)),
        out_shape=jax.ShapeDtypeStruct((E, dec_batch * cap_l, D), BF16),
        compiler_params=_params("arbitrary", "arbitrary"),
        name="expert_gather_latent",
    )(h2, pos_l)
    xs = jnp.concatenate([xs_c, xs_l], axis=1)
    del xs_shape

    tf = _pick(F, (512, 256, 128))
    hid = pl.pallas_call(
        _expert_up_kernel,
        grid=(E, F // tf),
        in_specs=[pl.BlockSpec((None, rows, D), lambda e, n: (e, 0, 0)),
                  pl.BlockSpec((None, D, tf), lambda e, n: (e, 0, n)),
                  pl.BlockSpec((None, D, tf), lambda e, n: (e, 0, n))],
        out_specs=pl.BlockSpec((None, rows, tf), lambda e, n: (e, 0, n)),
        out_shape=jax.ShapeDtypeStruct((E, rows, F), BF16),
        compiler_params=_params("arbitrary", "arbitrary"),
        name="expert_up",
    )(xs, w_gate, w_up)
    td = _pick(D, (512, 256, 128))
    y = pl.pallas_call(
        _expert_down_kernel,
        grid=(E, D // td),
        in_specs=[pl.BlockSpec((None, rows, F), lambda e, n: (e, 0, 0)),
                  pl.BlockSpec((None, F, td), lambda e, n: (e, 0, n))],
        out_specs=pl.BlockSpec((None, rows, td), lambda e, n: (e, 0, n)),
        out_shape=jax.ShapeDtypeStruct((E, rows, D), BF16),
        compiler_params=_params("arbitrary", "arbitrary"),
        name="expert_down",
    )(hid, w_down)

    vec = lambda nd: pl.BlockSpec((1, D), lambda *i: (0, 0))
    out_c = pl.pallas_call(
        functools.partial(_combine_ctx_kernel, experts=E, cap=cap_c),
        grid=(batch,),
        in_specs=[pl.BlockSpec((seq, D), lambda b: (b, 0)),
                  pl.BlockSpec((E, cap_c, D), lambda b: (0, b, 0)),
                  pl.BlockSpec((seq, E), lambda b: (b, 0)),
                  pl.BlockSpec((seq, E), lambda b: (b, 0)),
                  pl.BlockSpec((None, 1, D), lambda b: (0, 0, 5)),
                  vec(1)],
        out_specs=pl.BlockSpec((seq, D), lambda b: (b, 0)),
        out_shape=jax.ShapeDtypeStruct((n_ctx, D), F32),
        compiler_params=_params("arbitrary"),
        name="expert_combine_context",
    )(x, y, pos_t, aff, mod3, g_post)
    tm = _pick(dec_seq, (512, 256, 128))
    nt = dec_seq // tm
    tok = lambda b, i, e: (lat_blk * nt + b * nt + i, 0)
    out_l = pl.pallas_call(
        functools.partial(_combine_lat_kernel, cap=cap_l),
        grid=(dec_batch, nt, E),
        in_specs=[pl.BlockSpec((tm, D), tok),
                  pl.BlockSpec((None, cap_l, D), lambda b, i, e: (e, lat_row_blk + b, 0)),
                  pl.BlockSpec((tm, E), tok),
                  pl.BlockSpec((tm, E), tok),
                  pl.BlockSpec((None, 1, D), lambda b, i, e: (1 + b, 0, 5)),
                  vec(3)],
        out_specs=pl.BlockSpec((tm, D), lambda b, i, e: (b * nt + i, 0)),
        out_shape=jax.ShapeDtypeStruct((T - n_ctx, D), F32),
        scratch_shapes=[pltpu.VMEM((tm, D), F32)],
        compiler_params=_params("arbitrary", "arbitrary", "arbitrary"),
        name="expert_combine_latent",
    )(x, y, pos_t, aff, mod3, g_post)
    return jnp.concatenate([out_c, out_l], axis=0)


def kernel(x_prompt, x_sample, cache_k, cache_v, state_wkv, c, c_ctx, w_ada, b_ada, g_pre_mix, g_post_mix, g_pre_ffn, g_post_ffn, w_in, na_rpb, sg_gain, sg_w, sg_b, rw_w0, rw_w2, rw_a0, rw_a2, rw_g2, rw_kk, rw_ka, rw_rk, rw_ln_w, rw_ln_b, w_out, w_router, w_gate, w_up, w_down):
    batch, seq, D = x_prompt.shape
    dec_batch, dec_seq, _ = x_sample.shape
    depth = w_ada.shape[0]
    past = cache_k.shape[2]
    na_heads = cache_k.shape[3]
    naw = na_heads * HEAD_DIM
    sgw = sg_gain.shape[1]
    rw_heads = state_wkv.shape[3]
    rww = rw_heads * HEAD_DIM
    lora_w, lora_a, lora_g = rw_w2.shape[2], rw_a2.shape[2], rw_g2.shape[1]
    n_ctx = batch * seq
    n_lat = dec_batch * dec_seq
    assert 1 + dec_batch <= MOD_ROWS and rww % LANES == 0
    assert seq % RW_CHUNK == 0 and dec_seq % RW_CHUNK == 0 and dec_seq % GRID_W == 0

    tm = _pick(math.gcd(n_ctx, dec_seq), (1024, 512, 256, 128))
    tm_out = min(tm, 512)

    def mod_row_for(tile):
        def mod_row(i):
            t0 = i * tile
            return jnp.where(t0 < n_ctx, 0, 1 + (t0 - n_ctx) // dec_seq)
        return mod_row

    pair_ones = jnp.asarray(np.kron(np.eye(LANES // HEAD_DIM), np.ones((HEAD_DIM, HEAD_DIM))), F32)
    tril = np.tril(np.ones((RW_CHUNK, RW_CHUNK)))
    tri = jnp.asarray(np.stack([tril, tril.T]), F32)
    ut = jnp.asarray(np.triu(np.ones((LANES, LANES)), 1), BF16)
    kr = min(WIN_ROWS, dec_seq // GRID_W)

    cond = jnp.zeros((MOD_ROWS, D), F32).at[0].set(c_ctx).at[1:1 + dec_batch].set(c)
    mods = _modulation(cond, w_ada, b_ada.reshape(depth, 1, N_MOD * D))

    x = jnp.concatenate([x_prompt.reshape(n_ctx, D), x_sample.reshape(n_lat, D)], axis=0)
    h0_ctx = jnp.zeros((batch, 2, rw_heads, HEAD_DIM, HEAD_DIM), F32)
    ks, vs, ss = [], [], []
    for l in range(depth):
        mod3 = mods[l].reshape(MOD_ROWS, 1, N_MOD * D)
        z = _in_projection(x, mod3, mod_row_for(tm), g_pre_mix[l].reshape(1, D), w_in[l], tm)
        cuts = np.cumsum([0, naw, naw, naw, sgw, sgw, rww, rww, rww, lora_w, lora_a, lora_g])
        qa, ka, va, u_s, v_s, r_c, k_c, v_c, x_w, x_a, x_g = [z[:, a:b] for a, b in zip(cuts[:-1], cuts[1:])]

        o_a_ctx = _attention_context(qa[:n_ctx], ka[:n_ctx], va[:n_ctx], seq)
        lat3 = lambda t: t[n_ctx:].reshape(dec_batch, dec_seq, naw)
        o_a_lat = _attention_latent(lat3(qa), lat3(ka), lat3(va),
                                    cache_k[:, l].reshape(dec_batch, past, naw),
                                    cache_v[:, l].reshape(dec_batch, past, naw),
                                    _window_bias(na_rpb[l], kr))
        o_a = jnp.concatenate([o_a_ctx, o_a_lat.reshape(n_lat, naw)], axis=0)
        ks.append(ka[:n_ctx].reshape(batch, seq, na_heads, HEAD_DIM))
        vs.append(va[:n_ctx].reshape(batch, seq, na_heads, HEAD_DIM))

        o_b = _spatial_gating(u_s, v_s, sg_gain[l].reshape(1, sgw), sg_w[l], sg_b[l].T)

        pre = _rwkv_prepare(r_c, k_c, v_c, x_w, x_a, rw_w0[l], rw_w2[l], rw_a0[l], rw_a2[l],
                            rw_kk[l].reshape(1, rww), rw_ka[l].reshape(1, rww), pair_ones, tri)
        of_c, ob_c, s_ctx = _rwkv_core(pre, h0_ctx, batch, seq, 0)
        h0_lat = jnp.swapaxes(state_wkv[:, l], -1, -2)
        of_l, ob_l, _ = _rwkv_core(pre, h0_lat, dec_batch, dec_seq, n_ctx)
        o_c = _rwkv_finish(jnp.concatenate([of_c, of_l], axis=1), jnp.concatenate([ob_c, ob_l], axis=1),
                           r_c, k_c, v_c, x_g, rw_g2[l], rw_ln_w[l].reshape(1, rww), rw_ln_b[l].reshape(1, rww),
                           rw_rk[l].reshape(1, rww), pair_ones)
        ss.append(jnp.swapaxes(s_ctx, -1, -2))

        x, h2, aff = _out_projection(o_a, o_b, o_c, x, w_out[l], mod3, mod_row_for(tm_out),
                                     g_post_mix[l].reshape(1, D), g_pre_ffn[l].reshape(1, D), w_router[l], tm_out)
        x = _expert_ffn(x, h2, aff, mod3, g_post_ffn[l].reshape(1, D), w_gate[l], w_up[l], w_down[l], ut,
                        (batch, seq, dec_batch, dec_seq))

    y_prompt = x[:n_ctx].reshape(batch, seq, D)
    y_sample = x[n_ctx:].reshape(dec_batch, dec_seq, D)
    return (y_prompt, y_sample, jnp.stack(ks, axis=1), jnp.stack(vs, axis=1), jnp.stack(ss, axis=1))
```

```python
import functools
import math

import numpy as np
import jax
import jax.numpy as jnp
from jax import lax
from jax.experimental import pallas as pl
from jax.experimental.pallas import tpu as pltpu

F32 = jnp.float32
BF16 = jnp.bfloat16
HIGHEST = lax.Precision.HIGHEST

HEAD_DIM = 64
LANES = 128
GRID_W = 64
WIN_ROWS = 8
WIN_COLS = 16
SG_CHUNK = 128
SG_GROUP_DIM = 128
N_MOD = 6
CAPACITY_FACTOR = 2
NORM_EPS = 1e-6
GN_EPS = 64e-5
KK_EPS = 1e-12
RW_CHUNK = 64
RW_PAIR_GROUP = 8
MASK_NEG = -1e30
MOD_ROWS = 8
VMEM_LIMIT_BYTES = 56 * 1024 * 1024


def _pick(n, prefs):
    for p in prefs:
        if n % p == 0:
            return p
    raise ValueError(f"no tile in {prefs} divides {n}")


def _params(*sem):
    return pltpu.CompilerParams(dimension_semantics=sem, vmem_limit_bytes=VMEM_LIMIT_BYTES)


def _rms(x):
    return x * lax.rsqrt(jnp.mean(x * x, axis=-1, keepdims=True) + NORM_EPS)


def _dot(a, b):
    return jnp.dot(a, b, preferred_element_type=F32)


def _dot_nt(a, b):
    return lax.dot_general(a, b, (((1,), (1,)), ((), ())), preferred_element_type=F32)


def _dot_tn(a, b):
    return lax.dot_general(a, b, (((0,), (0,)), ((), ())), preferred_element_type=F32)


def _dot_hi(a, b):
    return jnp.dot(a, b, preferred_element_type=F32, precision=HIGHEST)


def _iota(shape, axis):
    return lax.broadcasted_iota(jnp.int32, shape, axis)


def _mod_kernel(c_ref, w_ref, b_ref, o_ref):
    c = c_ref[...]
    s = (c * jax.nn.sigmoid(c)).astype(BF16)
    o_ref[...] = _dot(s, w_ref[...].astype(BF16)) + b_ref[...]


def _modulation(cond, w_ada, b_ada):
    L, D, N = w_ada.shape
    tn = _pick(N, (1024, 512, 256, 128))
    return pl.pallas_call(
        _mod_kernel,
        grid=(L, N // tn),
        in_specs=[
            pl.BlockSpec((MOD_ROWS, D), lambda l, j: (0, 0)),
            pl.BlockSpec((None, D, tn), lambda l, j: (l, 0, j)),
            pl.BlockSpec((None, 1, tn), lambda l, j: (l, 0, j)),
        ],
        out_specs=pl.BlockSpec((None, MOD_ROWS, tn), lambda l, j: (l, 0, j)),
        out_shape=jax.ShapeDtypeStruct((L, MOD_ROWS, N), F32),
        compiler_params=_params("arbitrary", "arbitrary"),
        name="modulation",
    )(cond, w_ada, b_ada)


def _inproj_kernel(x_ref, sh_ref, sc_ref, g_ref, w_ref, z_ref, h_scr):
    @pl.when(pl.program_id(1) == 0)
    def _():
        h = _rms(x_ref[...]) * g_ref[...] * (1.0 + sc_ref[...]) + sh_ref[...]
        h_scr[...] = h.astype(BF16)

    z_ref[...] = _dot(h_scr[...], w_ref[...].astype(BF16))


def _in_projection(x, mod3, mod_row, g, w, tm):
    T, D = x.shape
    N = w.shape[1]
    tn = _pick(N, (512, 256, 128))
    return pl.pallas_call(
        _inproj_kernel,
        grid=(T // tm, N // tn),
        in_specs=[
            pl.BlockSpec((tm, D), lambda i, j: (i, 0)),
            pl.BlockSpec((None, 1, D), lambda i, j: (mod_row(i), 0, 0)),
            pl.BlockSpec((None, 1, D), lambda i, j: (mod_row(i), 0, 1)),
            pl.BlockSpec((1, D), lambda i, j: (0, 0)),
            pl.BlockSpec((D, tn), lambda i, j: (0, j)),
        ],
        out_specs=pl.BlockSpec((tm, tn), lambda i, j: (i, j)),
        out_shape=jax.ShapeDtypeStruct((T, N), F32),
        scratch_shapes=[pltpu.VMEM((tm, D), BF16)],
        compiler_params=_params("arbitrary", "arbitrary"),
        name="in_projection",
    )(x, mod3, mod3, g, w)


def _attn_ctx_kernel(q_ref, k_ref, v_ref, o_ref, *, heads):
    scale = HEAD_DIM ** -0.5
    outs = []
    for h in range(heads):
        sl = slice(h * HEAD_DIM, (h + 1) * HEAD_DIM)
        q = (q_ref[:, sl] * scale).astype(BF16)
        s = _dot_nt(q, k_ref[:, sl].astype(BF16))
        p = jnp.exp(s - jnp.max(s, axis=-1, keepdims=True))
        l = jnp.sum(p, axis=-1, keepdims=True)
        outs.append(_dot(p.astype(BF16), v_ref[:, sl].astype(BF16)) / l)
    o_ref[...] = jnp.concatenate(outs, axis=-1)


def _attention_context(q, k, v, seq):
    T, W = q.shape
    spec = pl.BlockSpec((seq, W), lambda b: (b, 0))
    return pl.pallas_call(
        functools.partial(_attn_ctx_kernel, heads=W // HEAD_DIM),
        grid=(T // seq,),
        in_specs=[spec, spec, spec],
        out_specs=spec,
        out_shape=jax.ShapeDtypeStruct((T, W), F32),
        compiler_params=_params("arbitrary"),
        name="attention_context",
    )(q, k, v)


def _attn_lat_kernel(q_ref, k_ref, v_ref, kc_ref, vc_ref, bias_ref, o_ref, *, heads, rows, kr):
    scale = HEAD_DIM ** -0.5
    r = pl.program_id(1)
    r0 = jnp.clip(r - kr // 2, 0, rows - kr)
    start = pl.multiple_of(r0 * GRID_W, GRID_W)
    k_win = k_ref[pl.ds(start, kr * GRID_W), :]
    v_win = v_ref[pl.ds(start, kr * GRID_W), :]
    outs = []
    for h in range(heads):
        sl = slice(h * HEAD_DIM, (h + 1) * HEAD_DIM)
        q = (q_ref[:, sl] * scale).astype(BF16)
        s_loc = _dot_nt(q, k_win[:, sl].astype(BF16)) + bias_ref[h]
        s_ctx = _dot_nt(q, kc_ref[:, sl].astype(BF16))
        m = jnp.maximum(jnp.max(s_loc, axis=-1, keepdims=True), jnp.max(s_ctx, axis=-1, keepdims=True))
        p_loc = jnp.exp(s_loc - m)
        p_ctx = jnp.exp(s_ctx - m)
        l = jnp.sum(p_loc, axis=-1, keepdims=True) + jnp.sum(p_ctx, axis=-1, keepdims=True)
        o = _dot(p_loc.astype(BF16), v_win[:, sl].astype(BF16)) + _dot(p_ctx.astype(BF16), vc_ref[:, sl].astype(BF16))
        outs.append(o / l)
    o_ref[...] = jnp.concatenate(outs, axis=-1)


def _toeplitz_kernel(rpb_ref, pick_ref, valid_ref, o_ref):
    o_ref[...] = jnp.where(valid_ref[...] > 0.5, _dot_hi(rpb_ref[...], pick_ref[...]), MASK_NEG)


def _window_bias(rpb, kr):
    delta = np.arange(kr)[:, None]
    i = np.arange(kr)[None, :]
    row_off = (WIN_ROWS - 1) - delta + i
    q = np.arange(GRID_W)[:, None]
    kc = np.arange(GRID_W)[None, :]
    c0 = np.clip(q - WIN_COLS // 2, 0, GRID_W - WIN_COLS)
    valid = (kc >= c0) & (kc < c0 + WIN_COLS)
    col_off = kc - q + (WIN_COLS - 1)
    pick = (col_off[None] == np.arange(2 * WIN_COLS - 1)[:, None, None]) & valid[None]
    H, n_ro, n_co = rpb.shape
    toep = pl.pallas_call(
        _toeplitz_kernel,
        out_shape=jax.ShapeDtypeStruct((H * n_ro, GRID_W * GRID_W), F32),
        name="window_bias",
    )(rpb.astype(F32).reshape(H * n_ro, n_co), jnp.asarray(pick.reshape(n_co, -1), F32),
      jnp.asarray(valid.reshape(1, -1), F32))
    toep = toep.reshape(H, n_ro, GRID_W, GRID_W)
    per_delta = []
    for dl in range(kr):
        lo = int(row_off[dl, 0])
        win = toep[:, lo:lo + kr]
        per_delta.append(win.transpose(0, 2, 1, 3).reshape(rpb.shape[0], GRID_W, kr * GRID_W))
    return jnp.stack(per_delta, axis=0)


def _attention_latent(q, k, v, k_ctx, v_ctx, bias):
    B, T, W = q.shape
    P = k_ctx.shape[1]
    rows = T // GRID_W
    kr = bias.shape[0]
    heads = W // HEAD_DIM

    def bias_index(b, r):
        return (r - jnp.clip(r - kr // 2, 0, rows - kr), 0, 0, 0)

    return pl.pallas_call(
        functools.partial(_attn_lat_kernel, heads=heads, rows=rows, kr=kr),
        grid=(B, rows),
        in_specs=[
            pl.BlockSpec((None, GRID_W, W), lambda b, r: (b, r, 0)),
            pl.BlockSpec((None, T, W), lambda b, r: (b, 0, 0)),
            pl.BlockSpec((None, T, W), lambda b, r: (b, 0, 0)),
            pl.BlockSpec((None, P, W), lambda b, r: (b, 0, 0)),
            pl.BlockSpec((None, P, W), lambda b, r: (b, 0, 0)),
            pl.BlockSpec((None, heads, GRID_W, kr * GRID_W), bias_index),
        ],
        out_specs=pl.BlockSpec((None, GRID_W, W), lambda b, r: (b, r, 0)),
        out_shape=jax.ShapeDtypeStruct((B, T, W), F32),
        compiler_params=_params("arbitrary", "arbitrary"),
        name="attention_latent",
    )(q, k, v, k_ctx, v_ctx, bias)


def _sgu_kernel(u_ref, v_ref, gain_ref, w_ref, b_ref, o_ref, *, groups, chunks):
    for c in range(chunks):
        rows = slice(c * SG_CHUNK, (c + 1) * SG_CHUNK)
        for g in range(groups):
            sl = slice(g * SG_GROUP_DIM, (g + 1) * SG_GROUP_DIM)
            vn = _rms(v_ref[rows, sl]) * gain_ref[:, sl]
            mixed = _dot(w_ref[g].astype(BF16), vn.astype(BF16)) + b_ref[:, g:g + 1]
            o_ref[rows, sl] = u_ref[rows, sl] * mixed


def _spatial_gating(u, v, gain, w_s, b_t):
    T, W = u.shape
    G = w_s.shape[0]
    tm = _pick(T, (512, 256, 128))
    spec = pl.BlockSpec((tm, W), lambda i: (i, 0))
    return pl.pallas_call(
        functools.partial(_sgu_kernel, groups=G, chunks=tm // SG_CHUNK),
        grid=(T // tm,),
        in_specs=[spec, spec,
                  pl.BlockSpec((1, W), lambda i: (0, 0)),
                  pl.BlockSpec((G, SG_CHUNK, SG_CHUNK), lambda i: (0, 0, 0)),
                  pl.BlockSpec((SG_CHUNK, G), lambda i: (0, 0))],
        out_specs=spec,
        out_shape=jax.ShapeDtypeStruct((T, W), F32),
        compiler_params=_params("arbitrary"),
        name="spatial_gating",
    )(u, v, gain, w_s, b_t)


def _softplus(y):
    return jnp.maximum(y, 0.0) + jnp.log(1.0 + jnp.exp(-jnp.abs(y)))


def _rwkv_pre_kernel(r_ref, k_ref, v_ref, xw_ref, xa_ref, w0_ref, w2_ref, a0_ref, a2_ref, kkw_ref, ka_ref,
                     bd_ref, tri_ref,
                     at_f, rt_f, bh_f, kh_f, wc_f, at_b, rt_b, bh_b, kh_b, wc_b, v_o, *, pairs, chunks):
    C = RW_CHUNK
    r = r_ref[...]
    k = k_ref[...]
    bd = bd_ref[...]
    kk = k * kkw_ref[...]
    kk2 = kk * kk
    ssq = jnp.concatenate([_dot_hi(kk2[:, p * LANES:(p + 1) * LANES], bd) for p in range(pairs)], axis=-1)
    kk = kk * lax.rsqrt(ssq + KK_EPS)
    tw = jnp.tanh(xw_ref[...]).astype(BF16)
    xa = xa_ref[...].astype(BF16)
    v_bf = v_ref[...].astype(BF16)
    for p in range(pairs):
        v_o[p] = v_bf[:, p * LANES:(p + 1) * LANES]
    outs = ((at_f, rt_f, bh_f, kh_f, wc_f), (at_b, rt_b, bh_b, kh_b, wc_b))
    for d in range(2):
        at_o, rt_o, bh_o, kh_o, wc_o = outs[d]
        w_log = -_softplus(-(w0_ref[d:d + 1, :] + _dot(tw, w2_ref[d].astype(BF16)))) - 0.5
        logw = -jnp.exp(w_log)
        a_rate = jax.nn.sigmoid(a0_ref[d:d + 1, :] + _dot(xa, a2_ref[d].astype(BF16)))
        k_d = k * (1.0 + (a_rate - 1.0) * ka_ref[...])
        b = kk * a_rate
        tri = tri_ref[d]
        tot_row = C - 1 if d == 0 else 0
        for c in range(chunks):
            rows = slice(c * C, (c + 1) * C)
            lw = logw[rows]
            cum = _dot_hi(tri, lw)
            tot = cum[tot_row:tot_row + 1, :]
            e_dn = jnp.exp(tot - cum)
            at = -kk[rows] * jnp.exp(cum - lw - tot)
            rt = r[rows] * jnp.exp(cum - tot)
            bh = b[rows] * e_dn
            kh = k_d[rows] * e_dn
            wc = jnp.exp(tot)
            for p in range(pairs):
                sl = slice(p * LANES, (p + 1) * LANES)
                at_o[p, rows, :] = at[:, sl].astype(BF16)
                rt_o[p, rows, :] = rt[:, sl].astype(BF16)
                bh_o[p, rows, :] = bh[:, sl].astype(BF16)
                kh_o[p, rows, :] = kh[:, sl].astype(BF16)
                wc_o[c, p] = wc[:, sl]


def _rwkv_prepare(r, k, v, xw, xa, w0, w2, a0, a2, kkw, ka, bd, tri):
    T, RW = r.shape
    pairs = RW // LANES
    C = RW_CHUNK
    tm = _pick(T, (256, 128, 64))
    tok = pl.BlockSpec((tm, RW), lambda i: (i, 0))
    lw_w = xw.shape[1]
    la_w = xa.shape[1]
    packed = pl.BlockSpec((pairs, tm, LANES), lambda i: (0, i, 0))
    wc_spec = pl.BlockSpec((tm // C, pairs, 1, LANES), lambda i: (i, 0, 0, 0))
    packed_shape = jax.ShapeDtypeStruct((pairs, T, LANES), BF16)
    wc_shape = jax.ShapeDtypeStruct((T // C, pairs, 1, LANES), F32)
    full = lambda a: pl.BlockSpec(a.shape, lambda i: (0,) * a.ndim)
    return pl.pallas_call(
        functools.partial(_rwkv_pre_kernel, pairs=pairs, chunks=tm // C),
        grid=(T // tm,),
        in_specs=[tok, tok, tok,
                  pl.BlockSpec((tm, lw_w), lambda i: (i, 0)),
                  pl.BlockSpec((tm, la_w), lambda i: (i, 0)),
                  full(w0), full(w2), full(a0), full(a2), full(kkw), full(ka), full(bd), full(tri)],
        out_specs=[packed, packed, packed, packed, wc_spec, packed, packed, packed, packed, wc_spec, packed],
        out_shape=[packed_shape] * 4 + [wc_shape] + [packed_shape] * 4 + [wc_shape] + [packed_shape],
        compiler_params=_params("arbitrary"),
        name="rwkv_prepare",
    )(r, k, v, xw, xa, w0, w2, a0, a2, kkw, ka, bd, tri)


def _rwkv_pair_chunk(at, rt, bh, kh, v, hs, masks):
    C = RW_CHUNK
    strict, incl, eye, same_head, lane_lo = masks
    zero = jnp.zeros_like(at)
    stack = lambda x: jnp.concatenate([jnp.where(lane_lo, x, zero), jnp.where(lane_lo, zero, x)], axis=0)
    twice = lambda x: jnp.concatenate([x, x], axis=0)
    atm, rtm, vm = stack(at), stack(rt), stack(v)
    bk = jnp.concatenate([twice(bh), twice(kh)], axis=0)
    a = _dot_nt(jnp.concatenate([atm, rtm], axis=0), bk)
    yield
    l = jnp.where(strict, a[:2 * C, :2 * C], 0.0)
    a_ak = jnp.where(strict, a[:2 * C, 2 * C:], 0.0).astype(BF16)
    a_rb = jnp.where(incl, a[2 * C:, :2 * C], 0.0).astype(BF16)
    a_rk = jnp.where(incl, a[2 * C:, 2 * C:], 0.0).astype(BF16)
    hs_b = hs.astype(BF16)
    x2 = _dot(jnp.concatenate([atm, a_ak], axis=1), jnp.concatenate([hs_b, vm], axis=0)).astype(BF16)
    t = eye + l
    pw = l.astype(BF16)
    pw = _dot(pw, pw).astype(BF16)
    yield
    for _ in range(int(math.log2(C)) - 2):
        both = _dot(jnp.concatenate([t.astype(BF16), pw], axis=0), pw)
        yield
        t = t + both[:2 * C]
        pw = both[2 * C:].astype(BF16)
    t = t + _dot(t.astype(BF16), pw)
    yield
    u = _dot(t.astype(BF16), x2).astype(BF16)
    yield
    o = _dot(jnp.concatenate([rtm, a_rb, a_rk], axis=1), jnp.concatenate([hs_b, u, vm], axis=0))
    h_new = hs + jnp.where(same_head, _dot_tn(bk, jnp.concatenate([u, vm], axis=0)), 0.0)
    return o[:C] + o[C:], h_new


def _lock_step(gens):
    results = [None] * len(gens)
    live = list(enumerate(gens))
    while live:
        still = []
        for i, g in live:
            try:
                next(g)
                still.append((i, g))
            except StopIteration as done:
                results[i] = done.value
        live = still
    return results


def _rwkv_core_kernel(at_f, rt_f, bh_f, kh_f, v_f, wc_f, at_b, rt_b, bh_b, kh_b, v_b, wc_b, h0_ref,
                      of_ref, ob_ref, hs_ref, *, pairs, group):
    C = RW_CHUNK

    @pl.when(pl.program_id(1) == 0)
    def _():
        hs_ref[...] = h0_ref[...]

    row = _iota((2 * C, 2 * C), 0)
    col = _iota((2 * C, 2 * C), 1)
    same = (row // C) == (col // C)
    rt_, ct_ = row % C, col % C
    eye = jnp.where(row == col, 1.0, 0.0).astype(F32)
    lrow = _iota((LANES, LANES), 0)
    lcol = _iota((LANES, LANES), 1)
    same_head = (lrow // HEAD_DIM) == (lcol // HEAD_DIM)
    lane_lo = _iota((C, LANES), 1) < HEAD_DIM
    masks = ((same & (ct_ < rt_), same & (ct_ <= rt_), eye, same_head, lane_lo),
             (same & (ct_ > rt_), same & (ct_ >= rt_), eye, same_head, lane_lo))
    dirs = ((at_f, rt_f, bh_f, kh_f, v_f, wc_f, of_ref), (at_b, rt_b, bh_b, kh_b, v_b, wc_b, ob_ref))

    for g0 in range(0, pairs, group):
        jobs = [(d, p) for p in range(g0, g0 + group) for d in range(2)]
        gens = []
        for d, p in jobs:
            at_r, rt_r, bh_r, kh_r, v_r, wc_r, _ = dirs[d]
            wcol = jnp.sum(jnp.where(lrow == lcol, jnp.broadcast_to(wc_r[p], (LANES, LANES)), 0.0),
                           axis=1, keepdims=True)
            gens.append(_rwkv_pair_chunk(at_r[p], rt_r[p], bh_r[p], kh_r[p], v_r[p], hs_ref[d, p] * wcol, masks[d]))
        for (d, p), (o, h_new) in zip(jobs, _lock_step(gens)):
            hs_ref[d, p] = h_new
            dirs[d][-1][p] = o


def _pair_states(h):
    n, _, H, _, _ = h.shape
    hp = h.reshape(n, 2, H // 2, 2, HEAD_DIM, HEAD_DIM)
    z = jnp.zeros_like(hp[:, :, :, 0])
    top = jnp.concatenate([hp[:, :, :, 0], z], axis=-1)
    bot = jnp.concatenate([z, hp[:, :, :, 1]], axis=-1)
    return jnp.concatenate([top, bot], axis=-2)


def _unpair_states(hp):
    n, _, P, _, _ = hp.shape
    h0 = hp[:, :, :, :HEAD_DIM, :HEAD_DIM]
    h1 = hp[:, :, :, HEAD_DIM:, HEAD_DIM:]
    return jnp.stack([h0, h1], axis=3).reshape(n, 2, 2 * P, HEAD_DIM, HEAD_DIM)


def _rwkv_core(pre, h0, n_seq, seq_len, tok_off):
    at_f, rt_f, bh_f, kh_f, wc_f, at_b, rt_b, bh_b, kh_b, wc_b, v = pre
    pairs, T, _ = at_f.shape
    C = RW_CHUNK
    nch = seq_len // C
    off = tok_off // C
    fwd = lambda s, c: off + s * nch + c
    bwd = lambda s, c: off + s * nch + (nch - 1 - c)
    pk = lambda f: pl.BlockSpec((pairs, C, LANES), lambda s, c: (0, f(s, c), 0))
    wc = lambda f: pl.BlockSpec((None, pairs, 1, LANES), lambda s, c: (f(s, c), 0, 0, 0))
    st = pl.BlockSpec((None, 2, pairs, LANES, LANES), lambda s, c: (s, 0, 0, 0, 0))
    o_shape = jax.ShapeDtypeStruct((pairs, n_seq * seq_len, LANES), F32)
    o_f = pl.BlockSpec((pairs, C, LANES), lambda s, c: (0, s * nch + c, 0))
    o_b = pl.BlockSpec((pairs, C, LANES), lambda s, c: (0, s * nch + (nch - 1 - c), 0))
    return pl.pallas_call(
        functools.partial(_rwkv_core_kernel, pairs=pairs, group=_pick(pairs, (RW_PAIR_GROUP, 4, 2, 1))),
        grid=(n_seq, nch),
        in_specs=[pk(fwd), pk(fwd), pk(fwd), pk(fwd), pk(fwd), wc(fwd),
                  pk(bwd), pk(bwd), pk(bwd), pk(bwd), pk(bwd), wc(bwd), st],
        out_specs=[o_f, o_b, st],
        out_shape=[o_shape, o_shape, jax.ShapeDtypeStruct(h0.shape, F32)],
        compiler_params=_params("arbitrary", "arbitrary"),
        name="rwkv_scan",
    )(at_f, rt_f, bh_f, kh_f, v, wc_f, at_b, rt_b, bh_b, kh_b, v, wc_b, h0)


def _rwkv_post_kernel(of_ref, ob_ref, r_ref, k_ref, v_ref, xg_ref, g2_ref, lnw_ref, lnb_ref, rk_ref, bd_ref,
                      o_ref, *, pairs):
    bd = bd_ref[...]
    sg = jax.nn.sigmoid(xg_ref[...]).astype(BF16)
    inv = 1.0 / HEAD_DIM
    for p in range(pairs):
        sl = slice(p * LANES, (p + 1) * LANES)
        o = of_ref[p] + ob_ref[p]
        mu = _dot_hi(o, bd) * inv
        dlt = o - mu
        var = _dot_hi(dlt * dlt, bd) * inv
        on = dlt * lax.rsqrt(var + GN_EPS) * lnw_ref[:, sl] + lnb_ref[:, sl]
        bonus = _dot_hi(r_ref[:, sl] * k_ref[:, sl] * rk_ref[:, sl], bd) * v_ref[:, sl]
        gate = _dot(sg, g2_ref[:, sl].astype(BF16))
        o_ref[:, sl] = (on + bonus) * gate


def _rwkv_finish(o_f, o_b, r, k, v, xg, g2, lnw, lnb, rk, bd):
    T, RW = r.shape
    pairs = RW // LANES
    tm = _pick(T, (256, 128, 64))
    tok = pl.BlockSpec((tm, RW), lambda i: (i, 0))
    packed = pl.BlockSpec((pairs, tm, LANES), lambda i: (0, i, 0))
    full = lambda a: pl.BlockSpec(a.shape, lambda i: (0,) * a.ndim)
    return pl.pallas_call(
        functools.partial(_rwkv_post_kernel, pairs=pairs),
        grid=(T // tm,),
        in_specs=[packed, packed, tok, tok, tok, pl.BlockSpec((tm, xg.shape[1]), lambda i: (i, 0)),
                  full(g2), full(lnw), full(lnb), full(rk), full(bd)],
        out_specs=tok,
        out_shape=jax.ShapeDtypeStruct((T, RW), F32),
        compiler_params=_params("arbitrary"),
        name="rwkv_finish",
    )(o_f, o_b, r, k, v, xg, g2, lnw, lnb, rk, bd)


def _outproj_kernel(oa_ref, ob_ref, oc_ref, x_ref, w_ref, g1_ref, sh2_ref, sc2_ref, gpost_ref, gpre_ref, wr_ref,
                    xo_ref, h2_ref, aff_ref, cat_scr, acc_scr, *, tn):
    j = pl.program_id(1)

    @pl.when(j == 0)
    def _():
        cat_scr[...] = jnp.concatenate([oa_ref[...], ob_ref[...], oc_ref[...]], axis=-1).astype(BF16)

    acc_scr[j] = _dot(cat_scr[...], w_ref[...].astype(BF16))

    @pl.when(j == pl.num_programs(1) - 1)
    def _():
        mixed = jnp.concatenate([acc_scr[n] for n in range(acc_scr.shape[0])], axis=-1)
        x = x_ref[...] + g1_ref[...] * (_rms(mixed) * gpost_ref[...])
        xo_ref[...] = x
        h2 = _rms(x) * gpre_ref[...] * (1.0 + sc2_ref[...]) + sh2_ref[...]
        h2_ref[...] = h2.astype(BF16)
        logits = _dot_hi(h2, wr_ref[...])
        e = jnp.exp(logits - jnp.max(logits, axis=-1, keepdims=True))
        aff_ref[...] = e / jnp.sum(e, axis=-1, keepdims=True)


def _out_projection(o_a, o_b, o_c, x, w_out, mod3, mod_row, g_post, g_pre, w_router, tm):
    T, D = x.shape
    E = w_router.shape[1]
    tn = _pick(D, (512, 256, 128))
    row = lambda a: pl.BlockSpec((tm, a.shape[1]), lambda i, j: (i, 0))
    mod = lambda col: pl.BlockSpec((None, 1, D), lambda i, j: (mod_row(i), 0, col))
    vec = pl.BlockSpec((1, D), lambda i, j: (0, 0))
    return pl.pallas_call(
        functools.partial(_outproj_kernel, tn=tn),
        grid=(T // tm, D // tn),
        in_specs=[row(o_a), row(o_b), row(o_c), row(x),
                  pl.BlockSpec((D, tn), lambda i, j: (0, j)),
                  mod(2), mod(3), mod(4), vec, vec,
                  pl.BlockSpec((D, E), lambda i, j: (0, 0))],
        out_specs=[pl.BlockSpec((tm, D), lambda i, j: (i, 0)),
                   pl.BlockSpec((tm, D), lambda i, j: (i, 0)),
                   pl.BlockSpec((tm, E), lambda i, j: (i, 0))],
        out_shape=[jax.ShapeDtypeStruct((T, D), F32), jax.ShapeDtypeStruct((T, D), BF16),
                   jax.ShapeDtypeStruct((T, E), F32)],
        scratch_shapes=[pltpu.VMEM((tm, D), BF16), pltpu.VMEM((D // tn, tm, tn), F32)],
        compiler_params=_params("arbitrary", "arbitrary"),
        name="out_projection",
    )(o_a, o_b, o_c, x, w_out, mod3, mod3, mod3, g_post, g_pre, w_router)


def _select_kernel(a_ref, ut_ref, pos_ref, *, cap):
    a = a_ref[...]
    R, T = a.shape
    lo = jnp.zeros((R, 1), jnp.int32)
    for bit in range(30, -1, -1):
        cand = lo | (1 << bit)
        cnt = jnp.sum(jnp.where(a >= lax.bitcast_convert_type(cand, F32), 1.0, 0.0), axis=1, keepdims=True)
        lo = jnp.where(cnt >= cap, cand, lo)
    thr = lax.bitcast_convert_type(lo, F32)
    gt = a > thr
    eq = a == thr
    need = cap - jnp.sum(jnp.where(gt, 1.0, 0.0), axis=1, keepdims=True)
    ut = ut_ref[...]
    nblk = T // LANES
    tie_carry = jnp.zeros((R, 1), F32)
    pos_carry = jnp.zeros((R, 1), F32)
    for n in range(nblk):
        sl = slice(n * LANES, (n + 1) * LANES)
        eq_b = jnp.where(eq[:, sl], 1.0, 0.0)
        tie_rank = _dot(eq_b.astype(BF16), ut) + tie_carry
        tie_carry = tie_carry + jnp.sum(eq_b, axis=1, keepdims=True)
        sel = jnp.where(gt[:, sl], 1.0, jnp.where(tie_rank < need, eq_b, 0.0))
        pos = _dot(sel.astype(BF16), ut) + pos_carry
        pos_carry = pos_carry + jnp.sum(sel, axis=1, keepdims=True)
        pos_ref[:, sl] = jnp.where(sel > 0.5, pos, -1.0).astype(jnp.int32)


def _select(aff_rows, cap, ut):
    R, T = aff_rows.shape
    tr = _pick(R, (128, 64, 32, 16, 8))
    return pl.pallas_call(
        functools.partial(_select_kernel, cap=cap),
        grid=(R // tr,),
        in_specs=[pl.BlockSpec((tr, T), lambda i: (i, 0)), pl.BlockSpec((LANES, LANES), lambda i: (0, 0))],
        out_specs=pl.BlockSpec((tr, T), lambda i: (i, 0)),
        out_shape=jax.ShapeDtypeStruct((R, T), jnp.int32),
        compiler_params=_params("arbitrary"),
        name="expert_select",
    )(aff_rows, ut)


def _one_hot_rows(pos_row, cap):
    T = pos_row.shape[1]
    return jnp.where(_iota((cap, T), 0) == pos_row, 1.0, 0.0).astype(BF16)


def _gather_ctx_kernel(h_ref, pos_ref, xs_ref, *, experts, cap):
    h = h_ref[...]
    for e in range(experts):
        xs_ref[e] = _dot(_one_hot_rows(pos_ref[e:e + 1, :], cap), h).astype(BF16)


def _gather_lat_kernel(h_ref, pos_ref, xs_ref, *, cap):
    xs_ref[...] = _dot(_one_hot_rows(pos_ref[...], cap), h_ref[...]).astype(BF16)


def _expert_up_kernel(xs_ref, wg_ref, wu_ref, hid_ref):
    x = xs_ref[...]
    g = _dot(x, wg_ref[...].astype(BF16))
    u = _dot(x, wu_ref[...].astype(BF16))
    hid_ref[...] = (g * jax.nn.sigmoid(g) * u).astype(BF16)


def _expert_down_kernel(hid_ref, wd_ref, y_ref):
    y_ref[...] = _dot(hid_ref[...], wd_ref[...].astype(BF16)).astype(BF16)


def _scatter_term(pos_col, gate_col, y, cap):
    tm = pos_col.shape[0]
    pt = jnp.where(_iota((tm, cap), 1).astype(F32) == pos_col, 1.0, 0.0).astype(BF16)
    return gate_col * _dot(pt, y)


def _ffn_residual(x, ffn, g2, g_post):
    return x + g2 * (_rms(ffn) * g_post)


def _combine_ctx_kernel(x_ref, y_ref, pos_ref, aff_ref, g2_ref, gpost_ref, o_ref, *, experts, cap):
    pos = pos_ref[...].astype(F32)
    aff = aff_ref[...]
    acc = jnp.zeros(x_ref.shape, F32)
    for e in range(experts):
        acc = acc + _scatter_term(pos[:, e:e + 1], aff[:, e:e + 1], y_ref[e], cap)
    o_ref[...] = _ffn_residual(x_ref[...], acc, g2_ref[...], gpost_ref[...])


def _combine_lat_kernel(x_ref, y_ref, pos_ref, aff_ref, g2_ref, gpost_ref, o_ref, acc_scr, *, cap):
    e = pl.program_id(2)

    @pl.when(e == 0)
    def _():
        acc_scr[...] = jnp.zeros(acc_scr.shape, F32)

    mine = _iota(pos_ref.shape, 1) == e
    pos_col = jnp.sum(jnp.where(mine, pos_ref[...].astype(F32), 0.0), axis=1, keepdims=True)
    gate_col = jnp.sum(jnp.where(mine, aff_ref[...], 0.0), axis=1, keepdims=True)
    acc_scr[...] += _scatter_term(pos_col, gate_col, y_ref[...], cap)

    @pl.when(e == pl.num_programs(2) - 1)
    def _():
        o_ref[...] = _ffn_residual(x_ref[...], acc_scr[...], g2_ref[...], gpost_ref[...])


def _expert_ffn(x, h2, aff, mod3, g_post, w_gate, w_up, w_down, ut, dims):
    batch, seq, dec_batch, dec_seq = dims
    T, D = x.shape
    E, _, F = w_gate.shape
    n_ctx = batch * seq
    cap_c = CAPACITY_FACTOR * seq // E
    cap_l = CAPACITY_FACTOR * dec_seq // E
    rows_c = batch * cap_c
    rows = rows_c + dec_batch * cap_l
    assert n_ctx % dec_seq == 0 and rows_c % cap_l == 0
    lat_blk = n_ctx // dec_seq
    lat_row_blk = rows_c // cap_l

    aff_c = aff[:n_ctx].reshape(batch, seq, E).transpose(0, 2, 1)
    aff_l = aff[n_ctx:].reshape(dec_batch, dec_seq, E).transpose(0, 2, 1)
    pos_c = _select(aff_c.reshape(batch * E, seq), cap_c, ut).reshape(batch, E, seq)
    pos_l = _select(aff_l.reshape(dec_batch * E, dec_seq), cap_l, ut).reshape(dec_batch, E, 1, dec_seq)
    pos_t = jnp.concatenate([pos_c.transpose(0, 2, 1).reshape(n_ctx, E),
                             pos_l.reshape(dec_batch, E, dec_seq).transpose(0, 2, 1).reshape(T - n_ctx, E)], axis=0)

    xs_shape = jax.ShapeDtypeStruct((E, rows, D), BF16)
    xs_c = pl.pallas_call(
        functools.partial(_gather_ctx_kernel, experts=E, cap=cap_c),
        grid=(batch,),
        in_specs=[pl.BlockSpec((seq, D), lambda b: (b, 0)), pl.BlockSpec((None, E, seq), lambda b: (b, 0, 0))],
        out_specs=pl.BlockSpec((E, cap_c, D), lambda b: (0, b, 0)),
        out_shape=jax.ShapeDtypeStruct((E, rows_c, D), BF16),
        compiler_params=_params("arbitrary"),
        name="expert_gather_context",
    )(h2, pos_c)
    xs_l = pl.pallas_call(
        functools.partial(_gather_lat_kernel, cap=cap_l),
        grid=(dec_batch, E),
        in_specs=[pl.BlockSpec((dec_seq, D), lambda b, e: (lat_blk + b, 0)),
                  pl.BlockSpec((None, None, 1, dec_seq), lambda b, e: (b, e, 0, 0))],
        out_specs=pl.BlockSpec((None, cap_l, D), lambda b, e: (e, b, 0)),
        out_shape=jax.ShapeDtypeStruct((E, dec_batch * cap_l, D), BF16),
        compiler_params=_params("arbitrary", "arbitrary"),
        name="expert_gather_latent",
    )(h2, pos_l)
    xs = jnp.concatenate([xs_c, xs_l], axis=1)
    del xs_shape

    tf = _pick(F, (512, 256, 128))
    hid = pl.pallas_call(
        _expert_up_kernel,
        grid=(E, F // tf),
        in_specs=[pl.BlockSpec((None, rows, D), lambda e, n: (e, 0, 0)),
                  pl.BlockSpec((None, D, tf), lambda e, n: (e, 0, n)),
                  pl.BlockSpec((None, D, tf), lambda e, n: (e, 0, n))],
        out_specs=pl.BlockSpec((None, rows, tf), lambda e, n: (e, 0, n)),
        out_shape=jax.ShapeDtypeStruct((E, rows, F), BF16),
        compiler_params=_params("arbitrary", "arbitrary"),
        name="expert_up",
    )(xs, w_gate, w_up)
    td = _pick(D, (512, 256, 128))
    y = pl.pallas_call(
        _expert_down_kernel,
        grid=(E, D // td),
        in_specs=[pl.BlockSpec((None, rows, F), lambda e, n: (e, 0, 0)),
                  pl.BlockSpec((None, F, td), lambda e, n: (e, 0, n))],
        out_specs=pl.BlockSpec((None, rows, td), lambda e, n: (e, 0, n)),
        out_shape=jax.ShapeDtypeStruct((E, rows, D), BF16),
        compiler_params=_params("arbitrary", "arbitrary"),
        name="expert_down",
    )(hid, w_down)

    vec = lambda nd: pl.BlockSpec((1, D), lambda *i: (0, 0))
    out_c = pl.pallas_call(
        functools.partial(_combine_ctx_kernel, experts=E, cap=cap_c),
        grid=(batch,),
        in_specs=[pl.BlockSpec((seq, D), lambda b: (b, 0)),
                  pl.BlockSpec((E, cap_c, D), lambda b: (0, b, 0)),
                  pl.BlockSpec((seq, E), lambda b: (b, 0)),
                  pl.BlockSpec((seq, E), lambda b: (b, 0)),
                  pl.BlockSpec((None, 1, D), lambda b: (0, 0, 5)),
                  vec(1)],
        out_specs=pl.BlockSpec((seq, D), lambda b: (b, 0)),
        out_shape=jax.ShapeDtypeStruct((n_ctx, D), F32),
        compiler_params=_params("arbitrary"),
        name="expert_combine_context",
    )(x, y, pos_t, aff, mod3, g_post)
    tm = _pick(dec_seq, (512, 256, 128))
    nt = dec_seq // tm
    tok = lambda b, i, e: (lat_blk * nt + b * nt + i, 0)
    out_l = pl.pallas_call(
        functools.partial(_combine_lat_kernel, cap=cap_l),
        grid=(dec_batch, nt, E),
        in_specs=[pl.BlockSpec((tm, D), tok),
                  pl.BlockSpec((None, cap_l, D), lambda b, i, e: (e, lat_row_blk + b, 0)),
                  pl.BlockSpec((tm, E), tok),
                  pl.BlockSpec((tm, E), tok),
                  pl.BlockSpec((None, 1, D), lambda b, i, e: (1 + b, 0, 5)),
                  vec(3)],
        out_specs=pl.BlockSpec((tm, D), lambda b, i, e: (b * nt + i, 0)),
        out_shape=jax.ShapeDtypeStruct((T - n_ctx, D), F32),
        scratch_shapes=[pltpu.VMEM((tm, D), F32)],
        compiler_params=_params("arbitrary", "arbitrary", "arbitrary"),
        name="expert_combine_latent",
    )(x, y, pos_t, aff, mod3, g_post)
    return jnp.concatenate([out_c, out_l], axis=0)


def kernel(x_prompt, x_sample, cache_k, cache_v, state_wkv, c, c_ctx, w_ada, b_ada, g_pre_mix, g_post_mix, g_pre_ffn, g_post_ffn, w_in, na_rpb, sg_gain, sg_w, sg_b, rw_w0, rw_w2, rw_a0, rw_a2, rw_g2, rw_kk, rw_ka, rw_rk, rw_ln_w, rw_ln_b, w_out, w_router, w_gate, w_up, w_down):
    batch, seq, D = x_prompt.shape
    dec_batch, dec_seq, _ = x_sample.shape
    depth = w_ada.shape[0]
    past = cache_k.shape[2]
    na_heads = cache_k.shape[3]
    naw = na_heads * HEAD_DIM
    sgw = sg_gain.shape[1]
    rw_heads = state_wkv.shape[3]
    rww = rw_heads * HEAD_DIM
    lora_w, lora_a, lora_g = rw_w2.shape[2], rw_a2.shape[2], rw_g2.shape[1]
    n_ctx = batch * seq
    n_lat = dec_batch * dec_seq
    assert 1 + dec_batch <= MOD_ROWS and rww % LANES == 0
    assert seq % RW_CHUNK == 0 and dec_seq % RW_CHUNK == 0 and dec_seq % GRID_W == 0

    tm = _pick(math.gcd(n_ctx, dec_seq), (1024, 512, 256, 128))
    tm_out = min(tm, 512)

    def mod_row_for(tile):
        def mod_row(i):
            t0 = i * tile
            return jnp.where(t0 < n_ctx, 0, 1 + (t0 - n_ctx) // dec_seq)
        return mod_row

    pair_ones = jnp.asarray(np.kron(np.eye(LANES // HEAD_DIM), np.ones((HEAD_DIM, HEAD_DIM))), F32)
    tril = np.tril(np.ones((RW_CHUNK, RW_CHUNK)))
    tri = jnp.asarray(np.stack([tril, tril.T]), F32)
    ut = jnp.asarray(np.triu(np.ones((LANES, LANES)), 1), BF16)
    kr = min(WIN_ROWS, dec_seq // GRID_W)

    cond = jnp.zeros((MOD_ROWS, D), F32).at[0].set(c_ctx).at[1:1 + dec_batch].set(c)
    mods = _modulation(cond, w_ada, b_ada.reshape(depth, 1, N_MOD * D))

    x = jnp.concatenate([x_prompt.reshape(n_ctx, D), x_sample.reshape(n_lat, D)], axis=0)
    h0_ctx = jnp.zeros((batch, 2, rww // LANES, LANES, LANES), F32)
    ks, vs, ss = [], [], []
    for l in range(depth):
        mod3 = mods[l].reshape(MOD_ROWS, 1, N_MOD * D)
        z = _in_projection(x, mod3, mod_row_for(tm), g_pre_mix[l].reshape(1, D), w_in[l], tm)
        cuts = np.cumsum([0, naw, naw, naw, sgw, sgw, rww, rww, rww, lora_w, lora_a, lora_g])
        qa, ka, va, u_s, v_s, r_c, k_c, v_c, x_w, x_a, x_g = [z[:, a:b] for a, b in zip(cuts[:-1], cuts[1:])]

        o_a_ctx = _attention_context(qa[:n_ctx], ka[:n_ctx], va[:n_ctx], seq)
        lat3 = lambda t: t[n_ctx:].reshape(dec_batch, dec_seq, naw)
        o_a_lat = _attention_latent(lat3(qa), lat3(ka), lat3(va),
                                    cache_k[:, l].reshape(dec_batch, past, naw),
                                    cache_v[:, l].reshape(dec_batch, past, naw),
                                    _window_bias(na_rpb[l], kr))
        o_a = jnp.concatenate([o_a_ctx, o_a_lat.reshape(n_lat, naw)], axis=0)
        ks.append(ka[:n_ctx].reshape(batch, seq, na_heads, HEAD_DIM))
        vs.append(va[:n_ctx].reshape(batch, seq, na_heads, HEAD_DIM))

        o_b = _spatial_gating(u_s, v_s, sg_gain[l].reshape(1, sgw), sg_w[l], sg_b[l].T)

        pre = _rwkv_prepare(r_c, k_c, v_c, x_w, x_a, rw_w0[l], rw_w2[l], rw_a0[l], rw_a2[l],
                            rw_kk[l].reshape(1, rww), rw_ka[l].reshape(1, rww), pair_ones, tri)
        of_c, ob_c, s_ctx = _rwkv_core(pre, h0_ctx, batch, seq, 0)
        h0_lat = _pair_states(jnp.swapaxes(state_wkv[:, l], -1, -2))
        of_l, ob_l, _ = _rwkv_core(pre, h0_lat, dec_batch, dec_seq, n_ctx)
        o_c = _rwkv_finish(jnp.concatenate([of_c, of_l], axis=1), jnp.concatenate([ob_c, ob_l], axis=1),
                           r_c, k_c, v_c, x_g, rw_g2[l], rw_ln_w[l].reshape(1, rww), rw_ln_b[l].reshape(1, rww),
                           rw_rk[l].reshape(1, rww), pair_ones)
        ss.append(jnp.swapaxes(_unpair_states(s_ctx), -1, -2))

        x, h2, aff = _out_projection(o_a, o_b, o_c, x, w_out[l], mod3, mod_row_for(tm_out),
                                     g_post_mix[l].reshape(1, D), g_pre_ffn[l].reshape(1, D), w_router[l], tm_out)
        x = _expert_ffn(x, h2, aff, mod3, g_post_ffn[l].reshape(1, D), w_gate[l], w_up[l], w_down[l], ut,
                        (batch, seq, dec_batch, dec_seq))

    y_prompt = x[:n_ctx].reshape(batch, seq, D)
    y_sample = x[n_ctx:].reshape(dec_batch, dec_seq, D)
    return (y_prompt, y_sample, jnp.stack(ks, axis=1), jnp.stack(vs, axis=1), jnp.stack(ss, axis=1))
```

```python
import functools
import math

import numpy as np
import jax
import jax.numpy as jnp
from jax import lax
from jax.experimental import pallas as pl
from jax.experimental.pallas import tpu as pltpu

F32 = jnp.float32
BF16 = jnp.bfloat16
HIGHEST = lax.Precision.HIGHEST

HEAD_DIM = 64
LANES = 128
GRID_W = 64
WIN_ROWS = 8
WIN_COLS = 16
SG_CHUNK = 128
SG_GROUP_DIM = 128
N_MOD = 6
CAPACITY_FACTOR = 2
NORM_EPS = 1e-6
GN_EPS = 64e-5
KK_EPS = 1e-12
RW_CHUNK = 64
RW_PAIR_GROUP = 8
MASK_NEG = -1e30
MOD_ROWS = 8
VMEM_LIMIT_BYTES = 56 * 1024 * 1024
MAX_WEIGHT_COLS = 3072


def _pick(n, prefs):
    for p in prefs:
        if n % p == 0:
            return p
    raise ValueError(f"no tile in {prefs} divides {n}")


def _col_tile(n, cap):
    tiles = [t for t in range(LANES, min(n, cap) + 1, LANES) if n % t == 0]
    return tiles[-1]


def _params(*sem):
    return pltpu.CompilerParams(dimension_semantics=sem, vmem_limit_bytes=VMEM_LIMIT_BYTES)


def _rms(x):
    return x * lax.rsqrt(jnp.mean(x * x, axis=-1, keepdims=True) + NORM_EPS)


def _dot(a, b):
    return jnp.dot(a, b, preferred_element_type=F32)


def _dot_nt(a, b):
    return lax.dot_general(a, b, (((1,), (1,)), ((), ())), preferred_element_type=F32)


def _dot_tn(a, b):
    return lax.dot_general(a, b, (((0,), (0,)), ((), ())), preferred_element_type=F32)


def _dot_hi(a, b):
    return jnp.dot(a, b, preferred_element_type=F32, precision=HIGHEST)


def _iota(shape, axis):
    return lax.broadcasted_iota(jnp.int32, shape, axis)


def _full(a):
    return pl.BlockSpec(a.shape, lambda *_: (0,) * a.ndim)


def _cast_kernel(w_ref, o_ref):
    o_ref[...] = w_ref[...].astype(BF16)


def _to_bf16(w):
    L, R, C = w.shape
    tr = _pick(R, (256, 128, 64, 32, 16))
    spec = pl.BlockSpec((None, tr, C), lambda l, i: (l, i, 0))
    return pl.pallas_call(
        _cast_kernel, grid=(L, R // tr), in_specs=[spec], out_specs=spec,
        out_shape=jax.ShapeDtypeStruct(w.shape, BF16),
        compiler_params=_params("arbitrary", "arbitrary"), name="cast_weights",
    )(w)


def _mod_kernel(c_ref, w_ref, b_ref, o_ref):
    c = c_ref[...]
    s = (c * jax.nn.sigmoid(c)).astype(BF16)
    o_ref[...] = _dot(s, w_ref[...].astype(BF16)) + b_ref[...]


def _modulation(cond, w_ada, b_ada):
    L, D, N = w_ada.shape
    tn = _pick(N, (1024, 512, 256, 128))
    return pl.pallas_call(
        _mod_kernel,
        grid=(L, N // tn),
        in_specs=[
            pl.BlockSpec((MOD_ROWS, D), lambda l, j: (0, 0)),
            pl.BlockSpec((None, D, tn), lambda l, j: (l, 0, j)),
            pl.BlockSpec((None, 1, tn), lambda l, j: (l, 0, j)),
        ],
        out_specs=pl.BlockSpec((None, MOD_ROWS, tn), lambda l, j: (l, 0, j)),
        out_shape=jax.ShapeDtypeStruct((L, MOD_ROWS, N), F32),
        compiler_params=_params("arbitrary", "arbitrary"),
        name="modulation",
    )(cond, w_ada, b_ada)


def _inproj_kernel(x_ref, sh_ref, sc_ref, g_ref, w_ref, z_ref):
    h = _rms(x_ref[...]) * g_ref[...] * (1.0 + sc_ref[...]) + sh_ref[...]
    z_ref[...] = _dot(h.astype(BF16), w_ref[...])


def _in_projection(x, mod3, mod_row, g, w, tm):
    T, D = x.shape
    N = w.shape[1]
    tn = _col_tile(N, MAX_WEIGHT_COLS)
    return pl.pallas_call(
        _inproj_kernel,
        grid=(N // tn, T // tm),
        in_specs=[
            pl.BlockSpec((tm, D), lambda j, i: (i, 0)),
            pl.BlockSpec((None, 1, D), lambda j, i: (mod_row(i), 0, 0)),
            pl.BlockSpec((None, 1, D), lambda j, i: (mod_row(i), 0, 1)),
            pl.BlockSpec((1, D), lambda j, i: (0, 0)),
            pl.BlockSpec((D, tn), lambda j, i: (0, j)),
        ],
        out_specs=pl.BlockSpec((tm, tn), lambda j, i: (i, j)),
        out_shape=jax.ShapeDtypeStruct((T, N), F32),
        compiler_params=_params("arbitrary", "arbitrary"),
        name="in_projection",
    )(x, mod3, mod3, g, w)


def _attn_ctx_kernel(q_ref, k_ref, v_ref, o_ref, *, heads):
    scale = HEAD_DIM ** -0.5
    outs = []
    for h in range(heads):
        sl = slice(h * HEAD_DIM, (h + 1) * HEAD_DIM)
        q = (q_ref[:, sl] * scale).astype(BF16)
        s = _dot_nt(q, k_ref[:, sl].astype(BF16))
        p = jnp.exp(s - jnp.max(s, axis=-1, keepdims=True))
        l = jnp.sum(p, axis=-1, keepdims=True)
        outs.append(_dot(p.astype(BF16), v_ref[:, sl].astype(BF16)) / l)
    o_ref[...] = jnp.concatenate(outs, axis=-1)


def _attention_context(z, seq, width):
    T = z.shape[0]
    col = lambda j: pl.BlockSpec((seq, width), lambda b: (b, j))
    return pl.pallas_call(
        functools.partial(_attn_ctx_kernel, heads=width // HEAD_DIM),
        grid=(T // seq,),
        in_specs=[col(0), col(1), col(2)],
        out_specs=col(0),
        out_shape=jax.ShapeDtypeStruct((T, width), F32),
        compiler_params=_params("arbitrary"),
        name="attention_context",
    )(z, z, z)


def _attn_lat_kernel(q_ref, k_ref, v_ref, kc_ref, vc_ref, bias_ref, o_ref, *, heads, rows, kr):
    scale = HEAD_DIM ** -0.5
    r = pl.program_id(1)
    r0 = jnp.clip(r - kr // 2, 0, rows - kr)
    start = pl.multiple_of(r0 * GRID_W, GRID_W)
    k_win = k_ref[pl.ds(start, kr * GRID_W), :]
    v_win = v_ref[pl.ds(start, kr * GRID_W), :]
    outs = []
    for h in range(heads):
        sl = slice(h * HEAD_DIM, (h + 1) * HEAD_DIM)
        q = (q_ref[:, sl] * scale).astype(BF16)
        s_loc = _dot_nt(q, k_win[:, sl].astype(BF16)) + bias_ref[h]
        s_ctx = _dot_nt(q, kc_ref[:, sl].astype(BF16))
        m = jnp.maximum(jnp.max(s_loc, axis=-1, keepdims=True), jnp.max(s_ctx, axis=-1, keepdims=True))
        p_loc = jnp.exp(s_loc - m)
        p_ctx = jnp.exp(s_ctx - m)
        l = jnp.sum(p_loc, axis=-1, keepdims=True) + jnp.sum(p_ctx, axis=-1, keepdims=True)
        o = _dot(p_loc.astype(BF16), v_win[:, sl].astype(BF16)) + _dot(p_ctx.astype(BF16), vc_ref[:, sl].astype(BF16))
        outs.append(o / l)
    o_ref[...] = jnp.concatenate(outs, axis=-1)


def _toeplitz_kernel(rpb_ref, pick_ref, valid_ref, o_ref):
    o_ref[...] = jnp.where(valid_ref[...] > 0.5, _dot_hi(rpb_ref[...], pick_ref[...]), MASK_NEG)


def _window_bias(rpb, kr):
    delta = np.arange(kr)[:, None]
    i = np.arange(kr)[None, :]
    row_off = (WIN_ROWS - 1) - delta + i
    q = np.arange(GRID_W)[:, None]
    kc = np.arange(GRID_W)[None, :]
    c0 = np.clip(q - WIN_COLS // 2, 0, GRID_W - WIN_COLS)
    valid = (kc >= c0) & (kc < c0 + WIN_COLS)
    col_off = kc - q + (WIN_COLS - 1)
    pick = (col_off[None] == np.arange(2 * WIN_COLS - 1)[:, None, None]) & valid[None]
    H, n_ro, n_co = rpb.shape
    toep = pl.pallas_call(
        _toeplitz_kernel,
        out_shape=jax.ShapeDtypeStruct((H * n_ro, GRID_W * GRID_W), F32),
        name="window_bias",
    )(rpb.astype(F32).reshape(H * n_ro, n_co), jnp.asarray(pick.reshape(n_co, -1), F32),
      jnp.asarray(valid.reshape(1, -1), F32))
    toep = toep.reshape(H, n_ro, GRID_W, GRID_W)
    per_delta = []
    for dl in range(kr):
        lo = int(row_off[dl, 0])
        win = toep[:, lo:lo + kr]
        per_delta.append(win.transpose(0, 2, 1, 3).reshape(rpb.shape[0], GRID_W, kr * GRID_W))
    return jnp.stack(per_delta, axis=0)


def _attention_latent(z, k_ctx, v_ctx, bias, dec_seq, width):
    B, P, _ = k_ctx.shape
    rows = dec_seq // GRID_W
    kr = bias.shape[0]
    heads = width // HEAD_DIM

    def bias_index(b, r):
        return (r - jnp.clip(r - kr // 2, 0, rows - kr), 0, 0, 0)

    q_spec = pl.BlockSpec((GRID_W, width), lambda b, r: (b * rows + r, 0))
    return pl.pallas_call(
        functools.partial(_attn_lat_kernel, heads=heads, rows=rows, kr=kr),
        grid=(B, rows),
        in_specs=[
            q_spec,
            pl.BlockSpec((dec_seq, width), lambda b, r: (b, 1)),
            pl.BlockSpec((dec_seq, width), lambda b, r: (b, 2)),
            pl.BlockSpec((None, P, width), lambda b, r: (b, 0, 0)),
            pl.BlockSpec((None, P, width), lambda b, r: (b, 0, 0)),
            pl.BlockSpec((None, heads, GRID_W, kr * GRID_W), bias_index),
        ],
        out_specs=q_spec,
        out_shape=jax.ShapeDtypeStruct((B * dec_seq, width), F32),
        compiler_params=_params("arbitrary", "arbitrary"),
        name="attention_latent",
    )(z, z, z, k_ctx, v_ctx, bias)


def _sgu_kernel(u_ref, v_ref, gain_ref, w_ref, b_ref, o_ref, *, groups, chunks):
    for c in range(chunks):
        rows = slice(c * SG_CHUNK, (c + 1) * SG_CHUNK)
        for g in range(groups):
            sl = slice(g * SG_GROUP_DIM, (g + 1) * SG_GROUP_DIM)
            vn = _rms(v_ref[rows, sl]) * gain_ref[:, sl]
            mixed = _dot(w_ref[g].astype(BF16), vn.astype(BF16)) + b_ref[:, g:g + 1]
            o_ref[rows, sl] = u_ref[rows, sl] * mixed


def _spatial_gating(z, col_u, gain, w_s, b_t):
    T = z.shape[0]
    G = w_s.shape[0]
    W = G * SG_GROUP_DIM
    tm = _pick(T, (512, 256, 128))
    return pl.pallas_call(
        functools.partial(_sgu_kernel, groups=G, chunks=tm // SG_CHUNK),
        grid=(T // tm,),
        in_specs=[pl.BlockSpec((tm, W), lambda i: (i, col_u)),
                  pl.BlockSpec((tm, W), lambda i: (i, col_u + 1)),
                  _full(gain), _full(w_s), _full(b_t)],
        out_specs=pl.BlockSpec((tm, W), lambda i: (i, 0)),
        out_shape=jax.ShapeDtypeStruct((T, W), F32),
        compiler_params=_params("arbitrary"),
        name="spatial_gating",
    )(z, z, gain, w_s, b_t)


def _softplus(y):
    return jnp.maximum(y, 0.0) + jnp.log(1.0 + jnp.exp(-jnp.abs(y)))


def _rwkv_pre_kernel(r_ref, k_ref, v_ref, lora_ref, w0_ref, w2_ref, a0_ref, a2_ref, kkw_ref, ka_ref,
                     bd_ref, tri_ref,
                     at_f, rt_f, bh_f, kh_f, wc_f, at_b, rt_b, bh_b, kh_b, wc_b, v_o, *, pairs, chunks, lora_w, lora_a):
    C = RW_CHUNK
    r = r_ref[...]
    k = k_ref[...]
    bd = bd_ref[...]
    kk = k * kkw_ref[...]
    kk2 = kk * kk
    ssq = jnp.concatenate([_dot_hi(kk2[:, p * LANES:(p + 1) * LANES], bd) for p in range(pairs)], axis=-1)
    kk = kk * lax.rsqrt(ssq + KK_EPS)
    tw = jnp.tanh(lora_ref[:, :lora_w]).astype(BF16)
    xa = lora_ref[:, lora_w:lora_w + lora_a].astype(BF16)
    v_bf = v_ref[...].astype(BF16)
    for p in range(pairs):
        v_o[p] = v_bf[:, p * LANES:(p + 1) * LANES]
    outs = ((at_f, rt_f, bh_f, kh_f, wc_f), (at_b, rt_b, bh_b, kh_b, wc_b))
    for d in range(2):
        at_o, rt_o, bh_o, kh_o, wc_o = outs[d]
        w_log = -_softplus(-(w0_ref[d:d + 1, :] + _dot(tw, w2_ref[d].astype(BF16)))) - 0.5
        logw = -jnp.exp(w_log)
        a_rate = jax.nn.sigmoid(a0_ref[d:d + 1, :] + _dot(xa, a2_ref[d].astype(BF16)))
        k_d = k * (1.0 + (a_rate - 1.0) * ka_ref[...])
        b = kk * a_rate
        tri = tri_ref[d]
        tot_row = C - 1 if d == 0 else 0
        for c in range(chunks):
            rows = slice(c * C, (c + 1) * C)
            lw = logw[rows]
            cum = _dot_hi(tri, lw)
            tot = cum[tot_row:tot_row + 1, :]
            e_dn = jnp.exp(tot - cum)
            at = -kk[rows] * jnp.exp(cum - lw - tot)
            rt = r[rows] * jnp.exp(cum - tot)
            bh = b[rows] * e_dn
            kh = k_d[rows] * e_dn
            wc = jnp.exp(tot)
            for p in range(pairs):
                sl = slice(p * LANES, (p + 1) * LANES)
                at_o[p, rows, :] = at[:, sl].astype(BF16)
                rt_o[p, rows, :] = rt[:, sl].astype(BF16)
                bh_o[p, rows, :] = bh[:, sl].astype(BF16)
                kh_o[p, rows, :] = kh[:, sl].astype(BF16)
                wc_o[c, p] = wc[:, sl]


def _rwkv_cols(z_cols, width):
    bw = math.gcd(z_cols, width)
    assert bw % LANES == 0
    return bw, z_cols // bw, (z_cols + width) // bw, (z_cols + 2 * width) // bw


def _rwkv_prepare(z, r_off, lora_blk, lora_dims, w0, w2, a0, a2, kkw, ka, bd, tri):
    T = z.shape[0]
    RW = w0.shape[1]
    lora_w, lora_a, lora_g = lora_dims
    lora_n = lora_w + lora_a + lora_g
    bw, cr, ck, cv = _rwkv_cols(r_off, RW)
    pairs = bw // LANES
    C = RW_CHUNK
    tm = _pick(T, (256, 128, 64))
    tok = lambda c0: pl.BlockSpec((tm, bw), lambda i, h: (i, c0 + h))
    par2 = lambda a: pl.BlockSpec((a.shape[0], bw), lambda i, h: (0, h))
    par3 = lambda a: pl.BlockSpec((a.shape[0], a.shape[1], bw), lambda i, h: (0, 0, h))
    packed = pl.BlockSpec((pairs, tm, LANES), lambda i, h: (h, i, 0))
    wc_spec = pl.BlockSpec((tm // C, pairs, 1, LANES), lambda i, h: (i, h, 0, 0))
    packed_shape = jax.ShapeDtypeStruct((RW // LANES, T, LANES), BF16)
    wc_shape = jax.ShapeDtypeStruct((T // C, RW // LANES, 1, LANES), F32)
    return pl.pallas_call(
        functools.partial(_rwkv_pre_kernel, pairs=pairs, chunks=tm // C, lora_w=lora_w, lora_a=lora_a),
        grid=(T // tm, RW // bw),
        in_specs=[tok(cr), tok(ck), tok(cv),
                  pl.BlockSpec((tm, lora_n), lambda i, h: (i, lora_blk)),
                  par2(w0), par3(w2), par2(a0), par3(a2), par2(kkw), par2(ka), _full(bd), _full(tri)],
        out_specs=[packed, packed, packed, packed, wc_spec, packed, packed, packed, packed, wc_spec, packed],
        out_shape=[packed_shape] * 4 + [wc_shape] + [packed_shape] * 4 + [wc_shape] + [packed_shape],
        compiler_params=_params("arbitrary", "arbitrary"),
        name="rwkv_prepare",
    )(z, z, z, z, w0, w2, a0, a2, kkw, ka, bd, tri)


def _rwkv_pair_chunk(at, rt, bh, kh, v, hs, masks):
    C = RW_CHUNK
    strict, incl, eye, same_head, lane_lo = masks
    zero = jnp.zeros_like(at)
    stack = lambda x: jnp.concatenate([jnp.where(lane_lo, x, zero), jnp.where(lane_lo, zero, x)], axis=0)
    twice = lambda x: jnp.concatenate([x, x], axis=0)
    atm, rtm, vm = stack(at), stack(rt), stack(v)
    bk = jnp.concatenate([twice(bh), twice(kh)], axis=0)
    a = _dot_nt(jnp.concatenate([atm, rtm], axis=0), bk)
    yield
    l = jnp.where(strict, a[:2 * C, :2 * C], 0.0)
    a_ak = jnp.where(strict, a[:2 * C, 2 * C:], 0.0).astype(BF16)
    a_rb = jnp.where(incl, a[2 * C:, :2 * C], 0.0).astype(BF16)
    a_rk = jnp.where(incl, a[2 * C:, 2 * C:], 0.0).astype(BF16)
    hs_b = hs.astype(BF16)
    x2 = _dot(jnp.concatenate([atm, a_ak], axis=1), jnp.concatenate([hs_b, vm], axis=0)).astype(BF16)
    t = eye + l
    pw = l.astype(BF16)
    pw = _dot(pw, pw).astype(BF16)
    yield
    for _ in range(int(math.log2(C)) - 2):
        both = _dot(jnp.concatenate([t.astype(BF16), pw], axis=0), pw)
        yield
        t = t + both[:2 * C]
        pw = both[2 * C:].astype(BF16)
    t = t + _dot(t.astype(BF16), pw)
    yield
    u = _dot(t.astype(BF16), x2).astype(BF16)
    yield
    o = _dot(jnp.concatenate([rtm, a_rb, a_rk], axis=1), jnp.concatenate([hs_b, u, vm], axis=0))
    h_new = hs + jnp.where(same_head, _dot_tn(bk, jnp.concatenate([u, vm], axis=0)), 0.0)
    return o[:C] + o[C:], h_new


def _lock_step(gens):
    results = [None] * len(gens)
    live = list(enumerate(gens))
    while live:
        still = []
        for i, g in live:
            try:
                next(g)
                still.append((i, g))
            except StopIteration as done:
                results[i] = done.value
        live = still
    return results


def _rwkv_core_kernel(at_f, rt_f, bh_f, kh_f, v_f, wc_f, at_b, rt_b, bh_b, kh_b, v_b, wc_b, h0_ref,
                      of_ref, ob_ref, hs_ref, *, pairs, group):
    C = RW_CHUNK

    @pl.when(pl.program_id(1) == 0)
    def _():
        hs_ref[...] = h0_ref[...]

    row = _iota((2 * C, 2 * C), 0)
    col = _iota((2 * C, 2 * C), 1)
    same = (row // C) == (col // C)
    rt_, ct_ = row % C, col % C
    eye = jnp.where(row == col, 1.0, 0.0).astype(F32)
    lrow = _iota((LANES, LANES), 0)
    lcol = _iota((LANES, LANES), 1)
    same_head = (lrow // HEAD_DIM) == (lcol // HEAD_DIM)
    lane_lo = _iota((C, LANES), 1) < HEAD_DIM
    masks = ((same & (ct_ < rt_), same & (ct_ <= rt_), eye, same_head, lane_lo),
             (same & (ct_ > rt_), same & (ct_ >= rt_), eye, same_head, lane_lo))
    dirs = ((at_f, rt_f, bh_f, kh_f, v_f, wc_f, of_ref), (at_b, rt_b, bh_b, kh_b, v_b, wc_b, ob_ref))

    for g0 in range(0, pairs, group):
        jobs = [(d, p) for p in range(g0, g0 + group) for d in range(2)]
        gens = []
        for d, p in jobs:
            at_r, rt_r, bh_r, kh_r, v_r, wc_r, _ = dirs[d]
            wcol = jnp.sum(jnp.where(lrow == lcol, jnp.broadcast_to(wc_r[p], (LANES, LANES)), 0.0),
                           axis=1, keepdims=True)
            gens.append(_rwkv_pair_chunk(at_r[p], rt_r[p], bh_r[p], kh_r[p], v_r[p], hs_ref[d, p] * wcol, masks[d]))
        for (d, p), (o, h_new) in zip(jobs, _lock_step(gens)):
            hs_ref[d, p] = h_new
            dirs[d][-1][p] = o


def _pair_states(h):
    n, _, H, _, _ = h.shape
    hp = h.reshape(n, 2, H // 2, 2, HEAD_DIM, HEAD_DIM)
    z = jnp.zeros_like(hp[:, :, :, 0])
    top = jnp.concatenate([hp[:, :, :, 0], z], axis=-1)
    bot = jnp.concatenate([z, hp[:, :, :, 1]], axis=-1)
    return jnp.concatenate([top, bot], axis=-2)


def _unpair_states(hp):
    n, _, P, _, _ = hp.shape
    h0 = hp[:, :, :, :HEAD_DIM, :HEAD_DIM]
    h1 = hp[:, :, :, HEAD_DIM:, HEAD_DIM:]
    return jnp.stack([h0, h1], axis=3).reshape(n, 2, 2 * P, HEAD_DIM, HEAD_DIM)


def _rwkv_core(pre, h0, n_seq, seq_len):
    at_f, rt_f, bh_f, kh_f, wc_f, at_b, rt_b, bh_b, kh_b, wc_b, v = pre
    pairs, T, _ = at_f.shape
    C = RW_CHUNK
    nch = seq_len // C
    fwd = lambda s, c: s * nch + c
    bwd = lambda s, c: s * nch + (nch - 1 - c)
    pk = lambda f: pl.BlockSpec((pairs, C, LANES), lambda s, c: (0, f(s, c), 0))
    wc = lambda f: pl.BlockSpec((None, pairs, 1, LANES), lambda s, c: (f(s, c), 0, 0, 0))
    st = pl.BlockSpec((None, 2, pairs, LANES, LANES), lambda s, c: (s, 0, 0, 0, 0))
    o_shape = jax.ShapeDtypeStruct((pairs, T, LANES), F32)
    return pl.pallas_call(
        functools.partial(_rwkv_core_kernel, pairs=pairs, group=_pick(pairs, (RW_PAIR_GROUP, 4, 2, 1))),
        grid=(n_seq, nch),
        in_specs=[pk(fwd), pk(fwd), pk(fwd), pk(fwd), pk(fwd), wc(fwd),
                  pk(bwd), pk(bwd), pk(bwd), pk(bwd), pk(bwd), wc(bwd), st],
        out_specs=[pk(fwd), pk(bwd), st],
        out_shape=[o_shape, o_shape, jax.ShapeDtypeStruct(h0.shape, F32)],
        compiler_params=_params("arbitrary", "arbitrary"),
        name="rwkv_scan",
    )(at_f, rt_f, bh_f, kh_f, v, wc_f, at_b, rt_b, bh_b, kh_b, v, wc_b, h0)


def _rwkv_post_kernel(of_ref, ob_ref, r_ref, k_ref, v_ref, lora_ref, g2_ref, lnw_ref, lnb_ref, rk_ref, bd_ref,
                      o_ref, *, pairs, lora_g):
    bd = bd_ref[...]
    n_lora = lora_ref.shape[1]
    sg = jax.nn.sigmoid(lora_ref[:, n_lora - lora_g:]).astype(BF16)
    inv = 1.0 / HEAD_DIM
    for p in range(pairs):
        sl = slice(p * LANES, (p + 1) * LANES)
        o = of_ref[p] + ob_ref[p]
        mu = _dot_hi(o, bd) * inv
        dlt = o - mu
        var = _dot_hi(dlt * dlt, bd) * inv
        on = dlt * lax.rsqrt(var + GN_EPS) * lnw_ref[:, sl] + lnb_ref[:, sl]
        bonus = _dot_hi(r_ref[:, sl] * k_ref[:, sl] * rk_ref[:, sl], bd) * v_ref[:, sl]
        gate = _dot(sg, g2_ref[:, sl].astype(BF16))
        o_ref[:, sl] = (on + bonus) * gate


def _rwkv_finish(o_f, o_b, z, r_off, lora_blk, lora_dims, g2, lnw, lnb, rk, bd):
    T = z.shape[0]
    RW = g2.shape[1]
    lora_n = sum(lora_dims)
    bw, cr, ck, cv = _rwkv_cols(r_off, RW)
    pairs = bw // LANES
    tm = _pick(T, (256, 128, 64))
    tok = lambda c0: pl.BlockSpec((tm, bw), lambda i, h: (i, c0 + h))
    par2 = lambda a: pl.BlockSpec((a.shape[0], bw), lambda i, h: (0, h))
    packed = pl.BlockSpec((pairs, tm, LANES), lambda i, h: (h, i, 0))
    return pl.pallas_call(
        functools.partial(_rwkv_post_kernel, pairs=pairs, lora_g=lora_dims[2]),
        grid=(T // tm, RW // bw),
        in_specs=[packed, packed, tok(cr), tok(ck), tok(cv),
                  pl.BlockSpec((tm, lora_n), lambda i, h: (i, lora_blk)),
                  par2(g2), par2(lnw), par2(lnb), par2(rk), _full(bd)],
        out_specs=pl.BlockSpec((tm, bw), lambda i, h: (i, h)),
        out_shape=jax.ShapeDtypeStruct((T, RW), F32),
        compiler_params=_params("arbitrary", "arbitrary"),
        name="rwkv_finish",
    )(o_f, o_b, z, z, z, z, g2, lnw, lnb, rk, bd)


def _outproj_kernel(oa_ref, ob_ref, oc_ref, x_ref, w_ref, g1_ref, sh2_ref, sc2_ref, gpost_ref, gpre_ref, wr_ref,
                    xo_ref, h2_ref, aff_ref):
    cat = jnp.concatenate([oa_ref[...], ob_ref[...], oc_ref[...]], axis=-1).astype(BF16)
    mixed = _dot(cat, w_ref[...])
    x = x_ref[...] + g1_ref[...] * (_rms(mixed) * gpost_ref[...])
    xo_ref[...] = x
    h2 = _rms(x) * gpre_ref[...] * (1.0 + sc2_ref[...]) + sh2_ref[...]
    h2_ref[...] = h2.astype(BF16)
    logits = _dot_hi(h2, wr_ref[...])
    e = jnp.exp(logits - jnp.max(logits, axis=-1, keepdims=True))
    aff_ref[...] = e / jnp.sum(e, axis=-1, keepdims=True)


def _out_projection(o_a, o_b, o_c, x, w_out, mod3, mod_row, g_post, g_pre, w_router, tm):
    T, D = x.shape
    E = w_router.shape[1]
    row = lambda a: pl.BlockSpec((tm, a.shape[1]), lambda i: (i, 0))
    mod = lambda col: pl.BlockSpec((None, 1, D), lambda i: (mod_row(i), 0, col))
    return pl.pallas_call(
        _outproj_kernel,
        grid=(T // tm,),
        in_specs=[row(o_a), row(o_b), row(o_c), row(x), _full(w_out),
                  mod(2), mod(3), mod(4), _full(g_post), _full(g_pre), _full(w_router)],
        out_specs=[pl.BlockSpec((tm, D), lambda i: (i, 0)),
                   pl.BlockSpec((tm, D), lambda i: (i, 0)),
                   pl.BlockSpec((tm, E), lambda i: (i, 0))],
        out_shape=[jax.ShapeDtypeStruct((T, D), F32), jax.ShapeDtypeStruct((T, D), BF16),
                   jax.ShapeDtypeStruct((T, E), F32)],
        compiler_params=_params("arbitrary"),
        name="out_projection",
    )(o_a, o_b, o_c, x, w_out, mod3, mod3, mod3, g_post, g_pre, w_router)


def _select_kernel(a_ref, ut_ref, pos_ref, *, cap):
    a = a_ref[...]
    R, T = a.shape
    lo = jnp.zeros((R, 1), jnp.int32)
    for bit in range(30, -1, -1):
        cand = lo | (1 << bit)
        cnt = jnp.sum(jnp.where(a >= lax.bitcast_convert_type(cand, F32), 1.0, 0.0), axis=1, keepdims=True)
        lo = jnp.where(cnt >= cap, cand, lo)
    thr = lax.bitcast_convert_type(lo, F32)
    gt = a > thr
    eq = a == thr
    need = cap - jnp.sum(jnp.where(gt, 1.0, 0.0), axis=1, keepdims=True)
    ut = ut_ref[...]
    nblk = T // LANES
    tie_carry = jnp.zeros((R, 1), F32)
    pos_carry = jnp.zeros((R, 1), F32)
    for n in range(nblk):
        sl = slice(n * LANES, (n + 1) * LANES)
        eq_b = jnp.where(eq[:, sl], 1.0, 0.0)
        tie_rank = _dot(eq_b.astype(BF16), ut) + tie_carry
        tie_carry = tie_carry + jnp.sum(eq_b, axis=1, keepdims=True)
        sel = jnp.where(gt[:, sl], 1.0, jnp.where(tie_rank < need, eq_b, 0.0))
        pos = _dot(sel.astype(BF16), ut) + pos_carry
        pos_carry = pos_carry + jnp.sum(sel, axis=1, keepdims=True)
        pos_ref[:, sl] = jnp.where(sel > 0.5, pos, -1.0).astype(jnp.int32)


def _select(aff_rows, cap, ut):
    R, T = aff_rows.shape
    tr = _pick(R, (128, 64, 32, 16, 8))
    return pl.pallas_call(
        functools.partial(_select_kernel, cap=cap),
        grid=(R // tr,),
        in_specs=[pl.BlockSpec((tr, T), lambda i: (i, 0)), pl.BlockSpec((LANES, LANES), lambda i: (0, 0))],
        out_specs=pl.BlockSpec((tr, T), lambda i: (i, 0)),
        out_shape=jax.ShapeDtypeStruct((R, T), jnp.int32),
        compiler_params=_params("arbitrary"),
        name="expert_select",
    )(aff_rows, ut)


def _route(aff, n_sets, set_len, cap, ut):
    E = aff.shape[1]
    rows = aff.reshape(n_sets, set_len, E).transpose(0, 2, 1).reshape(n_sets * E, set_len)
    pos = _select(rows, cap, ut).reshape(n_sets, E, set_len)
    return pos, pos.transpose(0, 2, 1).reshape(n_sets * set_len, E)


def _one_hot_rows(pos_row, cap):
    T = pos_row.shape[1]
    return jnp.where(_iota((cap, T), 0) == pos_row, 1.0, 0.0).astype(BF16)


def _gather_ctx_kernel(h_ref, pos_ref, xs_ref, *, experts, cap):
    h = h_ref[...]
    for e in range(experts):
        xs_ref[e] = _dot(_one_hot_rows(pos_ref[e:e + 1, :], cap), h).astype(BF16)


def _gather_lat_kernel(h_ref, pos_ref, xs_ref, *, cap):
    xs_ref[...] = _dot(_one_hot_rows(pos_ref[...], cap), h_ref[...]).astype(BF16)


def _expert_up_kernel(xc_ref, xl_ref, wg_ref, wu_ref, hid_ref):
    wg = wg_ref[...].astype(BF16)
    wu = wu_ref[...].astype(BF16)
    rc = xc_ref.shape[0]
    for ref, rows in ((xc_ref, slice(0, rc)), (xl_ref, slice(rc, hid_ref.shape[0]))):
        x = ref[...]
        g = _dot(x, wg)
        hid_ref[rows, :] = (g * jax.nn.sigmoid(g) * _dot(x, wu)).astype(BF16)


def _expert_down_kernel(hid_ref, wd_ref, y_ref):
    y_ref[...] = _dot(hid_ref[...], wd_ref[...].astype(BF16)).astype(BF16)


def _scatter_term(pos_col, gate_col, y, cap):
    tm = pos_col.shape[0]
    pt = jnp.where(_iota((tm, cap), 1).astype(F32) == pos_col, 1.0, 0.0).astype(BF16)
    return gate_col * _dot(pt, y)


def _ffn_residual(x, ffn, g2, g_post):
    return x + g2 * (_rms(ffn) * g_post)


def _combine_ctx_kernel(x_ref, y_ref, pos_ref, aff_ref, g2_ref, gpost_ref, o_ref, *, experts, cap):
    pos = pos_ref[...].astype(F32)
    aff = aff_ref[...]
    acc = jnp.zeros(x_ref.shape, F32)
    for e in range(experts):
        acc = acc + _scatter_term(pos[:, e:e + 1], aff[:, e:e + 1], y_ref[e], cap)
    o_ref[...] = _ffn_residual(x_ref[...], acc, g2_ref[...], gpost_ref[...])


def _combine_lat_kernel(x_ref, y_ref, pos_ref, aff_ref, g2_ref, gpost_ref, o_ref, acc_scr, *, cap):
    e = pl.program_id(2)

    @pl.when(e == 0)
    def _():
        acc_scr[...] = jnp.zeros(acc_scr.shape, F32)

    mine = _iota(pos_ref.shape, 1) == e
    pos_col = jnp.sum(jnp.where(mine, pos_ref[...].astype(F32), 0.0), axis=1, keepdims=True)
    gate_col = jnp.sum(jnp.where(mine, aff_ref[...], 0.0), axis=1, keepdims=True)
    acc_scr[...] += _scatter_term(pos_col, gate_col, y_ref[...], cap)

    @pl.when(e == pl.num_programs(2) - 1)
    def _():
        o_ref[...] = _ffn_residual(x_ref[...], acc_scr[...], g2_ref[...], gpost_ref[...])


def _expert_ffn(x_c, h2_c, aff_c, x_l, h2_l, aff_l, mod3, g_post, w_gate, w_up, w_down, ut, dims):
    batch, seq, dec_batch, dec_seq = dims
    D = x_c.shape[1]
    E, _, F = w_gate.shape
    cap_c = CAPACITY_FACTOR * seq // E
    cap_l = CAPACITY_FACTOR * dec_seq // E
    rows_c = batch * cap_c
    rows_l = dec_batch * cap_l
    rows = rows_c + rows_l
    assert rows_c % cap_l == 0
    lat_row_blk = rows_c // cap_l

    pos_c, post_c = _route(aff_c, batch, seq, cap_c, ut)
    pos_l, post_l = _route(aff_l, dec_batch, dec_seq, cap_l, ut)

    xs_c = pl.pallas_call(
        functools.partial(_gather_ctx_kernel, experts=E, cap=cap_c),
        grid=(batch,),
        in_specs=[pl.BlockSpec((seq, D), lambda b: (b, 0)), pl.BlockSpec((None, E, seq), lambda b: (b, 0, 0))],
        out_specs=pl.BlockSpec((E, cap_c, D), lambda b: (0, b, 0)),
        out_shape=jax.ShapeDtypeStruct((E, rows_c, D), BF16),
        compiler_params=_params("arbitrary"),
        name="expert_gather_context",
    )(h2_c, pos_c)
    xs_l = pl.pallas_call(
        functools.partial(_gather_lat_kernel, cap=cap_l),
        grid=(dec_batch, E),
        in_specs=[pl.BlockSpec((dec_seq, D), lambda b, e: (b, 0)),
                  pl.BlockSpec((None, None, 1, dec_seq), lambda b, e: (b, e, 0, 0))],
        out_specs=pl.BlockSpec((None, cap_l, D), lambda b, e: (e, b, 0)),
        out_shape=jax.ShapeDtypeStruct((E, rows_l, D), BF16),
        compiler_params=_params("arbitrary", "arbitrary"),
        name="expert_gather_latent",
    )(h2_l, pos_l.reshape(dec_batch, E, 1, dec_seq))

    tf = _pick(F, (512, 256, 128))
    hid = pl.pallas_call(
        _expert_up_kernel,
        grid=(E, F // tf),
        in_specs=[pl.BlockSpec((None, rows_c, D), lambda e, n: (e, 0, 0)),
                  pl.BlockSpec((None, rows_l, D), lambda e, n: (e, 0, 0)),
                  pl.BlockSpec((None, D, tf), lambda e, n: (e, 0, n)),
                  pl.BlockSpec((None, D, tf), lambda e, n: (e, 0, n))],
        out_specs=pl.BlockSpec((None, rows, tf), lambda e, n: (e, 0, n)),
        out_shape=jax.ShapeDtypeStruct((E, rows, F), BF16),
        compiler_params=_params("arbitrary", "arbitrary"),
        name="expert_up",
    )(xs_c, xs_l, w_gate, w_up)
    td = _pick(D, (512, 256, 128))
    y = pl.pallas_call(
        _expert_down_kernel,
        grid=(E, D // td),
        in_specs=[pl.BlockSpec((None, rows, F), lambda e, n: (e, 0, 0)),
                  pl.BlockSpec((None, F, td), lambda e, n: (e, 0, n))],
        out_specs=pl.BlockSpec((None, rows, td), lambda e, n: (e, 0, n)),
        out_shape=jax.ShapeDtypeStruct((E, rows, D), BF16),
        compiler_params=_params("arbitrary", "arbitrary"),
        name="expert_down",
    )(hid, w_down)

    out_c = pl.pallas_call(
        functools.partial(_combine_ctx_kernel, experts=E, cap=cap_c),
        grid=(batch,),
        in_specs=[pl.BlockSpec((seq, D), lambda b: (b, 0)),
                  pl.BlockSpec((E, cap_c, D), lambda b: (0, b, 0)),
                  pl.BlockSpec((seq, E), lambda b: (b, 0)),
                  pl.BlockSpec((seq, E), lambda b: (b, 0)),
                  pl.BlockSpec((None, 1, D), lambda b: (0, 0, 5)),
                  _full(g_post)],
        out_specs=pl.BlockSpec((seq, D), lambda b: (b, 0)),
        out_shape=jax.ShapeDtypeStruct(x_c.shape, F32),
        compiler_params=_params("arbitrary"),
        name="expert_combine_context",
    )(x_c, y, post_c, aff_c, mod3, g_post)
    tm = _pick(dec_seq, (512, 256, 128))
    nt = dec_seq // tm
    tok = lambda b, i, e: (b * nt + i, 0)
    out_l = pl.pallas_call(
        functools.partial(_combine_lat_kernel, cap=cap_l),
        grid=(dec_batch, nt, E),
        in_specs=[pl.BlockSpec((tm, D), tok),
                  pl.BlockSpec((None, cap_l, D), lambda b, i, e: (e, lat_row_blk + b, 0)),
                  pl.BlockSpec((tm, E), tok),
                  pl.BlockSpec((tm, E), tok),
                  pl.BlockSpec((None, 1, D), lambda b, i, e: (1 + b, 0, 5)),
                  _full(g_post)],
        out_specs=pl.BlockSpec((tm, D), tok),
        out_shape=jax.ShapeDtypeStruct(x_l.shape, F32),
        scratch_shapes=[pltpu.VMEM((tm, D), F32)],
        compiler_params=_params("arbitrary", "arbitrary", "arbitrary"),
        name="expert_combine_latent",
    )(x_l, y, post_l, aff_l, mod3, g_post)
    return out_c, out_l


def kernel(x_prompt, x_sample, cache_k, cache_v, state_wkv, c, c_ctx, w_ada, b_ada, g_pre_mix, g_post_mix, g_pre_ffn, g_post_ffn, w_in, na_rpb, sg_gain, sg_w, sg_b, rw_w0, rw_w2, rw_a0, rw_a2, rw_g2, rw_kk, rw_ka, rw_rk, rw_ln_w, rw_ln_b, w_out, w_router, w_gate, w_up, w_down):
    batch, seq, D = x_prompt.shape
    dec_batch, dec_seq, _ = x_sample.shape
    depth = w_ada.shape[0]
    past = cache_k.shape[2]
    na_heads = cache_k.shape[3]
    naw = na_heads * HEAD_DIM
    sgw = sg_gain.shape[1]
    rw_heads = state_wkv.shape[3]
    rww = rw_heads * HEAD_DIM
    lora_dims = (rw_w2.shape[2], rw_a2.shape[2], rw_g2.shape[1])
    n_ctx = batch * seq
    n_lat = dec_batch * dec_seq
    assert 1 + dec_batch <= MOD_ROWS and rww % LANES == 0
    assert seq % RW_CHUNK == 0 and dec_seq % RW_CHUNK == 0 and dec_seq % GRID_W == 0
    sg_off = 3 * naw
    rw_off = sg_off + 2 * sgw
    lora_off = rw_off + 3 * rww
    assert sg_off % sgw == 0 and lora_off % sum(lora_dims) == 0
    col_u = sg_off // sgw
    lora_blk = lora_off // sum(lora_dims)

    pair_ones = jnp.asarray(np.kron(np.eye(LANES // HEAD_DIM), np.ones((HEAD_DIM, HEAD_DIM))), F32)
    tril = np.tril(np.ones((RW_CHUNK, RW_CHUNK)))
    tri = jnp.asarray(np.stack([tril, tril.T]), F32)
    ut = jnp.asarray(np.triu(np.ones((LANES, LANES)), 1), BF16)
    kr = min(WIN_ROWS, dec_seq // GRID_W)

    cond = jnp.zeros((MOD_ROWS, D), F32).at[0].set(c_ctx).at[1:1 + dec_batch].set(c)
    mods = _modulation(cond, w_ada, b_ada.reshape(depth, 1, N_MOD * D))
    w_in_b = _to_bf16(w_in)
    w_out_b = _to_bf16(w_out)

    tm_c = _pick(n_ctx, (512, 256, 128))
    tm_l = _pick(dec_seq, (512, 256, 128))
    ctx_row = lambda i: 0
    lat_row = lambda i: 1 + (i * tm_l) // dec_seq

    x_c = x_prompt.reshape(n_ctx, D)
    x_l = x_sample.reshape(n_lat, D)
    h0_ctx = jnp.zeros((batch, 2, rww // LANES, LANES, LANES), F32)
    ks, vs, ss = [], [], []
    for l in range(depth):
        mod3 = mods[l].reshape(MOD_ROWS, 1, N_MOD * D)
        g_pre = g_pre_mix[l].reshape(1, D)
        rw_args = (rw_w0[l], rw_w2[l], rw_a0[l], rw_a2[l], rw_kk[l].reshape(1, rww), rw_ka[l].reshape(1, rww),
                   pair_ones, tri)
        fin_args = (rw_g2[l], rw_ln_w[l].reshape(1, rww), rw_ln_b[l].reshape(1, rww), rw_rk[l].reshape(1, rww),
                    pair_ones)
        sg_args = (sg_gain[l].reshape(1, sgw), sg_w[l], sg_b[l].T)

        def mixers(x, tm, mod_row, attend, h0, n_seq, seq_len):
            z = _in_projection(x, mod3, mod_row, g_pre, w_in_b[l], tm)
            o_a = attend(z)
            o_b = _spatial_gating(z, col_u, *sg_args)
            pre = _rwkv_prepare(z, rw_off, lora_blk, lora_dims, *rw_args)
            o_f, o_bk, s_fin = _rwkv_core(pre, h0, n_seq, seq_len)
            o_c = _rwkv_finish(o_f, o_bk, z, rw_off, lora_blk, lora_dims, *fin_args)
            x, h2, aff = _out_projection(o_a, o_b, o_c, x, w_out_b[l], mod3, mod_row,
                                         g_post_mix[l].reshape(1, D), g_pre_ffn[l].reshape(1, D), w_router[l], tm)
            return z, x, h2, aff, s_fin

        z_c, x_c, h2_c, aff_c, s_ctx = mixers(
            x_c, tm_c, ctx_row, lambda z: _attention_context(z, seq, naw), h0_ctx, batch, seq)
        ks.append(z_c[:, naw:2 * naw].reshape(batch, seq, na_heads, HEAD_DIM))
        vs.append(z_c[:, 2 * naw:3 * naw].reshape(batch, seq, na_heads, HEAD_DIM))
        ss.append(jnp.swapaxes(_unpair_states(s_ctx), -1, -2))

        bias = _window_bias(na_rpb[l], kr)
        k_past = cache_k[:, l].reshape(dec_batch, past, naw)
        v_past = cache_v[:, l].reshape(dec_batch, past, naw)
        h0_lat = _pair_states(jnp.swapaxes(state_wkv[:, l], -1, -2))
        _, x_l, h2_l, aff_l, _ = mixers(
            x_l, tm_l, lat_row, lambda z: _attention_latent(z, k_past, v_past, bias, dec_seq, naw),
            h0_lat, dec_batch, dec_seq)

        x_c, x_l = _expert_ffn(x_c, h2_c, aff_c, x_l, h2_l, aff_l, mod3, g_post_ffn[l].reshape(1, D),
                               w_gate[l], w_up[l], w_down[l], ut, (batch, seq, dec_batch, dec_seq))

    return (x_c.reshape(batch, seq, D), x_l.reshape(dec_batch, dec_seq, D),
            jnp.stack(ks, axis=1), jnp.stack(vs, axis=1), jnp.stack(ss, axis=1))
```

```python
import functools
import math

import numpy as np
import jax
import jax.numpy as jnp
from jax import lax
from jax.experimental import pallas as pl
from jax.experimental.pallas import tpu as pltpu

F32 = jnp.float32
BF16 = jnp.bfloat16
HIGHEST = lax.Precision.HIGHEST

HEAD_DIM = 64
LANES = 128
GRID_W = 64
WIN_ROWS = 8
WIN_COLS = 16
SG_CHUNK = 128
SG_GROUP_DIM = 128
N_MOD = 6
CAPACITY_FACTOR = 2
NORM_EPS = 1e-6
GN_EPS = 64e-5
KK_EPS = 1e-12
RW_CHUNK = 64
RW_PAIR_GROUP = 8
MASK_NEG = -1e30
MOD_ROWS = 8
VMEM_LIMIT_BYTES = 56 * 1024 * 1024
MAX_WEIGHT_COLS = 3072


def _pick(n, prefs):
    for p in prefs:
        if n % p == 0:
            return p
    raise ValueError(f"no tile in {prefs} divides {n}")


def _col_tile(n, cap):
    tiles = [t for t in range(LANES, min(n, cap) + 1, LANES) if n % t == 0]
    return tiles[-1]


def _params(*sem):
    return pltpu.CompilerParams(dimension_semantics=sem, vmem_limit_bytes=VMEM_LIMIT_BYTES)


def _rms(x):
    return x * lax.rsqrt(jnp.mean(x * x, axis=-1, keepdims=True) + NORM_EPS)


def _dot(a, b):
    return jnp.dot(a, b, preferred_element_type=F32)


def _dot_nt(a, b):
    return lax.dot_general(a, b, (((1,), (1,)), ((), ())), preferred_element_type=F32)


def _dot_tn(a, b):
    return lax.dot_general(a, b, (((0,), (0,)), ((), ())), preferred_element_type=F32)


def _dot_hi(a, b):
    return jnp.dot(a, b, preferred_element_type=F32, precision=HIGHEST)


def _split2(a):
    hi = a.astype(BF16)
    return hi, (a - hi.astype(F32)).astype(BF16)


def _dot_wide_lhs(a, b):
    hi, lo = _split2(a)
    return _dot(hi, b) + _dot(lo, b)


def _dot_wide_rhs(a, b):
    hi, lo = _split2(b)
    return _dot(a, hi) + _dot(a, lo)


def _lock_step(gens):
    results = [None] * len(gens)
    live = list(enumerate(gens))
    while live:
        still = []
        for i, g in live:
            try:
                next(g)
                still.append((i, g))
            except StopIteration as done:
                results[i] = done.value
        live = still
    return results


def _iota(shape, axis):
    return lax.broadcasted_iota(jnp.int32, shape, axis)


def _full(a):
    return pl.BlockSpec(a.shape, lambda *_: (0,) * a.ndim)


def _cast_kernel(w_ref, o_ref):
    o_ref[...] = w_ref[...].astype(BF16)


def _to_bf16(w):
    L, R, C = w.shape
    tr = _pick(R, (256, 128, 64, 32, 16))
    spec = pl.BlockSpec((None, tr, C), lambda l, i: (l, i, 0))
    return pl.pallas_call(
        _cast_kernel, grid=(L, R // tr), in_specs=[spec], out_specs=spec,
        out_shape=jax.ShapeDtypeStruct(w.shape, BF16),
        compiler_params=_params("arbitrary", "arbitrary"), name="cast_weights",
    )(w)


def _mod_kernel(c_ref, w_ref, b_ref, o_ref):
    c = c_ref[...]
    s = (c * jax.nn.sigmoid(c)).astype(BF16)
    o_ref[...] = _dot(s, w_ref[...].astype(BF16)) + b_ref[...]


def _modulation(cond, w_ada, b_ada):
    L, D, N = w_ada.shape
    tn = _pick(N, (1024, 512, 256, 128))
    return pl.pallas_call(
        _mod_kernel,
        grid=(L, N // tn),
        in_specs=[
            pl.BlockSpec((MOD_ROWS, D), lambda l, j: (0, 0)),
            pl.BlockSpec((None, D, tn), lambda l, j: (l, 0, j)),
            pl.BlockSpec((None, 1, tn), lambda l, j: (l, 0, j)),
        ],
        out_specs=pl.BlockSpec((None, MOD_ROWS, tn), lambda l, j: (l, 0, j)),
        out_shape=jax.ShapeDtypeStruct((L, MOD_ROWS, N), F32),
        compiler_params=_params("arbitrary", "arbitrary"),
        name="modulation",
    )(cond, w_ada, b_ada)


def _inproj_kernel(x_ref, sh_ref, sc_ref, g_ref, w_ref, z_ref):
    h = _rms(x_ref[...]) * g_ref[...] * (1.0 + sc_ref[...]) + sh_ref[...]
    z_ref[...] = _dot(h.astype(BF16), w_ref[...])


def _in_projection(x, mod3, mod_row, g, w, layer, tm):
    T, D = x.shape
    N = w.shape[2]
    tn = _col_tile(N, MAX_WEIGHT_COLS)
    return pl.pallas_call(
        _inproj_kernel,
        grid=(N // tn, T // tm),
        in_specs=[
            pl.BlockSpec((tm, D), lambda j, i: (i, 0)),
            pl.BlockSpec((None, 1, D), lambda j, i: (mod_row(i), 0, 0)),
            pl.BlockSpec((None, 1, D), lambda j, i: (mod_row(i), 0, 1)),
            pl.BlockSpec((1, D), lambda j, i: (0, 0)),
            pl.BlockSpec((None, D, tn), lambda j, i: (layer, 0, j)),
        ],
        out_specs=pl.BlockSpec((tm, tn), lambda j, i: (i, j)),
        out_shape=jax.ShapeDtypeStruct((T, N), F32),
        compiler_params=_params("arbitrary", "arbitrary"),
        name="in_projection",
    )(x, mod3, mod3, g, w)


def _attn_head(q, keys, values, biases):
    s = [_dot_nt(q, k) if b is None else _dot_nt(q, k) + b for k, b in zip(keys, biases)]
    yield
    m = functools.reduce(jnp.maximum, [jnp.max(x, axis=-1, keepdims=True) for x in s])
    p = [jnp.exp(x - m) for x in s]
    l = sum(jnp.sum(x, axis=-1, keepdims=True) for x in p)
    o = sum(_dot(x.astype(BF16), v) for x, v in zip(p, values))
    yield
    return o / l


def _attn_ctx_kernel(q_ref, k_ref, v_ref, o_ref, *, heads):
    scale = HEAD_DIM ** -0.5
    gens = []
    for h in range(heads):
        sl = slice(h * HEAD_DIM, (h + 1) * HEAD_DIM)
        q = (q_ref[:, sl] * scale).astype(BF16)
        gens.append(_attn_head(q, [k_ref[:, sl].astype(BF16)], [v_ref[:, sl].astype(BF16)], [None]))
    o_ref[...] = jnp.concatenate(_lock_step(gens), axis=-1)


def _attention_context(z, seq, width):
    T = z.shape[0]
    col = lambda j: pl.BlockSpec((seq, width), lambda b: (b, j))
    return pl.pallas_call(
        functools.partial(_attn_ctx_kernel, heads=width // HEAD_DIM),
        grid=(T // seq,),
        in_specs=[col(0), col(1), col(2)],
        out_specs=col(0),
        out_shape=jax.ShapeDtypeStruct((T, width), F32),
        compiler_params=_params("arbitrary"),
        name="attention_context",
    )(z, z, z)


def _attn_lat_kernel(q_ref, k_ref, v_ref, kc_ref, vc_ref, bias_ref, o_ref, *, heads, rows, kr):
    scale = HEAD_DIM ** -0.5
    r = pl.program_id(1)
    r0 = jnp.clip(r - kr // 2, 0, rows - kr)
    start = pl.multiple_of(r0 * GRID_W, GRID_W)
    k_win = k_ref[pl.ds(start, kr * GRID_W), :]
    v_win = v_ref[pl.ds(start, kr * GRID_W), :]
    gens = []
    for h in range(heads):
        sl = slice(h * HEAD_DIM, (h + 1) * HEAD_DIM)
        q = (q_ref[:, sl] * scale).astype(BF16)
        gens.append(_attn_head(q, [k_win[:, sl].astype(BF16), kc_ref[:, sl].astype(BF16)],
                               [v_win[:, sl].astype(BF16), vc_ref[:, sl].astype(BF16)], [bias_ref[h], None]))
    o_ref[...] = jnp.concatenate(_lock_step(gens), axis=-1)


def _toeplitz_kernel(rpb_ref, pick_ref, valid_ref, o_ref):
    o_ref[...] = jnp.where(valid_ref[...] > 0.5, _dot_hi(rpb_ref[...], pick_ref[...]), MASK_NEG)


def _window_bias(rpb, kr):
    delta = np.arange(kr)[:, None]
    i = np.arange(kr)[None, :]
    row_off = (WIN_ROWS - 1) - delta + i
    q = np.arange(GRID_W)[:, None]
    kc = np.arange(GRID_W)[None, :]
    c0 = np.clip(q - WIN_COLS // 2, 0, GRID_W - WIN_COLS)
    valid = (kc >= c0) & (kc < c0 + WIN_COLS)
    col_off = kc - q + (WIN_COLS - 1)
    pick = (col_off[None] == np.arange(2 * WIN_COLS - 1)[:, None, None]) & valid[None]
    H, n_ro, n_co = rpb.shape
    toep = pl.pallas_call(
        _toeplitz_kernel,
        out_shape=jax.ShapeDtypeStruct((H * n_ro, GRID_W * GRID_W), F32),
        name="window_bias",
    )(rpb.astype(F32).reshape(H * n_ro, n_co), jnp.asarray(pick.reshape(n_co, -1), F32),
      jnp.asarray(valid.reshape(1, -1), F32))
    toep = toep.reshape(H, n_ro, GRID_W, GRID_W)
    per_delta = []
    for dl in range(kr):
        lo = int(row_off[dl, 0])
        win = toep[:, lo:lo + kr]
        per_delta.append(win.transpose(0, 2, 1, 3).reshape(rpb.shape[0], GRID_W, kr * GRID_W))
    return jnp.stack(per_delta, axis=0)


def _attention_latent(z, k_ctx, v_ctx, bias, dec_seq, width):
    B, P, _ = k_ctx.shape
    rows = dec_seq // GRID_W
    kr = bias.shape[0]
    heads = width // HEAD_DIM

    def bias_index(b, r):
        return (r - jnp.clip(r - kr // 2, 0, rows - kr), 0, 0, 0)

    q_spec = pl.BlockSpec((GRID_W, width), lambda b, r: (b * rows + r, 0))
    return pl.pallas_call(
        functools.partial(_attn_lat_kernel, heads=heads, rows=rows, kr=kr),
        grid=(B, rows),
        in_specs=[
            q_spec,
            pl.BlockSpec((dec_seq, width), lambda b, r: (b, 1)),
            pl.BlockSpec((dec_seq, width), lambda b, r: (b, 2)),
            pl.BlockSpec((None, P, width), lambda b, r: (b, 0, 0)),
            pl.BlockSpec((None, P, width), lambda b, r: (b, 0, 0)),
            pl.BlockSpec((None, heads, GRID_W, kr * GRID_W), bias_index),
        ],
        out_specs=q_spec,
        out_shape=jax.ShapeDtypeStruct((B * dec_seq, width), F32),
        compiler_params=_params("arbitrary", "arbitrary"),
        name="attention_latent",
    )(z, z, z, k_ctx, v_ctx, bias)


def _sgu_kernel(u_ref, v_ref, gain_ref, w_ref, b_ref, o_ref, *, groups, chunks):
    for c in range(chunks):
        rows = slice(c * SG_CHUNK, (c + 1) * SG_CHUNK)
        for g in range(groups):
            sl = slice(g * SG_GROUP_DIM, (g + 1) * SG_GROUP_DIM)
            vn = _rms(v_ref[rows, sl]) * gain_ref[:, sl]
            mixed = _dot(w_ref[g].astype(BF16), vn.astype(BF16)) + b_ref[:, g:g + 1]
            o_ref[rows, sl] = u_ref[rows, sl] * mixed


def _spatial_gating(z, col_u, gain, w_s, b_t):
    T = z.shape[0]
    G = w_s.shape[0]
    W = G * SG_GROUP_DIM
    tm = _pick(T, (512, 256, 128))
    return pl.pallas_call(
        functools.partial(_sgu_kernel, groups=G, chunks=tm // SG_CHUNK),
        grid=(T // tm,),
        in_specs=[pl.BlockSpec((tm, W), lambda i: (i, col_u)),
                  pl.BlockSpec((tm, W), lambda i: (i, col_u + 1)),
                  _full(gain), _full(w_s), _full(b_t)],
        out_specs=pl.BlockSpec((tm, W), lambda i: (i, 0)),
        out_shape=jax.ShapeDtypeStruct((T, W), F32),
        compiler_params=_params("arbitrary"),
        name="spatial_gating",
    )(z, z, gain, w_s, b_t)


def _softplus(y):
    return jnp.maximum(y, 0.0) + jnp.log(1.0 + jnp.exp(-jnp.abs(y)))


def _rwkv_pre_kernel(r_ref, k_ref, v_ref, lora_ref, w0_ref, w2_ref, a0_ref, a2_ref, kkw_ref, ka_ref,
                     bd_ref, tri_ref,
                     at_f, rt_f, bh_f, kh_f, wc_f, at_b, rt_b, bh_b, kh_b, wc_b, v_o, *, pairs, chunks, lora_w, lora_a):
    C = RW_CHUNK
    r = r_ref[...]
    k = k_ref[...]
    bd = bd_ref[...]
    kk = k * kkw_ref[...]
    kk2 = kk * kk
    ssq = jnp.concatenate([_dot_wide_lhs(kk2[:, p * LANES:(p + 1) * LANES], bd) for p in range(pairs)], axis=-1)
    kk = kk * lax.rsqrt(ssq + KK_EPS)
    tw = jnp.tanh(lora_ref[:, :lora_w]).astype(BF16)
    xa = lora_ref[:, lora_w:lora_w + lora_a].astype(BF16)
    v_bf = v_ref[...].astype(BF16)
    for p in range(pairs):
        v_o[p] = v_bf[:, p * LANES:(p + 1) * LANES]
    outs = ((at_f, rt_f, bh_f, kh_f, wc_f), (at_b, rt_b, bh_b, kh_b, wc_b))
    for d in range(2):
        at_o, rt_o, bh_o, kh_o, wc_o = outs[d]
        w_log = -_softplus(-(w0_ref[d:d + 1, :] + _dot(tw, w2_ref[d].astype(BF16)))) - 0.5
        logw = -jnp.exp(w_log)
        a_rate = jax.nn.sigmoid(a0_ref[d:d + 1, :] + _dot(xa, a2_ref[d].astype(BF16)))
        k_d = k * (1.0 + (a_rate - 1.0) * ka_ref[...])
        b = kk * a_rate
        tri = tri_ref[d]
        tot_row = C - 1 if d == 0 else 0
        for c in range(chunks):
            rows = slice(c * C, (c + 1) * C)
            lw = logw[rows]
            cum = _dot_wide_rhs(tri, lw)
            tot = cum[tot_row:tot_row + 1, :]
            e_dn = jnp.exp(tot - cum)
            at = -kk[rows] * jnp.exp(cum - lw - tot)
            rt = r[rows] * jnp.exp(cum - tot)
            bh = b[rows] * e_dn
            kh = k_d[rows] * e_dn
            wc = jnp.exp(tot)
            for p in range(pairs):
                sl = slice(p * LANES, (p + 1) * LANES)
                at_o[p, rows, :] = at[:, sl].astype(BF16)
                rt_o[p, rows, :] = rt[:, sl].astype(BF16)
                bh_o[p, rows, :] = bh[:, sl].astype(BF16)
                kh_o[p, rows, :] = kh[:, sl].astype(BF16)
                wc_o[c, p] = wc[:, sl]


def _rwkv_cols(z_cols, width):
    bw = math.gcd(z_cols, width)
    assert bw % LANES == 0
    return bw, z_cols // bw, (z_cols + width) // bw, (z_cols + 2 * width) // bw


def _rwkv_prepare(z, r_off, lora_blk, lora_dims, w0, w2, a0, a2, kkw, ka, bd, tri):
    T = z.shape[0]
    RW = w0.shape[1]
    lora_w, lora_a, lora_g = lora_dims
    lora_n = lora_w + lora_a + lora_g
    bw, cr, ck, cv = _rwkv_cols(r_off, RW)
    pairs = bw // LANES
    C = RW_CHUNK
    tm = _pick(T, (256, 128, 64))
    tok = lambda c0: pl.BlockSpec((tm, bw), lambda i, h: (i, c0 + h))
    par2 = lambda a: pl.BlockSpec((a.shape[0], bw), lambda i, h: (0, h))
    par3 = lambda a: pl.BlockSpec((a.shape[0], a.shape[1], bw), lambda i, h: (0, 0, h))
    packed = pl.BlockSpec((pairs, tm, LANES), lambda i, h: (h, i, 0))
    wc_spec = pl.BlockSpec((tm // C, pairs, 1, LANES), lambda i, h: (i, h, 0, 0))
    packed_shape = jax.ShapeDtypeStruct((RW // LANES, T, LANES), BF16)
    wc_shape = jax.ShapeDtypeStruct((T // C, RW // LANES, 1, LANES), F32)
    return pl.pallas_call(
        functools.partial(_rwkv_pre_kernel, pairs=pairs, chunks=tm // C, lora_w=lora_w, lora_a=lora_a),
        grid=(T // tm, RW // bw),
        in_specs=[tok(cr), tok(ck), tok(cv),
                  pl.BlockSpec((tm, lora_n), lambda i, h: (i, lora_blk)),
                  par2(w0), par3(w2), par2(a0), par3(a2), par2(kkw), par2(ka), _full(bd), _full(tri)],
        out_specs=[packed, packed, packed, packed, wc_spec, packed, packed, packed, packed, wc_spec, packed],
        out_shape=[packed_shape] * 4 + [wc_shape] + [packed_shape] * 4 + [wc_shape] + [packed_shape],
        compiler_params=_params("arbitrary", "arbitrary"),
        name="rwkv_prepare",
    )(z, z, z, z, w0, w2, a0, a2, kkw, ka, bd, tri)


def _rwkv_pair_chunk(at, rt, bh, kh, v, hs, masks):
    C = RW_CHUNK
    strict, incl, eye, same_head, lane_lo = masks
    zero = jnp.zeros_like(at)
    stack = lambda x: jnp.concatenate([jnp.where(lane_lo, x, zero), jnp.where(lane_lo, zero, x)], axis=0)
    twice = lambda x: jnp.concatenate([x, x], axis=0)
    atm, rtm, vm = stack(at), stack(rt), stack(v)
    bk = jnp.concatenate([twice(bh), twice(kh)], axis=0)
    a = _dot_nt(jnp.concatenate([atm, rtm], axis=0), bk)
    yield
    l = jnp.where(strict, a[:2 * C, :2 * C], 0.0)
    a_ak = jnp.where(strict, a[:2 * C, 2 * C:], 0.0).astype(BF16)
    a_rb = jnp.where(incl, a[2 * C:, :2 * C], 0.0).astype(BF16)
    a_rk = jnp.where(incl, a[2 * C:, 2 * C:], 0.0).astype(BF16)
    hs_b = hs.astype(BF16)
    x2 = _dot(jnp.concatenate([atm, a_ak], axis=1), jnp.concatenate([hs_b, vm], axis=0)).astype(BF16)
    t = eye + l
    pw = l.astype(BF16)
    pw = _dot(pw, pw).astype(BF16)
    yield
    for _ in range(int(math.log2(C)) - 2):
        both = _dot(jnp.concatenate([t.astype(BF16), pw], axis=0), pw)
        yield
        t = t + both[:2 * C]
        pw = both[2 * C:].astype(BF16)
    t = t + _dot(t.astype(BF16), pw)
    yield
    u = _dot(t.astype(BF16), x2).astype(BF16)
    yield
    o = _dot(jnp.concatenate([rtm, a_rb, a_rk], axis=1), jnp.concatenate([hs_b, u, vm], axis=0))
    h_new = hs + jnp.where(same_head, _dot_tn(bk, jnp.concatenate([u, vm], axis=0)), 0.0)
    return o[:C] + o[C:], h_new


def _rwkv_core_kernel(at_f, rt_f, bh_f, kh_f, v_f, wc_f, at_b, rt_b, bh_b, kh_b, v_b, wc_b, h0_ref,
                      of_ref, ob_ref, hs_ref, *, pairs, group):
    C = RW_CHUNK

    @pl.when(pl.program_id(1) == 0)
    def _():
        hs_ref[...] = h0_ref[...]

    row = _iota((2 * C, 2 * C), 0)
    col = _iota((2 * C, 2 * C), 1)
    same = (row // C) == (col // C)
    rt_, ct_ = row % C, col % C
    eye = jnp.where(row == col, 1.0, 0.0).astype(F32)
    lrow = _iota((LANES, LANES), 0)
    lcol = _iota((LANES, LANES), 1)
    same_head = (lrow // HEAD_DIM) == (lcol // HEAD_DIM)
    lane_lo = _iota((C, LANES), 1) < HEAD_DIM
    masks = ((same & (ct_ < rt_), same & (ct_ <= rt_), eye, same_head, lane_lo),
             (same & (ct_ > rt_), same & (ct_ >= rt_), eye, same_head, lane_lo))
    dirs = ((at_f, rt_f, bh_f, kh_f, v_f, wc_f, of_ref), (at_b, rt_b, bh_b, kh_b, v_b, wc_b, ob_ref))

    for g0 in range(0, pairs, group):
        jobs = [(d, p) for p in range(g0, g0 + group) for d in range(2)]
        gens = []
        for d, p in jobs:
            at_r, rt_r, bh_r, kh_r, v_r, wc_r, _ = dirs[d]
            wcol = jnp.sum(jnp.where(lrow == lcol, jnp.broadcast_to(wc_r[p], (LANES, LANES)), 0.0),
                           axis=1, keepdims=True)
            gens.append(_rwkv_pair_chunk(at_r[p], rt_r[p], bh_r[p], kh_r[p], v_r[p], hs_ref[d, p] * wcol, masks[d]))
        for (d, p), (o, h_new) in zip(jobs, _lock_step(gens)):
            hs_ref[d, p] = h_new
            dirs[d][-1][p] = o


def _pair_states(h):
    n, _, H, _, _ = h.shape
    hp = h.reshape(n, 2, H // 2, 2, HEAD_DIM, HEAD_DIM)
    z = jnp.zeros_like(hp[:, :, :, 0])
    top = jnp.concatenate([hp[:, :, :, 0], z], axis=-1)
    bot = jnp.concatenate([z, hp[:, :, :, 1]], axis=-1)
    return jnp.concatenate([top, bot], axis=-2)


def _unpair_states(hp):
    n, _, P, _, _ = hp.shape
    h0 = hp[:, :, :, :HEAD_DIM, :HEAD_DIM]
    h1 = hp[:, :, :, HEAD_DIM:, HEAD_DIM:]
    return jnp.stack([h0, h1], axis=3).reshape(n, 2, 2 * P, HEAD_DIM, HEAD_DIM)


def _rwkv_core(pre, h0, n_seq, seq_len):
    at_f, rt_f, bh_f, kh_f, wc_f, at_b, rt_b, bh_b, kh_b, wc_b, v = pre
    pairs, T, _ = at_f.shape
    C = RW_CHUNK
    nch = seq_len // C
    fwd = lambda s, c: s * nch + c
    bwd = lambda s, c: s * nch + (nch - 1 - c)
    pk = lambda f: pl.BlockSpec((pairs, C, LANES), lambda s, c: (0, f(s, c), 0))
    wc = lambda f: pl.BlockSpec((None, pairs, 1, LANES), lambda s, c: (f(s, c), 0, 0, 0))
    st = pl.BlockSpec((None, 2, pairs, LANES, LANES), lambda s, c: (s, 0, 0, 0, 0))
    o_shape = jax.ShapeDtypeStruct((pairs, T, LANES), F32)
    return pl.pallas_call(
        functools.partial(_rwkv_core_kernel, pairs=pairs, group=_pick(pairs, (RW_PAIR_GROUP, 4, 2, 1))),
        grid=(n_seq, nch),
        in_specs=[pk(fwd), pk(fwd), pk(fwd), pk(fwd), pk(fwd), wc(fwd),
                  pk(bwd), pk(bwd), pk(bwd), pk(bwd), pk(bwd), wc(bwd), st],
        out_specs=[pk(fwd), pk(bwd), st],
        out_shape=[o_shape, o_shape, jax.ShapeDtypeStruct(h0.shape, F32)],
        compiler_params=_params("arbitrary", "arbitrary"),
        name="rwkv_scan",
    )(at_f, rt_f, bh_f, kh_f, v, wc_f, at_b, rt_b, bh_b, kh_b, v, wc_b, h0)


def _rwkv_post_kernel(of_ref, ob_ref, r_ref, k_ref, v_ref, lora_ref, g2_ref, lnw_ref, lnb_ref, rk_ref, bd_ref,
                      o_ref, *, pairs, lora_g):
    bd = bd_ref[...]
    n_lora = lora_ref.shape[1]
    sg = jax.nn.sigmoid(lora_ref[:, n_lora - lora_g:]).astype(BF16)
    inv = 1.0 / HEAD_DIM
    for p in range(pairs):
        sl = slice(p * LANES, (p + 1) * LANES)
        o = of_ref[p] + ob_ref[p]
        mu = _dot_wide_lhs(o, bd) * inv
        dlt = o - mu
        var = _dot_wide_lhs(dlt * dlt, bd) * inv
        on = dlt * lax.rsqrt(var + GN_EPS) * lnw_ref[:, sl] + lnb_ref[:, sl]
        bonus = _dot_wide_lhs(r_ref[:, sl] * k_ref[:, sl] * rk_ref[:, sl], bd) * v_ref[:, sl]
        gate = _dot(sg, g2_ref[:, sl].astype(BF16))
        o_ref[:, sl] = (on + bonus) * gate


def _rwkv_finish(o_f, o_b, z, r_off, lora_blk, lora_dims, g2, lnw, lnb, rk, bd):
    T = z.shape[0]
    RW = g2.shape[1]
    lora_n = sum(lora_dims)
    bw, cr, ck, cv = _rwkv_cols(r_off, RW)
    pairs = bw // LANES
    tm = _pick(T, (256, 128, 64))
    tok = lambda c0: pl.BlockSpec((tm, bw), lambda i, h: (i, c0 + h))
    par2 = lambda a: pl.BlockSpec((a.shape[0], bw), lambda i, h: (0, h))
    packed = pl.BlockSpec((pairs, tm, LANES), lambda i, h: (h, i, 0))
    return pl.pallas_call(
        functools.partial(_rwkv_post_kernel, pairs=pairs, lora_g=lora_dims[2]),
        grid=(T // tm, RW // bw),
        in_specs=[packed, packed, tok(cr), tok(ck), tok(cv),
                  pl.BlockSpec((tm, lora_n), lambda i, h: (i, lora_blk)),
                  par2(g2), par2(lnw), par2(lnb), par2(rk), _full(bd)],
        out_specs=pl.BlockSpec((tm, bw), lambda i, h: (i, h)),
        out_shape=jax.ShapeDtypeStruct((T, RW), F32),
        compiler_params=_params("arbitrary", "arbitrary"),
        name="rwkv_finish",
    )(o_f, o_b, z, z, z, z, g2, lnw, lnb, rk, bd)


def _outproj_kernel(oa_ref, ob_ref, oc_ref, x_ref, w_ref, g1_ref, sh2_ref, sc2_ref, gpost_ref, gpre_ref, wr_ref,
                    xo_ref, h2_ref, aff_ref):
    cat = jnp.concatenate([oa_ref[...], ob_ref[...], oc_ref[...]], axis=-1).astype(BF16)
    mixed = _dot(cat, w_ref[...])
    x = x_ref[...] + g1_ref[...] * (_rms(mixed) * gpost_ref[...])
    xo_ref[...] = x
    h2 = _rms(x) * gpre_ref[...] * (1.0 + sc2_ref[...]) + sh2_ref[...]
    h2_ref[...] = h2.astype(BF16)
    h_hi, h_lo = _split2(h2)
    w_hi, w_lo = _split2(wr_ref[...])
    logits = _dot(h_hi, w_hi) + _dot(h_lo, w_hi) + _dot(h_hi, w_lo)
    e = jnp.exp(logits - jnp.max(logits, axis=-1, keepdims=True))
    aff_ref[...] = e / jnp.sum(e, axis=-1, keepdims=True)


def _out_projection(o_a, o_b, o_c, x, w_out, layer, mod3, mod_row, g_post, g_pre, w_router, tm):
    T, D = x.shape
    E = w_router.shape[1]
    row = lambda a: pl.BlockSpec((tm, a.shape[1]), lambda i: (i, 0))
    mod = lambda col: pl.BlockSpec((None, 1, D), lambda i: (mod_row(i), 0, col))
    return pl.pallas_call(
        _outproj_kernel,
        grid=(T // tm,),
        in_specs=[row(o_a), row(o_b), row(o_c), row(x), pl.BlockSpec((None, D, D), lambda i: (layer, 0, 0)),
                  mod(2), mod(3), mod(4), _full(g_post), _full(g_pre), _full(w_router)],
        out_specs=[pl.BlockSpec((tm, D), lambda i: (i, 0)),
                   pl.BlockSpec((tm, D), lambda i: (i, 0)),
                   pl.BlockSpec((tm, E), lambda i: (i, 0))],
        out_shape=[jax.ShapeDtypeStruct((T, D), F32), jax.ShapeDtypeStruct((T, D), BF16),
                   jax.ShapeDtypeStruct((T, E), F32)],
        compiler_params=_params("arbitrary"),
        name="out_projection",
    )(o_a, o_b, o_c, x, w_out, mod3, mod3, mod3, g_post, g_pre, w_router)


def _select_kernel(a_ref, ut_ref, pos_ref, *, cap):
    a = a_ref[...]
    R, T = a.shape
    lo = jnp.zeros((R, 1), jnp.int32)
    for bit in range(30, -1, -1):
        cand = lo | (1 << bit)
        cnt = jnp.sum(jnp.where(a >= lax.bitcast_convert_type(cand, F32), 1.0, 0.0), axis=1, keepdims=True)
        lo = jnp.where(cnt >= cap, cand, lo)
    thr = lax.bitcast_convert_type(lo, F32)
    gt = a > thr
    eq = a == thr
    need = cap - jnp.sum(jnp.where(gt, 1.0, 0.0), axis=1, keepdims=True)
    ut = ut_ref[...]
    nblk = T // LANES
    tie_carry = jnp.zeros((R, 1), F32)
    pos_carry = jnp.zeros((R, 1), F32)
    for n in range(nblk):
        sl = slice(n * LANES, (n + 1) * LANES)
        eq_b = jnp.where(eq[:, sl], 1.0, 0.0)
        tie_rank = _dot(eq_b.astype(BF16), ut) + tie_carry
        tie_carry = tie_carry + jnp.sum(eq_b, axis=1, keepdims=True)
        sel = jnp.where(gt[:, sl], 1.0, jnp.where(tie_rank < need, eq_b, 0.0))
        pos = _dot(sel.astype(BF16), ut) + pos_carry
        pos_carry = pos_carry + jnp.sum(sel, axis=1, keepdims=True)
        pos_ref[:, sl] = jnp.where(sel > 0.5, pos, -1.0).astype(jnp.int32)


def _select(aff_rows, cap, ut):
    R, T = aff_rows.shape
    tr = _pick(R, (128, 64, 32, 16, 8))
    return pl.pallas_call(
        functools.partial(_select_kernel, cap=cap),
        grid=(R // tr,),
        in_specs=[pl.BlockSpec((tr, T), lambda i: (i, 0)), pl.BlockSpec((LANES, LANES), lambda i: (0, 0))],
        out_specs=pl.BlockSpec((tr, T), lambda i: (i, 0)),
        out_shape=jax.ShapeDtypeStruct((R, T), jnp.int32),
        compiler_params=_params("arbitrary"),
        name="expert_select",
    )(aff_rows, ut)


def _route(aff, n_sets, set_len, cap, ut):
    E = aff.shape[1]
    rows = aff.reshape(n_sets, set_len, E).transpose(0, 2, 1).reshape(n_sets * E, set_len)
    pos = _select(rows, cap, ut).reshape(n_sets, E, set_len)
    return pos, pos.transpose(0, 2, 1).reshape(n_sets * set_len, E)


def _one_hot_rows(pos_row, cap):
    T = pos_row.shape[1]
    return jnp.where(_iota((cap, T), 0) == pos_row, 1.0, 0.0).astype(BF16)


def _gather_ctx_kernel(h_ref, pos_ref, xs_ref, *, experts, cap):
    h = h_ref[...]
    for e in range(experts):
        xs_ref[e] = _dot(_one_hot_rows(pos_ref[e:e + 1, :], cap), h).astype(BF16)


def _gather_lat_kernel(h_ref, pos_ref, xs_ref, *, cap):
    xs_ref[...] = _dot(_one_hot_rows(pos_ref[...], cap), h_ref[...]).astype(BF16)


def _expert_up_kernel(xc_ref, xl_ref, wg_ref, wu_ref, hid_ref):
    wg = wg_ref[...].astype(BF16)
    wu = wu_ref[...].astype(BF16)
    rc = xc_ref.shape[0]
    for ref, rows in ((xc_ref, slice(0, rc)), (xl_ref, slice(rc, hid_ref.shape[0]))):
        x = ref[...]
        g = _dot(x, wg)
        hid_ref[rows, :] = (g * jax.nn.sigmoid(g) * _dot(x, wu)).astype(BF16)


def _expert_down_kernel(hid_ref, wd_ref, y_ref):
    y_ref[...] = _dot(hid_ref[...], wd_ref[...].astype(BF16)).astype(BF16)


def _slot_one_hot(pos_col, cap):
    tm = pos_col.shape[0]
    return jnp.where(_iota((tm, cap), 1).astype(F32) == pos_col, 1.0, 0.0).astype(BF16)


def _col_chunks(width):
    cw = _pick(width, (2 * LANES, LANES))
    return [slice(n * cw, (n + 1) * cw) for n in range(width // cw)]


def _ffn_residual(x, ffn, g2, g_post):
    return x + g2 * (_rms(ffn) * g_post)


def _combine_ctx_kernel(x_ref, y_ref, pos_ref, aff_ref, g2_ref, gpost_ref, o_ref, acc_scr, *, experts, cap):
    pos = pos_ref[...].astype(F32)
    aff = aff_ref[...]
    one_hots = [_slot_one_hot(pos[:, e:e + 1], cap) for e in range(experts)]
    for cs in _col_chunks(acc_scr.shape[1]):
        acc_scr[:, cs] = sum(aff[:, e:e + 1] * _dot(one_hots[e], y_ref[e, :, cs]) for e in range(experts))
    o_ref[...] = _ffn_residual(x_ref[...], acc_scr[...], g2_ref[...], gpost_ref[...])


def _combine_lat_kernel(x_ref, y_ref, pos_ref, aff_ref, g2_ref, gpost_ref, o_ref, acc_scr, *, cap):
    e = pl.program_id(2)

    @pl.when(e == 0)
    def _():
        acc_scr[...] = jnp.zeros(acc_scr.shape, F32)

    mine = _iota(pos_ref.shape, 1) == e
    pos_col = jnp.sum(jnp.where(mine, pos_ref[...].astype(F32), 0.0), axis=1, keepdims=True)
    gate_col = jnp.sum(jnp.where(mine, aff_ref[...], 0.0), axis=1, keepdims=True)
    one_hot = _slot_one_hot(pos_col, cap)
    for cs in _col_chunks(acc_scr.shape[1]):
        acc_scr[:, cs] += gate_col * _dot(one_hot, y_ref[:, cs])

    @pl.when(e == pl.num_programs(2) - 1)
    def _():
        o_ref[...] = _ffn_residual(x_ref[...], acc_scr[...], g2_ref[...], gpost_ref[...])


def _expert_ffn(x_c, h2_c, aff_c, x_l, h2_l, aff_l, mod3, g_post, w_gate, w_up, w_down, layer, ut, dims):
    batch, seq, dec_batch, dec_seq = dims
    D = x_c.shape[1]
    _, E, _, F = w_gate.shape
    cap_c = CAPACITY_FACTOR * seq // E
    cap_l = CAPACITY_FACTOR * dec_seq // E
    rows_c = batch * cap_c
    rows_l = dec_batch * cap_l
    rows = rows_c + rows_l
    assert rows_c % cap_l == 0
    lat_row_blk = rows_c // cap_l

    pos_c, post_c = _route(aff_c, batch, seq, cap_c, ut)
    pos_l, post_l = _route(aff_l, dec_batch, dec_seq, cap_l, ut)

    xs_c = pl.pallas_call(
        functools.partial(_gather_ctx_kernel, experts=E, cap=cap_c),
        grid=(batch,),
        in_specs=[pl.BlockSpec((seq, D), lambda b: (b, 0)), pl.BlockSpec((None, E, seq), lambda b: (b, 0, 0))],
        out_specs=pl.BlockSpec((E, cap_c, D), lambda b: (0, b, 0)),
        out_shape=jax.ShapeDtypeStruct((E, rows_c, D), BF16),
        compiler_params=_params("arbitrary"),
        name="expert_gather_context",
    )(h2_c, pos_c)
    xs_l = pl.pallas_call(
        functools.partial(_gather_lat_kernel, cap=cap_l),
        grid=(dec_batch, E),
        in_specs=[pl.BlockSpec((dec_seq, D), lambda b, e: (b, 0)),
                  pl.BlockSpec((None, None, 1, dec_seq), lambda b, e: (b, e, 0, 0))],
        out_specs=pl.BlockSpec((None, cap_l, D), lambda b, e: (e, b, 0)),
        out_shape=jax.ShapeDtypeStruct((E, rows_l, D), BF16),
        compiler_params=_params("arbitrary", "arbitrary"),
        name="expert_gather_latent",
    )(h2_l, pos_l.reshape(dec_batch, E, 1, dec_seq))

    tf = _pick(F, (512, 256, 128))
    hid = pl.pallas_call(
        _expert_up_kernel,
        grid=(E, F // tf),
        in_specs=[pl.BlockSpec((None, rows_c, D), lambda e, n: (e, 0, 0)),
                  pl.BlockSpec((None, rows_l, D), lambda e, n: (e, 0, 0)),
                  pl.BlockSpec((None, None, D, tf), lambda e, n: (layer, e, 0, n)),
                  pl.BlockSpec((None, None, D, tf), lambda e, n: (layer, e, 0, n))],
        out_specs=pl.BlockSpec((None, rows, tf), lambda e, n: (e, 0, n)),
        out_shape=jax.ShapeDtypeStruct((E, rows, F), BF16),
        compiler_params=_params("arbitrary", "arbitrary"),
        name="expert_up",
    )(xs_c, xs_l, w_gate, w_up)
    td = _pick(D, (512, 256, 128))
    y = pl.pallas_call(
        _expert_down_kernel,
        grid=(E, D // td),
        in_specs=[pl.BlockSpec((None, rows, F), lambda e, n: (e, 0, 0)),
                  pl.BlockSpec((None, None, F, td), lambda e, n: (layer, e, 0, n))],
        out_specs=pl.BlockSpec((None, rows, td), lambda e, n: (e, 0, n)),
        out_shape=jax.ShapeDtypeStruct((E, rows, D), BF16),
        compiler_params=_params("arbitrary", "arbitrary"),
        name="expert_down",
    )(hid, w_down)

    out_c = pl.pallas_call(
        functools.partial(_combine_ctx_kernel, experts=E, cap=cap_c),
        grid=(batch,),
        in_specs=[pl.BlockSpec((seq, D), lambda b: (b, 0)),
                  pl.BlockSpec((E, cap_c, D), lambda b: (0, b, 0)),
                  pl.BlockSpec((seq, E), lambda b: (b, 0)),
                  pl.BlockSpec((seq, E), lambda b: (b, 0)),
                  pl.BlockSpec((None, 1, D), lambda b: (0, 0, 5)),
                  _full(g_post)],
        out_specs=pl.BlockSpec((seq, D), lambda b: (b, 0)),
        out_shape=jax.ShapeDtypeStruct(x_c.shape, F32),
        scratch_shapes=[pltpu.VMEM((seq, D), F32)],
        compiler_params=_params("arbitrary"),
        name="expert_combine_context",
    )(x_c, y, post_c, aff_c, mod3, g_post)
    tm = _pick(dec_seq, (512, 256, 128))
    nt = dec_seq // tm
    tok = lambda b, i, e: (b * nt + i, 0)
    out_l = pl.pallas_call(
        functools.partial(_combine_lat_kernel, cap=cap_l),
        grid=(dec_batch, nt, E),
        in_specs=[pl.BlockSpec((tm, D), tok),
                  pl.BlockSpec((None, cap_l, D), lambda b, i, e: (e, lat_row_blk + b, 0)),
                  pl.BlockSpec((tm, E), tok),
                  pl.BlockSpec((tm, E), tok),
                  pl.BlockSpec((None, 1, D), lambda b, i, e: (1 + b, 0, 5)),
                  _full(g_post)],
        out_specs=pl.BlockSpec((tm, D), tok),
        out_shape=jax.ShapeDtypeStruct(x_l.shape, F32),
        scratch_shapes=[pltpu.VMEM((tm, D), F32)],
        compiler_params=_params("arbitrary", "arbitrary", "arbitrary"),
        name="expert_combine_latent",
    )(x_l, y, post_l, aff_l, mod3, g_post)
    return out_c, out_l


def kernel(x_prompt, x_sample, cache_k, cache_v, state_wkv, c, c_ctx, w_ada, b_ada, g_pre_mix, g_post_mix, g_pre_ffn, g_post_ffn, w_in, na_rpb, sg_gain, sg_w, sg_b, rw_w0, rw_w2, rw_a0, rw_a2, rw_g2, rw_kk, rw_ka, rw_rk, rw_ln_w, rw_ln_b, w_out, w_router, w_gate, w_up, w_down):
    batch, seq, D = x_prompt.shape
    dec_batch, dec_seq, _ = x_sample.shape
    depth = w_ada.shape[0]
    past = cache_k.shape[2]
    na_heads = cache_k.shape[3]
    naw = na_heads * HEAD_DIM
    sgw = sg_gain.shape[1]
    rw_heads = state_wkv.shape[3]
    rww = rw_heads * HEAD_DIM
    lora_dims = (rw_w2.shape[2], rw_a2.shape[2], rw_g2.shape[1])
    n_ctx = batch * seq
    n_lat = dec_batch * dec_seq
    assert 1 + dec_batch <= MOD_ROWS and rww % LANES == 0
    assert seq % RW_CHUNK == 0 and dec_seq % RW_CHUNK == 0 and dec_seq % GRID_W == 0
    sg_off = 3 * naw
    rw_off = sg_off + 2 * sgw
    lora_off = rw_off + 3 * rww
    assert sg_off % sgw == 0 and lora_off % sum(lora_dims) == 0
    col_u = sg_off // sgw
    lora_blk = lora_off // sum(lora_dims)

    pair_ones = jnp.asarray(np.kron(np.eye(LANES // HEAD_DIM), np.ones((HEAD_DIM, HEAD_DIM))), BF16)
    tril = np.tril(np.ones((RW_CHUNK, RW_CHUNK)))
    tri = jnp.asarray(np.stack([tril, tril.T]), BF16)
    ut = jnp.asarray(np.triu(np.ones((LANES, LANES)), 1), BF16)
    kr = min(WIN_ROWS, dec_seq // GRID_W)

    cond = jnp.zeros((MOD_ROWS, D), F32).at[0].set(c_ctx).at[1:1 + dec_batch].set(c)
    mods = _modulation(cond, w_ada, b_ada.reshape(depth, 1, N_MOD * D))
    w_in_b = _to_bf16(w_in)
    w_out_b = _to_bf16(w_out)

    tm_c = _pick(n_ctx, (512, 256, 128))
    tm_l = _pick(dec_seq, (512, 256, 128))
    ctx_row = lambda i: 0
    lat_row = lambda i: 1 + (i * tm_l) // dec_seq

    x_c = x_prompt.reshape(n_ctx, D)
    x_l = x_sample.reshape(n_lat, D)
    h0_ctx = jnp.zeros((batch, 2, rww // LANES, LANES, LANES), F32)
    ks, vs, ss = [], [], []
    for l in range(depth):
        mod3 = mods[l].reshape(MOD_ROWS, 1, N_MOD * D)
        g_pre = g_pre_mix[l].reshape(1, D)
        rw_args = (rw_w0[l], rw_w2[l], rw_a0[l], rw_a2[l], rw_kk[l].reshape(1, rww), rw_ka[l].reshape(1, rww),
                   pair_ones, tri)
        fin_args = (rw_g2[l], rw_ln_w[l].reshape(1, rww), rw_ln_b[l].reshape(1, rww), rw_rk[l].reshape(1, rww),
                    pair_ones)
        sg_args = (sg_gain[l].reshape(1, sgw), sg_w[l], sg_b[l].T)

        def mixers(x, tm, mod_row, attend, h0, n_seq, seq_len):
            z = _in_projection(x, mod3, mod_row, g_pre, w_in_b, l, tm)
            o_a = attend(z)
            o_b = _spatial_gating(z, col_u, *sg_args)
            pre = _rwkv_prepare(z, rw_off, lora_blk, lora_dims, *rw_args)
            o_f, o_bk, s_fin = _rwkv_core(pre, h0, n_seq, seq_len)
            o_c = _rwkv_finish(o_f, o_bk, z, rw_off, lora_blk, lora_dims, *fin_args)
            x, h2, aff = _out_projection(o_a, o_b, o_c, x, w_out_b, l, mod3, mod_row,
                                         g_post_mix[l].reshape(1, D), g_pre_ffn[l].reshape(1, D), w_router[l], tm)
            return z, x, h2, aff, s_fin

        z_c, x_c, h2_c, aff_c, s_ctx = mixers(
            x_c, tm_c, ctx_row, lambda z: _attention_context(z, seq, naw), h0_ctx, batch, seq)
        ks.append(z_c[:, naw:2 * naw].reshape(batch, seq, na_heads, HEAD_DIM))
        vs.append(z_c[:, 2 * naw:3 * naw].reshape(batch, seq, na_heads, HEAD_DIM))
        ss.append(jnp.swapaxes(_unpair_states(s_ctx), -1, -2))

        bias = _window_bias(na_rpb[l], kr)
        k_past = cache_k[:, l].reshape(dec_batch, past, naw)
        v_past = cache_v[:, l].reshape(dec_batch, past, naw)
        h0_lat = _pair_states(jnp.swapaxes(state_wkv[:, l], -1, -2))
        _, x_l, h2_l, aff_l, _ = mixers(
            x_l, tm_l, lat_row, lambda z: _attention_latent(z, k_past, v_past, bias, dec_seq, naw),
            h0_lat, dec_batch, dec_seq)

        x_c, x_l = _expert_ffn(x_c, h2_c, aff_c, x_l, h2_l, aff_l, mod3, g_post_ffn[l].reshape(1, D),
                               w_gate, w_up, w_down, l, ut, (batch, seq, dec_batch, dec_seq))

    return (x_c.reshape(batch, seq, D), x_l.reshape(dec_batch, dec_seq, D),
            jnp.stack(ks, axis=1), jnp.stack(vs, axis=1), jnp.stack(ss, axis=1))
```

```python
import functools
import math

import numpy as np
import jax
import jax.numpy as jnp
from jax import lax
from jax.experimental import pallas as pl
from jax.experimental.pallas import tpu as pltpu

F32 = jnp.float32
BF16 = jnp.bfloat16
HIGHEST = lax.Precision.HIGHEST

HEAD_DIM = 64
LANES = 128
GRID_W = 64
WIN_ROWS = 8
WIN_COLS = 16
SG_CHUNK = 128
SG_GROUP_DIM = 128
N_MOD = 6
CAPACITY_FACTOR = 2
NORM_EPS = 1e-6
GN_EPS = 64e-5
KK_EPS = 1e-12
RW_CHUNK = 64
RW_PAIR_GROUP = 8
MASK_NEG = -1e30
MOD_ROWS = 8
VMEM_LIMIT_BYTES = 56 * 1024 * 1024
MAX_WEIGHT_COLS = 3072
GATHER_TOKENS = 256
SLOT_TILE = 128
COMBINE_GROUP = 4


def _pick(n, prefs):
    for p in prefs:
        if n % p == 0:
            return p
    raise ValueError(f"no tile in {prefs} divides {n}")


def _col_tile(n, cap):
    tiles = [t for t in range(LANES, min(n, cap) + 1, LANES) if n % t == 0]
    return tiles[-1]


def _params(*sem):
    return pltpu.CompilerParams(dimension_semantics=sem, vmem_limit_bytes=VMEM_LIMIT_BYTES)


def _rms(x):
    return x * lax.rsqrt(jnp.mean(x * x, axis=-1, keepdims=True) + NORM_EPS)


def _dot(a, b):
    return jnp.dot(a, b, preferred_element_type=F32)


def _dot_nt(a, b):
    return lax.dot_general(a, b, (((1,), (1,)), ((), ())), preferred_element_type=F32)


def _dot_tn(a, b):
    return lax.dot_general(a, b, (((0,), (0,)), ((), ())), preferred_element_type=F32)


def _dot_hi(a, b):
    return jnp.dot(a, b, preferred_element_type=F32, precision=HIGHEST)


def _split2(a):
    hi = a.astype(BF16)
    return hi, (a - hi.astype(F32)).astype(BF16)


def _dot_wide_lhs(a, b):
    hi, lo = _split2(a)
    return _dot(hi, b) + _dot(lo, b)


def _dot_wide_rhs(a, b):
    hi, lo = _split2(b)
    return _dot(a, hi) + _dot(a, lo)


def _lock_step(gens):
    results = [None] * len(gens)
    live = list(enumerate(gens))
    while live:
        still = []
        for i, g in live:
            try:
                next(g)
                still.append((i, g))
            except StopIteration as done:
                results[i] = done.value
        live = still
    return results


def _iota(shape, axis):
    return lax.broadcasted_iota(jnp.int32, shape, axis)


def _full(a):
    return pl.BlockSpec(a.shape, lambda *_: (0,) * a.ndim)


def _cast_kernel(w_ref, o_ref):
    o_ref[...] = w_ref[...].astype(BF16)


def _to_bf16(w):
    L, R, C = w.shape
    tr = _pick(R, (256, 128, 64, 32, 16))
    spec = pl.BlockSpec((None, tr, C), lambda l, i: (l, i, 0))
    return pl.pallas_call(
        _cast_kernel, grid=(L, R // tr), in_specs=[spec], out_specs=spec,
        out_shape=jax.ShapeDtypeStruct(w.shape, BF16),
        compiler_params=_params("arbitrary", "arbitrary"), name="cast_weights",
    )(w)


def _mod_kernel(c_ref, w_ref, b_ref, o_ref):
    c = c_ref[...]
    s = (c * jax.nn.sigmoid(c)).astype(BF16)
    o_ref[...] = _dot(s, w_ref[...].astype(BF16)) + b_ref[...]


def _modulation(cond, w_ada, b_ada):
    L, D, N = w_ada.shape
    tn = _pick(N, (1024, 512, 256, 128))
    return pl.pallas_call(
        _mod_kernel,
        grid=(L, N // tn),
        in_specs=[
            pl.BlockSpec((MOD_ROWS, D), lambda l, j: (0, 0)),
            pl.BlockSpec((None, D, tn), lambda l, j: (l, 0, j)),
            pl.BlockSpec((None, 1, tn), lambda l, j: (l, 0, j)),
        ],
        out_specs=pl.BlockSpec((None, MOD_ROWS, tn), lambda l, j: (l, 0, j)),
        out_shape=jax.ShapeDtypeStruct((L, MOD_ROWS, N), F32),
        compiler_params=_params("arbitrary", "arbitrary"),
        name="modulation",
    )(cond, w_ada, b_ada)


def _inproj_kernel(x_ref, sh_ref, sc_ref, g_ref, w_ref, z_ref):
    h = _rms(x_ref[...]) * g_ref[...] * (1.0 + sc_ref[...]) + sh_ref[...]
    z_ref[...] = _dot(h.astype(BF16), w_ref[...])


def _in_projection(x, mod3, mod_row, g, w, layer, tm):
    T, D = x.shape
    N = w.shape[2]
    tn = _col_tile(N, MAX_WEIGHT_COLS)
    return pl.pallas_call(
        _inproj_kernel,
        grid=(N // tn, T // tm),
        in_specs=[
            pl.BlockSpec((tm, D), lambda j, i: (i, 0)),
            pl.BlockSpec((None, 1, D), lambda j, i: (mod_row(i), 0, 0)),
            pl.BlockSpec((None, 1, D), lambda j, i: (mod_row(i), 0, 1)),
            pl.BlockSpec((1, D), lambda j, i: (0, 0)),
            pl.BlockSpec((None, D, tn), lambda j, i: (layer, 0, j)),
        ],
        out_specs=pl.BlockSpec((tm, tn), lambda j, i: (i, j)),
        out_shape=jax.ShapeDtypeStruct((T, N), F32),
        compiler_params=_params("arbitrary", "arbitrary"),
        name="in_projection",
    )(x, mod3, mod3, g, w)


def _attn_head(q, keys, values, biases):
    s = [_dot_nt(q, k) if b is None else _dot_nt(q, k) + b for k, b in zip(keys, biases)]
    yield
    m = functools.reduce(jnp.maximum, [jnp.max(x, axis=-1, keepdims=True) for x in s])
    p = [jnp.exp(x - m) for x in s]
    l = sum(jnp.sum(x, axis=-1, keepdims=True) for x in p)
    o = sum(_dot(x.astype(BF16), v) for x, v in zip(p, values))
    yield
    return o / l


def _attn_ctx_kernel(q_ref, k_ref, v_ref, o_ref, *, heads):
    scale = HEAD_DIM ** -0.5
    gens = []
    for h in range(heads):
        sl = slice(h * HEAD_DIM, (h + 1) * HEAD_DIM)
        q = (q_ref[:, sl] * scale).astype(BF16)
        gens.append(_attn_head(q, [k_ref[:, sl].astype(BF16)], [v_ref[:, sl].astype(BF16)], [None]))
    o_ref[...] = jnp.concatenate(_lock_step(gens), axis=-1)


def _attention_context(z, seq, width):
    T = z.shape[0]
    col = lambda j: pl.BlockSpec((seq, width), lambda b: (b, j))
    return pl.pallas_call(
        functools.partial(_attn_ctx_kernel, heads=width // HEAD_DIM),
        grid=(T // seq,),
        in_specs=[col(0), col(1), col(2)],
        out_specs=col(0),
        out_shape=jax.ShapeDtypeStruct((T, width), F32),
        compiler_params=_params("arbitrary"),
        name="attention_context",
    )(z, z, z)


def _attn_lat_kernel(q_ref, k_ref, v_ref, kc_ref, vc_ref, bias_ref, o_ref, *, heads, rows, kr):
    scale = HEAD_DIM ** -0.5
    r = pl.program_id(1)
    r0 = jnp.clip(r - kr // 2, 0, rows - kr)
    start = pl.multiple_of(r0 * GRID_W, GRID_W)
    k_win = k_ref[pl.ds(start, kr * GRID_W), :]
    v_win = v_ref[pl.ds(start, kr * GRID_W), :]
    gens = []
    for h in range(heads):
        sl = slice(h * HEAD_DIM, (h + 1) * HEAD_DIM)
        q = (q_ref[:, sl] * scale).astype(BF16)
        gens.append(_attn_head(q, [k_win[:, sl].astype(BF16), kc_ref[:, sl].astype(BF16)],
                               [v_win[:, sl].astype(BF16), vc_ref[:, sl].astype(BF16)], [bias_ref[h], None]))
    o_ref[...] = jnp.concatenate(_lock_step(gens), axis=-1)


def _toeplitz_kernel(rpb_ref, pick_ref, valid_ref, o_ref):
    o_ref[...] = jnp.where(valid_ref[...] > 0.5, _dot_hi(rpb_ref[...], pick_ref[...]), MASK_NEG)


def _window_bias(rpb, kr):
    delta = np.arange(kr)[:, None]
    i = np.arange(kr)[None, :]
    row_off = (WIN_ROWS - 1) - delta + i
    q = np.arange(GRID_W)[:, None]
    kc = np.arange(GRID_W)[None, :]
    c0 = np.clip(q - WIN_COLS // 2, 0, GRID_W - WIN_COLS)
    valid = (kc >= c0) & (kc < c0 + WIN_COLS)
    col_off = kc - q + (WIN_COLS - 1)
    pick = (col_off[None] == np.arange(2 * WIN_COLS - 1)[:, None, None]) & valid[None]
    H, n_ro, n_co = rpb.shape
    toep = pl.pallas_call(
        _toeplitz_kernel,
        out_shape=jax.ShapeDtypeStruct((H * n_ro, GRID_W * GRID_W), F32),
        name="window_bias",
    )(rpb.astype(F32).reshape(H * n_ro, n_co), jnp.asarray(pick.reshape(n_co, -1), F32),
      jnp.asarray(valid.reshape(1, -1), F32))
    toep = toep.reshape(H, n_ro, GRID_W, GRID_W)
    per_delta = []
    for dl in range(kr):
        lo = int(row_off[dl, 0])
        win = toep[:, lo:lo + kr]
        per_delta.append(win.transpose(0, 2, 1, 3).reshape(rpb.shape[0], GRID_W, kr * GRID_W))
    return jnp.stack(per_delta, axis=0)


def _attention_latent(z, k_ctx, v_ctx, bias, dec_seq, width):
    B, P, _ = k_ctx.shape
    rows = dec_seq // GRID_W
    kr = bias.shape[0]
    heads = width // HEAD_DIM

    def bias_index(b, r):
        return (r - jnp.clip(r - kr // 2, 0, rows - kr), 0, 0, 0)

    q_spec = pl.BlockSpec((GRID_W, width), lambda b, r: (b * rows + r, 0))
    return pl.pallas_call(
        functools.partial(_attn_lat_kernel, heads=heads, rows=rows, kr=kr),
        grid=(B, rows),
        in_specs=[
            q_spec,
            pl.BlockSpec((dec_seq, width), lambda b, r: (b, 1)),
            pl.BlockSpec((dec_seq, width), lambda b, r: (b, 2)),
            pl.BlockSpec((None, P, width), lambda b, r: (b, 0, 0)),
            pl.BlockSpec((None, P, width), lambda b, r: (b, 0, 0)),
            pl.BlockSpec((None, heads, GRID_W, kr * GRID_W), bias_index),
        ],
        out_specs=q_spec,
        out_shape=jax.ShapeDtypeStruct((B * dec_seq, width), F32),
        compiler_params=_params("arbitrary", "arbitrary"),
        name="attention_latent",
    )(z, z, z, k_ctx, v_ctx, bias)


def _sgu_kernel(u_ref, v_ref, gain_ref, w_ref, b_ref, o_ref, *, groups, chunks):
    for c in range(chunks):
        rows = slice(c * SG_CHUNK, (c + 1) * SG_CHUNK)
        for g in range(groups):
            sl = slice(g * SG_GROUP_DIM, (g + 1) * SG_GROUP_DIM)
            vn = _rms(v_ref[rows, sl]) * gain_ref[:, sl]
            mixed = _dot(w_ref[g].astype(BF16), vn.astype(BF16)) + b_ref[:, g:g + 1]
            o_ref[rows, sl] = u_ref[rows, sl] * mixed


def _spatial_gating(z, col_u, gain, w_s, b_t):
    T = z.shape[0]
    G = w_s.shape[0]
    W = G * SG_GROUP_DIM
    tm = _pick(T, (512, 256, 128))
    return pl.pallas_call(
        functools.partial(_sgu_kernel, groups=G, chunks=tm // SG_CHUNK),
        grid=(T // tm,),
        in_specs=[pl.BlockSpec((tm, W), lambda i: (i, col_u)),
                  pl.BlockSpec((tm, W), lambda i: (i, col_u + 1)),
                  _full(gain), _full(w_s), _full(b_t)],
        out_specs=pl.BlockSpec((tm, W), lambda i: (i, 0)),
        out_shape=jax.ShapeDtypeStruct((T, W), F32),
        compiler_params=_params("arbitrary"),
        name="spatial_gating",
    )(z, z, gain, w_s, b_t)


def _softplus(y):
    return jnp.maximum(y, 0.0) + jnp.log(1.0 + jnp.exp(-jnp.abs(y)))


def _rwkv_pre_kernel(r_ref, k_ref, v_ref, lora_ref, w0_ref, w2_ref, a0_ref, a2_ref, kkw_ref, ka_ref,
                     bd_ref, tri_ref,
                     at_f, rt_f, bh_f, kh_f, wc_f, at_b, rt_b, bh_b, kh_b, wc_b, v_o, *, pairs, chunks, lora_w, lora_a):
    C = RW_CHUNK
    r = r_ref[...]
    k = k_ref[...]
    bd = bd_ref[...]
    kk = k * kkw_ref[...]
    kk2 = kk * kk
    ssq = jnp.concatenate([_dot_wide_lhs(kk2[:, p * LANES:(p + 1) * LANES], bd) for p in range(pairs)], axis=-1)
    kk = kk * lax.rsqrt(ssq + KK_EPS)
    tw = jnp.tanh(lora_ref[:, :lora_w]).astype(BF16)
    xa = lora_ref[:, lora_w:lora_w + lora_a].astype(BF16)
    v_bf = v_ref[...].astype(BF16)
    for p in range(pairs):
        v_o[p] = v_bf[:, p * LANES:(p + 1) * LANES]
    outs = ((at_f, rt_f, bh_f, kh_f, wc_f), (at_b, rt_b, bh_b, kh_b, wc_b))
    for d in range(2):
        at_o, rt_o, bh_o, kh_o, wc_o = outs[d]
        w_log = -_softplus(-(w0_ref[d:d + 1, :] + _dot(tw, w2_ref[d].astype(BF16)))) - 0.5
        logw = -jnp.exp(w_log)
        a_rate = jax.nn.sigmoid(a0_ref[d:d + 1, :] + _dot(xa, a2_ref[d].astype(BF16)))
        k_d = k * (1.0 + (a_rate - 1.0) * ka_ref[...])
        b = kk * a_rate
        tri = tri_ref[d]
        tot_row = C - 1 if d == 0 else 0
        for c in range(chunks):
            rows = slice(c * C, (c + 1) * C)
            lw = logw[rows]
            cum = _dot_wide_rhs(tri, lw)
            tot = cum[tot_row:tot_row + 1, :]
            e_dn = jnp.exp(tot - cum)
            at = -kk[rows] * jnp.exp(cum - lw - tot)
            rt = r[rows] * jnp.exp(cum - tot)
            bh = b[rows] * e_dn
            kh = k_d[rows] * e_dn
            wc = jnp.exp(tot)
            for p in range(pairs):
                sl = slice(p * LANES, (p + 1) * LANES)
                at_o[p, rows, :] = at[:, sl].astype(BF16)
                rt_o[p, rows, :] = rt[:, sl].astype(BF16)
                bh_o[p, rows, :] = bh[:, sl].astype(BF16)
                kh_o[p, rows, :] = kh[:, sl].astype(BF16)
                wc_o[c, p] = wc[:, sl]


def _rwkv_cols(z_cols, width):
    bw = math.gcd(z_cols, width)
    assert bw % LANES == 0
    return bw, z_cols // bw, (z_cols + width) // bw, (z_cols + 2 * width) // bw


def _rwkv_prepare(z, r_off, lora_blk, lora_dims, w0, w2, a0, a2, kkw, ka, bd, tri):
    T = z.shape[0]
    RW = w0.shape[1]
    lora_w, lora_a, lora_g = lora_dims
    lora_n = lora_w + lora_a + lora_g
    bw, cr, ck, cv = _rwkv_cols(r_off, RW)
    pairs = bw // LANES
    C = RW_CHUNK
    tm = _pick(T, (256, 128, 64))
    tok = lambda c0: pl.BlockSpec((tm, bw), lambda i, h: (i, c0 + h))
    par2 = lambda a: pl.BlockSpec((a.shape[0], bw), lambda i, h: (0, h))
    par3 = lambda a: pl.BlockSpec((a.shape[0], a.shape[1], bw), lambda i, h: (0, 0, h))
    packed = pl.BlockSpec((pairs, tm, LANES), lambda i, h: (h, i, 0))
    wc_spec = pl.BlockSpec((tm // C, pairs, 1, LANES), lambda i, h: (i, h, 0, 0))
    packed_shape = jax.ShapeDtypeStruct((RW // LANES, T, LANES), BF16)
    wc_shape = jax.ShapeDtypeStruct((T // C, RW // LANES, 1, LANES), F32)
    return pl.pallas_call(
        functools.partial(_rwkv_pre_kernel, pairs=pairs, chunks=tm // C, lora_w=lora_w, lora_a=lora_a),
        grid=(T // tm, RW // bw),
        in_specs=[tok(cr), tok(ck), tok(cv),
                  pl.BlockSpec((tm, lora_n), lambda i, h: (i, lora_blk)),
                  par2(w0), par3(w2), par2(a0), par3(a2), par2(kkw), par2(ka), _full(bd), _full(tri)],
        out_specs=[packed, packed, packed, packed, wc_spec, packed, packed, packed, packed, wc_spec, packed],
        out_shape=[packed_shape] * 4 + [wc_shape] + [packed_shape] * 4 + [wc_shape] + [packed_shape],
        compiler_params=_params("arbitrary", "arbitrary"),
        name="rwkv_prepare",
    )(z, z, z, z, w0, w2, a0, a2, kkw, ka, bd, tri)


def _rwkv_pair_chunk(at, rt, bh, kh, v, hs, masks):
    C = RW_CHUNK
    strict, incl, eye, same_head, lane_lo = masks
    zero = jnp.zeros_like(at)
    stack = lambda x: jnp.concatenate([jnp.where(lane_lo, x, zero), jnp.where(lane_lo, zero, x)], axis=0)
    twice = lambda x: jnp.concatenate([x, x], axis=0)
    atm, rtm, vm = stack(at), stack(rt), stack(v)
    bk = jnp.concatenate([twice(bh), twice(kh)], axis=0)
    a = _dot_nt(jnp.concatenate([atm, rtm], axis=0), bk)
    yield
    l = jnp.where(strict, a[:2 * C, :2 * C], 0.0)
    a_ak = jnp.where(strict, a[:2 * C, 2 * C:], 0.0).astype(BF16)
    a_rb = jnp.where(incl, a[2 * C:, :2 * C], 0.0).astype(BF16)
    a_rk = jnp.where(incl, a[2 * C:, 2 * C:], 0.0).astype(BF16)
    hs_b = hs.astype(BF16)
    x2 = _dot(jnp.concatenate([atm, a_ak], axis=1), jnp.concatenate([hs_b, vm], axis=0)).astype(BF16)
    t = eye + l
    pw = l.astype(BF16)
    pw = _dot(pw, pw).astype(BF16)
    yield
    for _ in range(int(math.log2(C)) - 2):
        both = _dot(jnp.concatenate([t.astype(BF16), pw], axis=0), pw)
        yield
        t = t + both[:2 * C]
        pw = both[2 * C:].astype(BF16)
    t = t + _dot(t.astype(BF16), pw)
    yield
    u = _dot(t.astype(BF16), x2).astype(BF16)
    yield
    o = _dot(jnp.concatenate([rtm, a_rb, a_rk], axis=1), jnp.concatenate([hs_b, u, vm], axis=0))
    h_new = hs + jnp.where(same_head, _dot_tn(bk, jnp.concatenate([u, vm], axis=0)), 0.0)
    return o[:C] + o[C:], h_new


def _rwkv_core_kernel(at_f, rt_f, bh_f, kh_f, v_f, wc_f, at_b, rt_b, bh_b, kh_b, v_b, wc_b, h0_ref,
                      of_ref, ob_ref, hs_ref, *, pairs, group):
    C = RW_CHUNK

    @pl.when(pl.program_id(1) == 0)
    def _():
        hs_ref[...] = h0_ref[...]

    row = _iota((2 * C, 2 * C), 0)
    col = _iota((2 * C, 2 * C), 1)
    same = (row // C) == (col // C)
    rt_, ct_ = row % C, col % C
    eye = jnp.where(row == col, 1.0, 0.0).astype(F32)
    lrow = _iota((LANES, LANES), 0)
    lcol = _iota((LANES, LANES), 1)
    same_head = (lrow // HEAD_DIM) == (lcol // HEAD_DIM)
    lane_lo = _iota((C, LANES), 1) < HEAD_DIM
    masks = ((same & (ct_ < rt_), same & (ct_ <= rt_), eye, same_head, lane_lo),
             (same & (ct_ > rt_), same & (ct_ >= rt_), eye, same_head, lane_lo))
    dirs = ((at_f, rt_f, bh_f, kh_f, v_f, wc_f, of_ref), (at_b, rt_b, bh_b, kh_b, v_b, wc_b, ob_ref))

    for g0 in range(0, pairs, group):
        jobs = [(d, p) for p in range(g0, g0 + group) for d in range(2)]
        gens = []
        for d, p in jobs:
            at_r, rt_r, bh_r, kh_r, v_r, wc_r, _ = dirs[d]
            wcol = jnp.sum(jnp.where(lrow == lcol, jnp.broadcast_to(wc_r[p], (LANES, LANES)), 0.0),
                           axis=1, keepdims=True)
            gens.append(_rwkv_pair_chunk(at_r[p], rt_r[p], bh_r[p], kh_r[p], v_r[p], hs_ref[d, p] * wcol, masks[d]))
        for (d, p), (o, h_new) in zip(jobs, _lock_step(gens)):
            hs_ref[d, p] = h_new
            dirs[d][-1][p] = o


def _pair_states(h):
    n, _, H, _, _ = h.shape
    hp = h.reshape(n, 2, H // 2, 2, HEAD_DIM, HEAD_DIM)
    z = jnp.zeros_like(hp[:, :, :, 0])
    top = jnp.concatenate([hp[:, :, :, 0], z], axis=-1)
    bot = jnp.concatenate([z, hp[:, :, :, 1]], axis=-1)
    return jnp.concatenate([top, bot], axis=-2)


def _unpair_states(hp):
    n, _, P, _, _ = hp.shape
    h0 = hp[:, :, :, :HEAD_DIM, :HEAD_DIM]
    h1 = hp[:, :, :, HEAD_DIM:, HEAD_DIM:]
    return jnp.stack([h0, h1], axis=3).reshape(n, 2, 2 * P, HEAD_DIM, HEAD_DIM)


def _rwkv_core(pre, h0, n_seq, seq_len):
    at_f, rt_f, bh_f, kh_f, wc_f, at_b, rt_b, bh_b, kh_b, wc_b, v = pre
    pairs, T, _ = at_f.shape
    C = RW_CHUNK
    nch = seq_len // C
    fwd = lambda s, c: s * nch + c
    bwd = lambda s, c: s * nch + (nch - 1 - c)
    pk = lambda f: pl.BlockSpec((pairs, C, LANES), lambda s, c: (0, f(s, c), 0))
    wc = lambda f: pl.BlockSpec((None, pairs, 1, LANES), lambda s, c: (f(s, c), 0, 0, 0))
    st = pl.BlockSpec((None, 2, pairs, LANES, LANES), lambda s, c: (s, 0, 0, 0, 0))
    o_shape = jax.ShapeDtypeStruct((pairs, T, LANES), F32)
    return pl.pallas_call(
        functools.partial(_rwkv_core_kernel, pairs=pairs, group=_pick(pairs, (RW_PAIR_GROUP, 4, 2, 1))),
        grid=(n_seq, nch),
        in_specs=[pk(fwd), pk(fwd), pk(fwd), pk(fwd), pk(fwd), wc(fwd),
                  pk(bwd), pk(bwd), pk(bwd), pk(bwd), pk(bwd), wc(bwd), st],
        out_specs=[pk(fwd), pk(bwd), st],
        out_shape=[o_shape, o_shape, jax.ShapeDtypeStruct(h0.shape, F32)],
        compiler_params=_params("arbitrary", "arbitrary"),
        name="rwkv_scan",
    )(at_f, rt_f, bh_f, kh_f, v, wc_f, at_b, rt_b, bh_b, kh_b, v, wc_b, h0)


def _rwkv_post_kernel(of_ref, ob_ref, r_ref, k_ref, v_ref, lora_ref, g2_ref, lnw_ref, lnb_ref, rk_ref, bd_ref,
                      o_ref, *, pairs, lora_g):
    bd = bd_ref[...]
    n_lora = lora_ref.shape[1]
    sg = jax.nn.sigmoid(lora_ref[:, n_lora - lora_g:]).astype(BF16)
    inv = 1.0 / HEAD_DIM
    for p in range(pairs):
        sl = slice(p * LANES, (p + 1) * LANES)
        o = of_ref[p] + ob_ref[p]
        mu = _dot_wide_lhs(o, bd) * inv
        dlt = o - mu
        var = _dot_wide_lhs(dlt * dlt, bd) * inv
        on = dlt * lax.rsqrt(var + GN_EPS) * lnw_ref[:, sl] + lnb_ref[:, sl]
        bonus = _dot_wide_lhs(r_ref[:, sl] * k_ref[:, sl] * rk_ref[:, sl], bd) * v_ref[:, sl]
        gate = _dot(sg, g2_ref[:, sl].astype(BF16))
        o_ref[:, sl] = (on + bonus) * gate


def _rwkv_finish(o_f, o_b, z, r_off, lora_blk, lora_dims, g2, lnw, lnb, rk, bd):
    T = z.shape[0]
    RW = g2.shape[1]
    lora_n = sum(lora_dims)
    bw, cr, ck, cv = _rwkv_cols(r_off, RW)
    pairs = bw // LANES
    tm = _pick(T, (256, 128, 64))
    tok = lambda c0: pl.BlockSpec((tm, bw), lambda i, h: (i, c0 + h))
    par2 = lambda a: pl.BlockSpec((a.shape[0], bw), lambda i, h: (0, h))
    packed = pl.BlockSpec((pairs, tm, LANES), lambda i, h: (h, i, 0))
    return pl.pallas_call(
        functools.partial(_rwkv_post_kernel, pairs=pairs, lora_g=lora_dims[2]),
        grid=(T // tm, RW // bw),
        in_specs=[packed, packed, tok(cr), tok(ck), tok(cv),
                  pl.BlockSpec((tm, lora_n), lambda i, h: (i, lora_blk)),
                  par2(g2), par2(lnw), par2(lnb), par2(rk), _full(bd)],
        out_specs=pl.BlockSpec((tm, bw), lambda i, h: (i, h)),
        out_shape=jax.ShapeDtypeStruct((T, RW), F32),
        compiler_params=_params("arbitrary", "arbitrary"),
        name="rwkv_finish",
    )(o_f, o_b, z, z, z, z, g2, lnw, lnb, rk, bd)


def _outproj_kernel(oa_ref, ob_ref, oc_ref, x_ref, w_ref, g1_ref, sh2_ref, sc2_ref, gpost_ref, gpre_ref, wr_ref,
                    xo_ref, h2_ref, aff_ref):
    cat = jnp.concatenate([oa_ref[...], ob_ref[...], oc_ref[...]], axis=-1).astype(BF16)
    mixed = _dot(cat, w_ref[...])
    x = x_ref[...] + g1_ref[...] * (_rms(mixed) * gpost_ref[...])
    xo_ref[...] = x
    h2 = _rms(x) * gpre_ref[...] * (1.0 + sc2_ref[...]) + sh2_ref[...]
    h2_ref[...] = h2.astype(BF16)
    h_hi, h_lo = _split2(h2)
    w_hi, w_lo = _split2(wr_ref[...])
    logits = _dot(h_hi, w_hi) + _dot(h_lo, w_hi) + _dot(h_hi, w_lo)
    e = jnp.exp(logits - jnp.max(logits, axis=-1, keepdims=True))
    aff_ref[...] = e / jnp.sum(e, axis=-1, keepdims=True)


def _out_projection(o_a, o_b, o_c, x, w_out, layer, mod3, mod_row, g_post, g_pre, w_router, tm):
    T, D = x.shape
    E = w_router.shape[1]
    row = lambda a: pl.BlockSpec((tm, a.shape[1]), lambda i: (i, 0))
    mod = lambda col: pl.BlockSpec((None, 1, D), lambda i: (mod_row(i), 0, col))
    return pl.pallas_call(
        _outproj_kernel,
        grid=(T // tm,),
        in_specs=[row(o_a), row(o_b), row(o_c), row(x), pl.BlockSpec((None, D, D), lambda i: (layer, 0, 0)),
                  mod(2), mod(3), mod(4), _full(g_post), _full(g_pre), _full(w_router)],
        out_specs=[pl.BlockSpec((tm, D), lambda i: (i, 0)),
                   pl.BlockSpec((tm, D), lambda i: (i, 0)),
                   pl.BlockSpec((tm, E), lambda i: (i, 0))],
        out_shape=[jax.ShapeDtypeStruct((T, D), F32), jax.ShapeDtypeStruct((T, D), BF16),
                   jax.ShapeDtypeStruct((T, E), F32)],
        compiler_params=_params("arbitrary"),
        name="out_projection",
    )(o_a, o_b, o_c, x, w_out, mod3, mod3, mod3, g_post, g_pre, w_router)


def _select_kernel(a_ref, ut_ref, pos_ref, cnt_ref, *, cap):
    a = a_ref[...]
    R, T = a.shape
    lo = jnp.zeros((R, 1), jnp.int32)
    for bit in range(30, -1, -1):
        cand = lo | (1 << bit)
        cnt = jnp.sum(jnp.where(a >= lax.bitcast_convert_type(cand, F32), 1.0, 0.0), axis=1, keepdims=True)
        lo = jnp.where(cnt >= cap, cand, lo)
    thr = lax.bitcast_convert_type(lo, F32)
    gt = a > thr
    eq = a == thr
    need = cap - jnp.sum(jnp.where(gt, 1.0, 0.0), axis=1, keepdims=True)
    ut = ut_ref[...]
    nblk = T // LANES
    tie_carry = jnp.zeros((R, 1), F32)
    pos_carry = jnp.zeros((R, 1), F32)
    for n in range(nblk):
        sl = slice(n * LANES, (n + 1) * LANES)
        eq_b = jnp.where(eq[:, sl], 1.0, 0.0)
        tie_rank = _dot(eq_b.astype(BF16), ut) + tie_carry
        tie_carry = tie_carry + jnp.sum(eq_b, axis=1, keepdims=True)
        sel = jnp.where(gt[:, sl], 1.0, jnp.where(tie_rank < need, eq_b, 0.0))
        pos = _dot(sel.astype(BF16), ut) + pos_carry
        cnt_ref[:, n:n + 1] = pos_carry.astype(jnp.int32)
        pos_carry = pos_carry + jnp.sum(sel, axis=1, keepdims=True)
        pos_ref[:, sl] = jnp.where(sel > 0.5, pos, -1.0).astype(jnp.int32)


def _select(aff_rows, cap, ut):
    R, T = aff_rows.shape
    tr = _pick(R, (128, 64, 32, 16, 8))
    return pl.pallas_call(
        functools.partial(_select_kernel, cap=cap),
        grid=(R // tr,),
        in_specs=[pl.BlockSpec((tr, T), lambda i: (i, 0)), pl.BlockSpec((LANES, LANES), lambda i: (0, 0))],
        out_specs=[pl.BlockSpec((tr, T), lambda i: (i, 0)), pl.BlockSpec((tr, T // LANES), lambda i: (i, 0))],
        out_shape=[jax.ShapeDtypeStruct((R, T), jnp.int32), jax.ShapeDtypeStruct((R, T // LANES), jnp.int32)],
        compiler_params=_params("arbitrary"),
        name="expert_select",
    )(aff_rows, ut)


def _route(aff, n_sets, set_len, cap, ut):
    E = aff.shape[1]
    rows = aff.reshape(n_sets, set_len, E).transpose(0, 2, 1).reshape(n_sets * E, set_len)
    pos, cnt = _select(rows, cap, ut)
    pos = pos.reshape(n_sets, E, set_len)
    return pos, pos.transpose(0, 2, 1).reshape(n_sets * set_len, E), cnt, rows


def _slot_tables(experts, cap):
    pair_expert = np.arange(experts * cap) // cap
    rep = (pair_expert[:, None] == np.arange(experts)[None, :]).astype(np.float32)
    slot = (np.arange(experts * cap) % cap).astype(np.float32)
    return jnp.asarray(rep, BF16), jnp.asarray(rep.T, BF16), jnp.asarray(slot[:, None]), jnp.asarray(slot[None, :])


def _gather_ctx_kernel(h_ref, pos_ref, aff_ref, rep_ref, slot_ref, xs_ref, gs_ref, *, experts, cap):
    rep = rep_ref[...]
    pos_rep = _dot(rep, pos_ref[...].astype(F32).astype(BF16))
    hit = pos_rep == slot_ref[...]
    xs = _dot(jnp.where(hit, 1.0, 0.0).astype(BF16), h_ref[...]).astype(BF16)
    xs_ref[...] = xs.reshape(experts, cap, xs.shape[1])
    gs = jnp.sum(jnp.where(hit, _dot_wide_rhs(rep, aff_ref[...]), 0.0), axis=1, keepdims=True)
    gs_ref[...] = gs.reshape(experts, cap, 1)


def _gather_lat_kernel(cnt_ref, h_ref, pos_ref, aff_ref, xs_ref, gs_ref, acc_scr, gacc_scr, *, cap, experts):
    row = pl.program_id(0) * experts + pl.program_id(1)
    n_blocks = pos_ref.shape[0]
    step = GATHER_TOKENS // LANES
    for i in range(cap // SLOT_TILE):
        first_slot = i * SLOT_TILE
        starts = [cnt_ref[row, n * step] for n in range(n_blocks)]
        n_lo = sum(jnp.where(s <= first_slot, 1, 0) for s in starts) - 1
        n_hi = sum(jnp.where(s < first_slot + SLOT_TILE, 1, 0) for s in starts) - 1
        acc_scr[...] = jnp.zeros(acc_scr.shape, F32)
        gacc_scr[...] = jnp.zeros(gacc_scr.shape, F32)

        def body(n, carry, first_slot=first_slot):
            toks = h_ref[pl.ds(pl.multiple_of(n * GATHER_TOKENS, GATHER_TOKENS), GATHER_TOKENS), :]
            hit = _iota((SLOT_TILE, GATHER_TOKENS), 0) + first_slot == pos_ref[n]
            acc_scr[...] += _dot(jnp.where(hit, 1.0, 0.0).astype(BF16), toks)
            gacc_scr[...] += jnp.sum(jnp.where(hit, aff_ref[n], 0.0), axis=1, keepdims=True)
            return carry

        lax.fori_loop(n_lo, n_hi + 1, body, 0)
        xs_ref[first_slot:first_slot + SLOT_TILE, :] = acc_scr[...].astype(BF16)
        gs_ref[first_slot:first_slot + SLOT_TILE, :] = gacc_scr[...]


def _expert_up_kernel(xc_ref, xl_ref, wg_ref, wu_ref, hid_ref):
    wg = wg_ref[...].astype(BF16)
    wu = wu_ref[...].astype(BF16)
    rc = xc_ref.shape[0]
    for ref, rows in ((xc_ref, slice(0, rc)), (xl_ref, slice(rc, hid_ref.shape[0]))):
        x = ref[...]
        g = _dot(x, wg)
        hid_ref[rows, :] = (g * jax.nn.sigmoid(g) * _dot(x, wu)).astype(BF16)


def _expert_down_kernel(hid_ref, gs_ref, wd_ref, y_ref):
    y_ref[...] = (_dot(hid_ref[...], wd_ref[...].astype(BF16)) * gs_ref[...]).astype(BF16)


def _slot_one_hot(pos_col, cap):
    tm = pos_col.shape[0]
    return jnp.where(_iota((tm, cap), 1).astype(F32) == pos_col, 1.0, 0.0).astype(BF16)


def _col_chunks(width):
    cw = _pick(width, (2 * LANES, LANES))
    return [slice(n * cw, (n + 1) * cw) for n in range(width // cw)]


def _ffn_residual(x, ffn, g2, g_post):
    return x + g2 * (_rms(ffn) * g_post)


def _combine_ctx_kernel(x_ref, y_ref, pos_ref, rep_ref, slot_ref, g2_ref, gpost_ref, o_ref, *, experts, cap):
    pos_exp = _dot(pos_ref[...].astype(F32).astype(BF16), rep_ref[...])
    one_hot = jnp.where(pos_exp == slot_ref[...], 1.0, 0.0).astype(BF16)
    y = y_ref[...].reshape(experts * cap, y_ref.shape[2])
    o_ref[...] = _ffn_residual(x_ref[...], _dot(one_hot, y), g2_ref[...], gpost_ref[...])


def _combine_lat_kernel(x_ref, y_ref, pos_ref, g2_ref, gpost_ref, o_ref, acc_scr, *, cap, group):
    eg = pl.program_id(2)

    @pl.when(eg == 0)
    def _():
        acc_scr[...] = jnp.zeros(acc_scr.shape, F32)

    lane = _iota(pos_ref.shape, 1)
    pos = pos_ref[...].astype(F32)
    one_hots = []
    for k in range(group):
        pos_col = jnp.sum(jnp.where(lane == eg * group + k, pos, 0.0), axis=1, keepdims=True)
        one_hots.append(_slot_one_hot(pos_col, cap))
    for cs in _col_chunks(acc_scr.shape[1]):
        acc_scr[:, cs] += sum(_dot(oh, y_ref[k, :, cs]) for k, oh in enumerate(one_hots))

    @pl.when(eg == pl.num_programs(2) - 1)
    def _():
        o_ref[...] = _ffn_residual(x_ref[...], acc_scr[...], g2_ref[...], gpost_ref[...])


def _expert_ffn(x_c, h2_c, aff_c, x_l, h2_l, aff_l, mod3, g_post, w_gate, w_up, w_down, layer, ut, dims):
    batch, seq, dec_batch, dec_seq = dims
    D = x_c.shape[1]
    _, E, _, F = w_gate.shape
    cap_c = CAPACITY_FACTOR * seq // E
    cap_l = CAPACITY_FACTOR * dec_seq // E
    rows_c = batch * cap_c
    rows_l = dec_batch * cap_l
    rows = rows_c + rows_l
    assert rows_c % cap_l == 0
    lat_row_blk = rows_c // cap_l

    assert cap_c <= 256 and cap_l % SLOT_TILE == 0 and dec_seq % GATHER_TOKENS == 0 and E % COMBINE_GROUP == 0
    pos_c, post_c, _, affr_c = _route(aff_c, batch, seq, cap_c, ut)
    pos_l, post_l, cnt_l, affr_l = _route(aff_l, dec_batch, dec_seq, cap_l, ut)
    rep, rep_t, slot_col, slot_row = _slot_tables(E, cap_c)

    set_rows = pl.BlockSpec((None, E, seq), lambda b: (b, 0, 0))
    xs_c, gs_c = pl.pallas_call(
        functools.partial(_gather_ctx_kernel, experts=E, cap=cap_c),
        grid=(batch,),
        in_specs=[pl.BlockSpec((seq, D), lambda b: (b, 0)), set_rows, set_rows, _full(rep), _full(slot_col)],
        out_specs=[pl.BlockSpec((E, cap_c, D), lambda b: (0, b, 0)), pl.BlockSpec((E, cap_c, 1), lambda b: (0, b, 0))],
        out_shape=[jax.ShapeDtypeStruct((E, rows_c, D), BF16), jax.ShapeDtypeStruct((E, rows_c, 1), F32)],
        compiler_params=_params("arbitrary"),
        name="expert_gather_context",
    )(h2_c, pos_c, affr_c.reshape(batch, E, seq), rep, slot_col)
    n_tok_blk = dec_seq // GATHER_TOKENS
    blocked = pl.BlockSpec((None, None, n_tok_blk, 1, GATHER_TOKENS), lambda b, e, cnt: (b, e, 0, 0, 0))
    xs_l, gs_l = pl.pallas_call(
        functools.partial(_gather_lat_kernel, cap=cap_l, experts=E),
        grid_spec=pltpu.PrefetchScalarGridSpec(
            num_scalar_prefetch=1,
            grid=(dec_batch, E),
            in_specs=[pl.BlockSpec((dec_seq, D), lambda b, e, cnt: (b, 0)), blocked, blocked],
            out_specs=[pl.BlockSpec((None, cap_l, D), lambda b, e, cnt: (e, b, 0)),
                       pl.BlockSpec((None, cap_l, 1), lambda b, e, cnt: (e, b, 0))],
            scratch_shapes=[pltpu.VMEM((SLOT_TILE, D), F32), pltpu.VMEM((SLOT_TILE, 1), F32)]),
        out_shape=[jax.ShapeDtypeStruct((E, rows_l, D), BF16), jax.ShapeDtypeStruct((E, rows_l, 1), F32)],
        compiler_params=_params("arbitrary", "arbitrary"),
        name="expert_gather_latent",
    )(cnt_l, h2_l, pos_l.reshape(dec_batch, E, n_tok_blk, 1, GATHER_TOKENS),
      affr_l.reshape(dec_batch, E, n_tok_blk, 1, GATHER_TOKENS))
    gs = jnp.concatenate([gs_c, gs_l], axis=1)

    tf = _pick(F, (512, 256, 128))
    hid = pl.pallas_call(
        _expert_up_kernel,
        grid=(E, F // tf),
        in_specs=[pl.BlockSpec((None, rows_c, D), lambda e, n: (e, 0, 0)),
                  pl.BlockSpec((None, rows_l, D), lambda e, n: (e, 0, 0)),
                  pl.BlockSpec((None, None, D, tf), lambda e, n: (layer, e, 0, n)),
                  pl.BlockSpec((None, None, D, tf), lambda e, n: (layer, e, 0, n))],
        out_specs=pl.BlockSpec((None, rows, tf), lambda e, n: (e, 0, n)),
        out_shape=jax.ShapeDtypeStruct((E, rows, F), BF16),
        compiler_params=_params("arbitrary", "arbitrary"),
        name="expert_up",
    )(xs_c, xs_l, w_gate, w_up)
    td = _pick(D, (512, 256, 128))
    y = pl.pallas_call(
        _expert_down_kernel,
        grid=(E, D // td),
        in_specs=[pl.BlockSpec((None, rows, F), lambda e, n: (e, 0, 0)),
                  pl.BlockSpec((None, rows, 1), lambda e, n: (e, 0, 0)),
                  pl.BlockSpec((None, None, F, td), lambda e, n: (layer, e, 0, n))],
        out_specs=pl.BlockSpec((None, rows, td), lambda e, n: (e, 0, n)),
        out_shape=jax.ShapeDtypeStruct((E, rows, D), BF16),
        compiler_params=_params("arbitrary", "arbitrary"),
        name="expert_down",
    )(hid, gs, w_down)

    out_c = pl.pallas_call(
        functools.partial(_combine_ctx_kernel, experts=E, cap=cap_c),
        grid=(batch,),
        in_specs=[pl.BlockSpec((seq, D), lambda b: (b, 0)),
                  pl.BlockSpec((E, cap_c, D), lambda b: (0, b, 0)),
                  pl.BlockSpec((seq, E), lambda b: (b, 0)),
                  _full(rep_t), _full(slot_row),
                  pl.BlockSpec((None, 1, D), lambda b: (0, 0, 5)),
                  _full(g_post)],
        out_specs=pl.BlockSpec((seq, D), lambda b: (b, 0)),
        out_shape=jax.ShapeDtypeStruct(x_c.shape, F32),
        compiler_params=_params("arbitrary"),
        name="expert_combine_context",
    )(x_c, y, post_c, rep_t, slot_row, mod3, g_post)
    tm = _pick(dec_seq, (512, 256, 128))
    nt = dec_seq // tm
    tok = lambda b, i, e: (b * nt + i, 0)
    out_l = pl.pallas_call(
        functools.partial(_combine_lat_kernel, cap=cap_l, group=COMBINE_GROUP),
        grid=(dec_batch, nt, E // COMBINE_GROUP),
        in_specs=[pl.BlockSpec((tm, D), tok),
                  pl.BlockSpec((COMBINE_GROUP, cap_l, D), lambda b, i, e: (e, lat_row_blk + b, 0)),
                  pl.BlockSpec((tm, E), tok),
                  pl.BlockSpec((None, 1, D), lambda b, i, e: (1 + b, 0, 5)),
                  _full(g_post)],
        out_specs=pl.BlockSpec((tm, D), tok),
        out_shape=jax.ShapeDtypeStruct(x_l.shape, F32),
        scratch_shapes=[pltpu.VMEM((tm, D), F32)],
        compiler_params=_params("arbitrary", "arbitrary", "arbitrary"),
        name="expert_combine_latent",
    )(x_l, y, post_l, mod3, g_post)
    return out_c, out_l


def kernel(x_prompt, x_sample, cache_k, cache_v, state_wkv, c, c_ctx, w_ada, b_ada, g_pre_mix, g_post_mix, g_pre_ffn, g_post_ffn, w_in, na_rpb, sg_gain, sg_w, sg_b, rw_w0, rw_w2, rw_a0, rw_a2, rw_g2, rw_kk, rw_ka, rw_rk, rw_ln_w, rw_ln_b, w_out, w_router, w_gate, w_up, w_down):
    batch, seq, D = x_prompt.shape
    dec_batch, dec_seq, _ = x_sample.shape
    depth = w_ada.shape[0]
    past = cache_k.shape[2]
    na_heads = cache_k.shape[3]
    naw = na_heads * HEAD_DIM
    sgw = sg_gain.shape[1]
    rw_heads = state_wkv.shape[3]
    rww = rw_heads * HEAD_DIM
    lora_dims = (rw_w2.shape[2], rw_a2.shape[2], rw_g2.shape[1])
    n_ctx = batch * seq
    n_lat = dec_batch * dec_seq
    assert 1 + dec_batch <= MOD_ROWS and rww % LANES == 0
    assert seq % RW_CHUNK == 0 and dec_seq % RW_CHUNK == 0 and dec_seq % GRID_W == 0
    sg_off = 3 * naw
    rw_off = sg_off + 2 * sgw
    lora_off = rw_off + 3 * rww
    assert sg_off % sgw == 0 and lora_off % sum(lora_dims) == 0
    col_u = sg_off // sgw
    lora_blk = lora_off // sum(lora_dims)

    pair_ones = jnp.asarray(np.kron(np.eye(LANES // HEAD_DIM), np.ones((HEAD_DIM, HEAD_DIM))), BF16)
    tril = np.tril(np.ones((RW_CHUNK, RW_CHUNK)))
    tri = jnp.asarray(np.stack([tril, tril.T]), BF16)
    ut = jnp.asarray(np.triu(np.ones((LANES, LANES)), 1), BF16)
    kr = min(WIN_ROWS, dec_seq // GRID_W)

    cond = jnp.zeros((MOD_ROWS, D), F32).at[0].set(c_ctx).at[1:1 + dec_batch].set(c)
    mods = _modulation(cond, w_ada, b_ada.reshape(depth, 1, N_MOD * D))
    w_in_b = _to_bf16(w_in)
    w_out_b = _to_bf16(w_out)

    tm_c = _pick(n_ctx, (512, 256, 128))
    tm_l = _pick(dec_seq, (512, 256, 128))
    ctx_row = lambda i: 0
    lat_row = lambda i: 1 + (i * tm_l) // dec_seq

    x_c = x_prompt.reshape(n_ctx, D)
    x_l = x_sample.reshape(n_lat, D)
    h0_ctx = jnp.zeros((batch, 2, rww // LANES, LANES, LANES), F32)
    ks, vs, ss = [], [], []
    for l in range(depth):
        mod3 = mods[l].reshape(MOD_ROWS, 1, N_MOD * D)
        g_pre = g_pre_mix[l].reshape(1, D)
        rw_args = (rw_w0[l], rw_w2[l], rw_a0[l], rw_a2[l], rw_kk[l].reshape(1, rww), rw_ka[l].reshape(1, rww),
                   pair_ones, tri)
        fin_args = (rw_g2[l], rw_ln_w[l].reshape(1, rww), rw_ln_b[l].reshape(1, rww), rw_rk[l].reshape(1, rww),
                    pair_ones)
        sg_args = (sg_gain[l].reshape(1, sgw), sg_w[l], sg_b[l].T)

        def mixers(x, tm, mod_row, attend, h0, n_seq, seq_len):
            z = _in_projection(x, mod3, mod_row, g_pre, w_in_b, l, tm)
            o_a = attend(z)
            o_b = _spatial_gating(z, col_u, *sg_args)
            pre = _rwkv_prepare(z, rw_off, lora_blk, lora_dims, *rw_args)
            o_f, o_bk, s_fin = _rwkv_core(pre, h0, n_seq, seq_len)
            o_c = _rwkv_finish(o_f, o_bk, z, rw_off, lora_blk, lora_dims, *fin_args)
            x, h2, aff = _out_projection(o_a, o_b, o_c, x, w_out_b, l, mod3, mod_row,
                                         g_post_mix[l].reshape(1, D), g_pre_ffn[l].reshape(1, D), w_router[l], tm)
            return z, x, h2, aff, s_fin

        z_c, x_c, h2_c, aff_c, s_ctx = mixers(
            x_c, tm_c, ctx_row, lambda z: _attention_context(z, seq, naw), h0_ctx, batch, seq)
        ks.append(z_c[:, naw:2 * naw].reshape(batch, seq, na_heads, HEAD_DIM))
        vs.append(z_c[:, 2 * naw:3 * naw].reshape(batch, seq, na_heads, HEAD_DIM))
        ss.append(jnp.swapaxes(_unpair_states(s_ctx), -1, -2))

        bias = _window_bias(na_rpb[l], kr)
        k_past = cache_k[:, l].reshape(dec_batch, past, naw)
        v_past = cache_v[:, l].reshape(dec_batch, past, naw)
        h0_lat = _pair_states(jnp.swapaxes(state_wkv[:, l], -1, -2))
        _, x_l, h2_l, aff_l, _ = mixers(
            x_l, tm_l, lat_row, lambda z: _attention_latent(z, k_past, v_past, bias, dec_seq, naw),
            h0_lat, dec_batch, dec_seq)

        x_c, x_l = _expert_ffn(x_c, h2_c, aff_c, x_l, h2_l, aff_l, mod3, g_post_ffn[l].reshape(1, D),
                               w_gate, w_up, w_down, l, ut, (batch, seq, dec_batch, dec_seq))

    return (x_c.reshape(batch, seq, D), x_l.reshape(dec_batch, dec_seq, D),
            jnp.stack(ks, axis=1), jnp.stack(vs, axis=1), jnp.stack(ss, axis=1))
```

```python
import functools
import math

import numpy as np
import jax
import jax.numpy as jnp
from jax import lax
from jax.experimental import pallas as pl
from jax.experimental.pallas import tpu as pltpu

F32 = jnp.float32
BF16 = jnp.bfloat16
HIGHEST = lax.Precision.HIGHEST

HEAD_DIM = 64
LANES = 128
GRID_W = 64
WIN_ROWS = 8
WIN_COLS = 16
SG_CHUNK = 128
SG_GROUP_DIM = 128
N_MOD = 6
CAPACITY_FACTOR = 2
NORM_EPS = 1e-6
GN_EPS = 64e-5
KK_EPS = 1e-12
RW_CHUNK = 64
RW_PAIR_GROUP = 8
MASK_NEG = -1e30
MOD_ROWS = 8
VMEM_LIMIT_BYTES = 56 * 1024 * 1024
MAX_WEIGHT_COLS = 3072
GATHER_TOKENS = 256
SLOT_TILE = 128
COMBINE_GROUP = 4


def _pick(n, prefs):
    for p in prefs:
        if n % p == 0:
            return p
    raise ValueError(f"no tile in {prefs} divides {n}")


def _col_tile(n, cap):
    tiles = [t for t in range(LANES, min(n, cap) + 1, LANES) if n % t == 0]
    return tiles[-1]


def _params(*sem):
    return pltpu.CompilerParams(dimension_semantics=sem, vmem_limit_bytes=VMEM_LIMIT_BYTES)


def _rms(x):
    return x * lax.rsqrt(jnp.mean(x * x, axis=-1, keepdims=True) + NORM_EPS)


def _dot(a, b):
    return jnp.dot(a, b, preferred_element_type=F32)


def _dot_nt(a, b):
    return lax.dot_general(a, b, (((1,), (1,)), ((), ())), preferred_element_type=F32)


def _dot_tn(a, b):
    return lax.dot_general(a, b, (((0,), (0,)), ((), ())), preferred_element_type=F32)


def _dot_hi(a, b):
    return jnp.dot(a, b, preferred_element_type=F32, precision=HIGHEST)


def _split2(a):
    hi = a.astype(BF16)
    return hi, (a - hi.astype(F32)).astype(BF16)


def _dot_wide_lhs(a, b):
    hi, lo = _split2(a)
    return _dot(hi, b) + _dot(lo, b)


def _dot_wide_rhs(a, b):
    hi, lo = _split2(b)
    return _dot(a, hi) + _dot(a, lo)


def _lock_step(gens):
    results = [None] * len(gens)
    live = list(enumerate(gens))
    while live:
        still = []
        for i, g in live:
            try:
                next(g)
                still.append((i, g))
            except StopIteration as done:
                results[i] = done.value
        live = still
    return results


def _iota(shape, axis):
    return lax.broadcasted_iota(jnp.int32, shape, axis)


def _full(a):
    return pl.BlockSpec(a.shape, lambda *_: (0,) * a.ndim)


def _of_layer(a, layer, block=None, index=None):
    shape = tuple(a.shape[1:]) if block is None else tuple(block)
    if index is None:
        return pl.BlockSpec((None,) + shape, lambda *_: (layer,) + (0,) * len(shape))
    return pl.BlockSpec((None,) + shape, lambda *g: (layer,) + tuple(index(*g)))


def _cast_kernel(w_ref, o_ref):
    o_ref[...] = w_ref[...].astype(BF16)


def _to_bf16(w):
    L, R, C = w.shape
    tr = _pick(R, (256, 128, 64, 32, 16))
    spec = pl.BlockSpec((None, tr, C), lambda l, i: (l, i, 0))
    return pl.pallas_call(
        _cast_kernel, grid=(L, R // tr), in_specs=[spec], out_specs=spec,
        out_shape=jax.ShapeDtypeStruct(w.shape, BF16),
        compiler_params=_params("arbitrary", "arbitrary"), name="cast_weights",
    )(w)


def _mod_kernel(c_ref, w_ref, b_ref, o_ref):
    c = c_ref[...]
    s = (c * jax.nn.sigmoid(c)).astype(BF16)
    o_ref[...] = _dot(s, w_ref[...].astype(BF16)) + b_ref[...]


def _modulation(cond, w_ada, b_ada):
    L, D, N = w_ada.shape
    tn = _pick(N, (1024, 512, 256, 128))
    return pl.pallas_call(
        _mod_kernel,
        grid=(L, N // tn),
        in_specs=[
            pl.BlockSpec((MOD_ROWS, D), lambda l, j: (0, 0)),
            pl.BlockSpec((None, D, tn), lambda l, j: (l, 0, j)),
            pl.BlockSpec((None, 1, tn), lambda l, j: (l, 0, j)),
        ],
        out_specs=pl.BlockSpec((None, MOD_ROWS, tn), lambda l, j: (l, 0, j)),
        out_shape=jax.ShapeDtypeStruct((L, MOD_ROWS, N), F32),
        compiler_params=_params("arbitrary", "arbitrary"),
        name="modulation",
    )(cond, w_ada, b_ada)


def _inproj_kernel(x_ref, sh_ref, sc_ref, g_ref, w_ref, z_ref):
    h = _rms(x_ref[...]) * g_ref[...] * (1.0 + sc_ref[...]) + sh_ref[...]
    z_ref[...] = _dot(h.astype(BF16), w_ref[...])


def _mod_spec(mods, layer, mod_row, col):
    D = mods.shape[3] // N_MOD
    return pl.BlockSpec((None, None, 1, D), lambda *g: (layer, mod_row(g[-1]), 0, col))


def _in_projection(x, mods, mod_row, g, w, layer, tm):
    T, D = x.shape
    N = w.shape[2]
    tn = _col_tile(N, MAX_WEIGHT_COLS)
    return pl.pallas_call(
        _inproj_kernel,
        grid=(N // tn, T // tm),
        in_specs=[
            pl.BlockSpec((tm, D), lambda j, i: (i, 0)),
            _mod_spec(mods, layer, mod_row, 0),
            _mod_spec(mods, layer, mod_row, 1),
            _of_layer(g, layer),
            pl.BlockSpec((None, D, tn), lambda j, i: (layer, 0, j)),
        ],
        out_specs=pl.BlockSpec((tm, tn), lambda j, i: (i, j)),
        out_shape=jax.ShapeDtypeStruct((T, N), F32),
        compiler_params=_params("arbitrary", "arbitrary"),
        name="in_projection",
    )(x, mods, mods, g, w)


def _attn_head(q, keys, values, biases):
    s = [_dot_nt(q, k) if b is None else _dot_nt(q, k) + b for k, b in zip(keys, biases)]
    yield
    m = functools.reduce(jnp.maximum, [jnp.max(x, axis=-1, keepdims=True) for x in s])
    p = [jnp.exp(x - m) for x in s]
    l = sum(jnp.sum(x, axis=-1, keepdims=True) for x in p)
    o = sum(_dot(x.astype(BF16), v) for x, v in zip(p, values))
    yield
    return o / l


def _attn_ctx_kernel(q_ref, k_ref, v_ref, o_ref, *, heads):
    scale = HEAD_DIM ** -0.5
    gens = []
    for h in range(heads):
        sl = slice(h * HEAD_DIM, (h + 1) * HEAD_DIM)
        q = (q_ref[:, sl] * scale).astype(BF16)
        gens.append(_attn_head(q, [k_ref[:, sl].astype(BF16)], [v_ref[:, sl].astype(BF16)], [None]))
    o_ref[...] = jnp.concatenate(_lock_step(gens), axis=-1)


def _attention_context(z, seq, width):
    T = z.shape[0]
    col = lambda j: pl.BlockSpec((seq, width), lambda b: (b, j))
    return pl.pallas_call(
        functools.partial(_attn_ctx_kernel, heads=width // HEAD_DIM),
        grid=(T // seq,),
        in_specs=[col(0), col(1), col(2)],
        out_specs=col(0),
        out_shape=jax.ShapeDtypeStruct((T, width), F32),
        compiler_params=_params("arbitrary"),
        name="attention_context",
    )(z, z, z)


def _attn_lat_kernel(q_ref, k_ref, v_ref, kc_ref, vc_ref, bias_ref, o_ref, *, heads, rows, kr):
    scale = HEAD_DIM ** -0.5
    r = pl.program_id(1)
    r0 = jnp.clip(r - kr // 2, 0, rows - kr)
    start = pl.multiple_of(r0 * GRID_W, GRID_W)
    k_win = k_ref[pl.ds(start, kr * GRID_W), :]
    v_win = v_ref[pl.ds(start, kr * GRID_W), :]
    gens = []
    for h in range(heads):
        sl = slice(h * HEAD_DIM, (h + 1) * HEAD_DIM)
        q = (q_ref[:, sl] * scale).astype(BF16)
        gens.append(_attn_head(q, [k_win[:, sl].astype(BF16), kc_ref[:, sl].astype(BF16)],
                               [v_win[:, sl].astype(BF16), vc_ref[:, sl].astype(BF16)], [bias_ref[h], None]))
    o_ref[...] = jnp.concatenate(_lock_step(gens), axis=-1)


def _toeplitz_kernel(rpb_ref, pick_ref, valid_ref, o_ref):
    o_ref[...] = jnp.where(valid_ref[...] > 0.5, _dot_hi(rpb_ref[...], pick_ref[...]), MASK_NEG)


def _window_bias(rpb, kr):
    rpb = rpb.reshape((-1,) + rpb.shape[2:])
    delta = np.arange(kr)[:, None]
    i = np.arange(kr)[None, :]
    row_off = (WIN_ROWS - 1) - delta + i
    q = np.arange(GRID_W)[:, None]
    kc = np.arange(GRID_W)[None, :]
    c0 = np.clip(q - WIN_COLS // 2, 0, GRID_W - WIN_COLS)
    valid = (kc >= c0) & (kc < c0 + WIN_COLS)
    col_off = kc - q + (WIN_COLS - 1)
    pick = (col_off[None] == np.arange(2 * WIN_COLS - 1)[:, None, None]) & valid[None]
    H, n_ro, n_co = rpb.shape
    toep = pl.pallas_call(
        _toeplitz_kernel,
        out_shape=jax.ShapeDtypeStruct((H * n_ro, GRID_W * GRID_W), F32),
        name="window_bias",
    )(rpb.astype(F32).reshape(H * n_ro, n_co), jnp.asarray(pick.reshape(n_co, -1), F32),
      jnp.asarray(valid.reshape(1, -1), F32))
    toep = toep.reshape(H, n_ro, GRID_W, GRID_W)
    per_delta = []
    for dl in range(kr):
        lo = int(row_off[dl, 0])
        win = toep[:, lo:lo + kr]
        per_delta.append(win.transpose(0, 2, 1, 3).reshape(rpb.shape[0], GRID_W, kr * GRID_W))
    return jnp.stack(per_delta, axis=0)


def _attention_latent(z, k_ctx, v_ctx, bias, layer, dec_seq, width):
    B, _, P, _ = k_ctx.shape
    rows = dec_seq // GRID_W
    kr = bias.shape[0]
    heads = width // HEAD_DIM

    def bias_index(b, r):
        return (r - jnp.clip(r - kr // 2, 0, rows - kr), layer, 0, 0)

    q_spec = pl.BlockSpec((GRID_W, width), lambda b, r: (b * rows + r, 0))
    return pl.pallas_call(
        functools.partial(_attn_lat_kernel, heads=heads, rows=rows, kr=kr),
        grid=(B, rows),
        in_specs=[
            q_spec,
            pl.BlockSpec((dec_seq, width), lambda b, r: (b, 1)),
            pl.BlockSpec((dec_seq, width), lambda b, r: (b, 2)),
            pl.BlockSpec((None, None, P, width), lambda b, r: (b, layer, 0, 0)),
            pl.BlockSpec((None, None, P, width), lambda b, r: (b, layer, 0, 0)),
            pl.BlockSpec((None, heads, GRID_W, kr * GRID_W), bias_index),
        ],
        out_specs=q_spec,
        out_shape=jax.ShapeDtypeStruct((B * dec_seq, width), F32),
        compiler_params=_params("arbitrary", "arbitrary"),
        name="attention_latent",
    )(z, z, z, k_ctx, v_ctx, bias)


def _sgu_kernel(u_ref, v_ref, gain_ref, w_ref, b_ref, o_ref, *, groups, chunks):
    for c in range(chunks):
        rows = slice(c * SG_CHUNK, (c + 1) * SG_CHUNK)
        for g in range(groups):
            sl = slice(g * SG_GROUP_DIM, (g + 1) * SG_GROUP_DIM)
            vn = _rms(v_ref[rows, sl]) * gain_ref[:, sl]
            mixed = _dot(w_ref[g].astype(BF16), vn.astype(BF16)) + b_ref[:, g:g + 1]
            o_ref[rows, sl] = u_ref[rows, sl] * mixed


def _spatial_gating(z, col_u, gain, w_s, b_t, layer):
    T = z.shape[0]
    G = w_s.shape[1]
    W = G * SG_GROUP_DIM
    tm = _pick(T, (512, 256, 128))
    return pl.pallas_call(
        functools.partial(_sgu_kernel, groups=G, chunks=tm // SG_CHUNK),
        grid=(T // tm,),
        in_specs=[pl.BlockSpec((tm, W), lambda i: (i, col_u)),
                  pl.BlockSpec((tm, W), lambda i: (i, col_u + 1)),
                  _of_layer(gain, layer), _of_layer(w_s, layer), _of_layer(b_t, layer)],
        out_specs=pl.BlockSpec((tm, W), lambda i: (i, 0)),
        out_shape=jax.ShapeDtypeStruct((T, W), F32),
        compiler_params=_params("arbitrary"),
        name="spatial_gating",
    )(z, z, gain, w_s, b_t)


def _softplus(y):
    return jnp.maximum(y, 0.0) + jnp.log(1.0 + jnp.exp(-jnp.abs(y)))


def _rwkv_pre_kernel(r_ref, k_ref, v_ref, lora_ref, w0_ref, w2_ref, a0_ref, a2_ref, kkw_ref, ka_ref,
                     bd_ref, tri_ref,
                     at_f, rt_f, bh_f, kh_f, wc_f, at_b, rt_b, bh_b, kh_b, wc_b, v_o, *, pairs, chunks, lora_w, lora_a):
    C = RW_CHUNK
    r = r_ref[...]
    k = k_ref[...]
    bd = bd_ref[...]
    kk = k * kkw_ref[...]
    kk2 = kk * kk
    ssq = jnp.concatenate([_dot_wide_lhs(kk2[:, p * LANES:(p + 1) * LANES], bd) for p in range(pairs)], axis=-1)
    kk = kk * lax.rsqrt(ssq + KK_EPS)
    tw = jnp.tanh(lora_ref[:, :lora_w]).astype(BF16)
    xa = lora_ref[:, lora_w:lora_w + lora_a].astype(BF16)
    v_bf = v_ref[...].astype(BF16)
    for p in range(pairs):
        v_o[p] = v_bf[:, p * LANES:(p + 1) * LANES]
    outs = ((at_f, rt_f, bh_f, kh_f, wc_f), (at_b, rt_b, bh_b, kh_b, wc_b))
    for d in range(2):
        at_o, rt_o, bh_o, kh_o, wc_o = outs[d]
        w_log = -_softplus(-(w0_ref[d:d + 1, :] + _dot(tw, w2_ref[d].astype(BF16)))) - 0.5
        logw = -jnp.exp(w_log)
        a_rate = jax.nn.sigmoid(a0_ref[d:d + 1, :] + _dot(xa, a2_ref[d].astype(BF16)))
        k_d = k * (1.0 + (a_rate - 1.0) * ka_ref[...])
        b = kk * a_rate
        tri = tri_ref[d]
        tot_row = C - 1 if d == 0 else 0
        for c in range(chunks):
            rows = slice(c * C, (c + 1) * C)
            lw = logw[rows]
            cum = _dot_wide_rhs(tri, lw)
            tot = cum[tot_row:tot_row + 1, :]
            e_dn = jnp.exp(tot - cum)
            at = -kk[rows] * jnp.exp(cum - lw - tot)
            rt = r[rows] * jnp.exp(cum - tot)
            bh = b[rows] * e_dn
            kh = k_d[rows] * e_dn
            wc = jnp.exp(tot)
            for p in range(pairs):
                sl = slice(p * LANES, (p + 1) * LANES)
                at_o[p, rows, :] = at[:, sl].astype(BF16)
                rt_o[p, rows, :] = rt[:, sl].astype(BF16)
                bh_o[p, rows, :] = bh[:, sl].astype(BF16)
                kh_o[p, rows, :] = kh[:, sl].astype(BF16)
                wc_o[c, p] = wc[:, sl]


def _rwkv_cols(z_cols, width):
    bw = math.gcd(z_cols, width)
    assert bw % LANES == 0
    return bw, z_cols // bw, (z_cols + width) // bw, (z_cols + 2 * width) // bw


def _rwkv_prepare(z, r_off, lora_blk, lora_dims, w0, w2, a0, a2, kkw, ka, bd, tri, layer):
    T = z.shape[0]
    RW = w0.shape[-1]
    lora_w, lora_a, lora_g = lora_dims
    lora_n = lora_w + lora_a + lora_g
    bw, cr, ck, cv = _rwkv_cols(r_off, RW)
    pairs = bw // LANES
    C = RW_CHUNK
    tm = _pick(T, (256, 128, 64))
    tok = lambda c0: pl.BlockSpec((tm, bw), lambda i, h: (i, c0 + h))
    par2 = lambda a: _of_layer(a, layer, (a.shape[1], bw), lambda i, h: (0, h))
    par3 = lambda a: _of_layer(a, layer, (a.shape[1], a.shape[2], bw), lambda i, h: (0, 0, h))
    packed = pl.BlockSpec((pairs, tm, LANES), lambda i, h: (h, i, 0))
    wc_spec = pl.BlockSpec((tm // C, pairs, 1, LANES), lambda i, h: (i, h, 0, 0))
    packed_shape = jax.ShapeDtypeStruct((RW // LANES, T, LANES), BF16)
    wc_shape = jax.ShapeDtypeStruct((T // C, RW // LANES, 1, LANES), F32)
    return pl.pallas_call(
        functools.partial(_rwkv_pre_kernel, pairs=pairs, chunks=tm // C, lora_w=lora_w, lora_a=lora_a),
        grid=(T // tm, RW // bw),
        in_specs=[tok(cr), tok(ck), tok(cv),
                  pl.BlockSpec((tm, lora_n), lambda i, h: (i, lora_blk)),
                  par2(w0), par3(w2), par2(a0), par3(a2), par2(kkw), par2(ka), _full(bd), _full(tri)],
        out_specs=[packed, packed, packed, packed, wc_spec, packed, packed, packed, packed, wc_spec, packed],
        out_shape=[packed_shape] * 4 + [wc_shape] + [packed_shape] * 4 + [wc_shape] + [packed_shape],
        compiler_params=_params("arbitrary", "arbitrary"),
        name="rwkv_prepare",
    )(z, z, z, z, w0, w2, a0, a2, kkw, ka, bd, tri)


def _rwkv_pair_chunk(at, rt, bh, kh, v, hs, masks):
    C = RW_CHUNK
    strict, incl, eye, same_head, lane_lo = masks
    zero = jnp.zeros_like(at)
    stack = lambda x: jnp.concatenate([jnp.where(lane_lo, x, zero), jnp.where(lane_lo, zero, x)], axis=0)
    twice = lambda x: jnp.concatenate([x, x], axis=0)
    atm, rtm, vm = stack(at), stack(rt), stack(v)
    bk = jnp.concatenate([twice(bh), twice(kh)], axis=0)
    a = _dot_nt(jnp.concatenate([atm, rtm], axis=0), bk)
    yield
    l = jnp.where(strict, a[:2 * C, :2 * C], 0.0)
    a_ak = jnp.where(strict, a[:2 * C, 2 * C:], 0.0).astype(BF16)
    a_rb = jnp.where(incl, a[2 * C:, :2 * C], 0.0).astype(BF16)
    a_rk = jnp.where(incl, a[2 * C:, 2 * C:], 0.0).astype(BF16)
    hs_b = hs.astype(BF16)
    x2 = _dot(jnp.concatenate([atm, a_ak], axis=1), jnp.concatenate([hs_b, vm], axis=0)).astype(BF16)
    t = eye + l
    pw = l.astype(BF16)
    pw = _dot(pw, pw).astype(BF16)
    yield
    for _ in range(int(math.log2(C)) - 2):
        both = _dot(jnp.concatenate([t.astype(BF16), pw], axis=0), pw)
        yield
        t = t + both[:2 * C]
        pw = both[2 * C:].astype(BF16)
    t = t + _dot(t.astype(BF16), pw)
    yield
    u = _dot(t.astype(BF16), x2).astype(BF16)
    yield
    o = _dot(jnp.concatenate([rtm, a_rb, a_rk], axis=1), jnp.concatenate([hs_b, u, vm], axis=0))
    h_new = hs + jnp.where(same_head, _dot_tn(bk, jnp.concatenate([u, vm], axis=0)), 0.0)
    return o[:C] + o[C:], h_new


def _rwkv_core_kernel(at_f, rt_f, bh_f, kh_f, v_f, wc_f, at_b, rt_b, bh_b, kh_b, v_b, wc_b, h0_ref,
                      of_ref, ob_ref, hs_ref, *, pairs, group):
    C = RW_CHUNK

    @pl.when(pl.program_id(1) == 0)
    def _():
        hs_ref[...] = h0_ref[...]

    row = _iota((2 * C, 2 * C), 0)
    col = _iota((2 * C, 2 * C), 1)
    same = (row // C) == (col // C)
    rt_, ct_ = row % C, col % C
    eye = jnp.where(row == col, 1.0, 0.0).astype(F32)
    lrow = _iota((LANES, LANES), 0)
    lcol = _iota((LANES, LANES), 1)
    same_head = (lrow // HEAD_DIM) == (lcol // HEAD_DIM)
    lane_lo = _iota((C, LANES), 1) < HEAD_DIM
    masks = ((same & (ct_ < rt_), same & (ct_ <= rt_), eye, same_head, lane_lo),
             (same & (ct_ > rt_), same & (ct_ >= rt_), eye, same_head, lane_lo))
    dirs = ((at_f, rt_f, bh_f, kh_f, v_f, wc_f, of_ref), (at_b, rt_b, bh_b, kh_b, v_b, wc_b, ob_ref))

    for g0 in range(0, pairs, group):
        jobs = [(d, p) for p in range(g0, g0 + group) for d in range(2)]
        gens = []
        for d, p in jobs:
            at_r, rt_r, bh_r, kh_r, v_r, wc_r, _ = dirs[d]
            wcol = jnp.sum(jnp.where(lrow == lcol, jnp.broadcast_to(wc_r[p], (LANES, LANES)), 0.0),
                           axis=1, keepdims=True)
            gens.append(_rwkv_pair_chunk(at_r[p], rt_r[p], bh_r[p], kh_r[p], v_r[p], hs_ref[d, p] * wcol, masks[d]))
        for (d, p), (o, h_new) in zip(jobs, _lock_step(gens)):
            hs_ref[d, p] = h_new
            dirs[d][-1][p] = o


def _pair_states(h):
    n, _, H, _, _ = h.shape
    hp = h.reshape(n, 2, H // 2, 2, HEAD_DIM, HEAD_DIM)
    z = jnp.zeros_like(hp[:, :, :, 0])
    top = jnp.concatenate([hp[:, :, :, 0], z], axis=-1)
    bot = jnp.concatenate([z, hp[:, :, :, 1]], axis=-1)
    return jnp.concatenate([top, bot], axis=-2)


def _rwkv_core(pre, h0, n_seq, seq_len):
    at_f, rt_f, bh_f, kh_f, wc_f, at_b, rt_b, bh_b, kh_b, wc_b, v = pre
    pairs, T, _ = at_f.shape
    C = RW_CHUNK
    nch = seq_len // C
    fwd = lambda s, c: s * nch + c
    bwd = lambda s, c: s * nch + (nch - 1 - c)
    pk = lambda f: pl.BlockSpec((pairs, C, LANES), lambda s, c: (0, f(s, c), 0))
    wc = lambda f: pl.BlockSpec((None, pairs, 1, LANES), lambda s, c: (f(s, c), 0, 0, 0))
    st = pl.BlockSpec((None, 2, pairs, LANES, LANES), lambda s, c: (s, 0, 0, 0, 0))
    o_shape = jax.ShapeDtypeStruct((pairs, T, LANES), F32)
    return pl.pallas_call(
        functools.partial(_rwkv_core_kernel, pairs=pairs, group=_pick(pairs, (RW_PAIR_GROUP, 4, 2, 1))),
        grid=(n_seq, nch),
        in_specs=[pk(fwd), pk(fwd), pk(fwd), pk(fwd), pk(fwd), wc(fwd),
                  pk(bwd), pk(bwd), pk(bwd), pk(bwd), pk(bwd), wc(bwd), st],
        out_specs=[pk(fwd), pk(bwd), st],
        out_shape=[o_shape, o_shape, jax.ShapeDtypeStruct(h0.shape, F32)],
        compiler_params=_params("arbitrary", "arbitrary"),
        name="rwkv_scan",
    )(at_f, rt_f, bh_f, kh_f, v, wc_f, at_b, rt_b, bh_b, kh_b, v, wc_b, h0)


def _rwkv_post_kernel(of_ref, ob_ref, r_ref, k_ref, v_ref, lora_ref, g2_ref, lnw_ref, lnb_ref, rk_ref, bd_ref,
                      o_ref, *, pairs, lora_g):
    bd = bd_ref[...]
    n_lora = lora_ref.shape[1]
    sg = jax.nn.sigmoid(lora_ref[:, n_lora - lora_g:]).astype(BF16)
    inv = 1.0 / HEAD_DIM
    for p in range(pairs):
        sl = slice(p * LANES, (p + 1) * LANES)
        o = of_ref[p] + ob_ref[p]
        mu = _dot_wide_lhs(o, bd) * inv
        dlt = o - mu
        var = _dot_wide_lhs(dlt * dlt, bd) * inv
        on = dlt * lax.rsqrt(var + GN_EPS) * lnw_ref[:, sl] + lnb_ref[:, sl]
        bonus = _dot_wide_lhs(r_ref[:, sl] * k_ref[:, sl] * rk_ref[:, sl], bd) * v_ref[:, sl]
        gate = _dot(sg, g2_ref[:, sl].astype(BF16))
        o_ref[:, sl] = (on + bonus) * gate


def _rwkv_finish(o_f, o_b, z, r_off, lora_blk, lora_dims, g2, lnw, lnb, rk, bd, layer):
    T = z.shape[0]
    RW = g2.shape[-1]
    lora_n = sum(lora_dims)
    bw, cr, ck, cv = _rwkv_cols(r_off, RW)
    pairs = bw // LANES
    tm = _pick(T, (256, 128, 64))
    tok = lambda c0: pl.BlockSpec((tm, bw), lambda i, h: (i, c0 + h))
    par2 = lambda a: _of_layer(a, layer, (a.shape[1], bw), lambda i, h: (0, h))
    packed = pl.BlockSpec((pairs, tm, LANES), lambda i, h: (h, i, 0))
    return pl.pallas_call(
        functools.partial(_rwkv_post_kernel, pairs=pairs, lora_g=lora_dims[2]),
        grid=(T // tm, RW // bw),
        in_specs=[packed, packed, tok(cr), tok(ck), tok(cv),
                  pl.BlockSpec((tm, lora_n), lambda i, h: (i, lora_blk)),
                  par2(g2), par2(lnw), par2(lnb), par2(rk), _full(bd)],
        out_specs=pl.BlockSpec((tm, bw), lambda i, h: (i, h)),
        out_shape=jax.ShapeDtypeStruct((T, RW), F32),
        compiler_params=_params("arbitrary", "arbitrary"),
        name="rwkv_finish",
    )(o_f, o_b, z, z, z, z, g2, lnw, lnb, rk, bd)


def _outproj_kernel(oa_ref, ob_ref, oc_ref, x_ref, w_ref, g1_ref, sh2_ref, sc2_ref, gpost_ref, gpre_ref, wr_ref,
                    xo_ref, h2_ref, aff_ref):
    cat = jnp.concatenate([oa_ref[...], ob_ref[...], oc_ref[...]], axis=-1).astype(BF16)
    mixed = _dot(cat, w_ref[...])
    x = x_ref[...] + g1_ref[...] * (_rms(mixed) * gpost_ref[...])
    xo_ref[...] = x
    h2 = _rms(x) * gpre_ref[...] * (1.0 + sc2_ref[...]) + sh2_ref[...]
    h2_ref[...] = h2.astype(BF16)
    h_hi, h_lo = _split2(h2)
    w_hi, w_lo = _split2(wr_ref[...])
    logits = _dot(h_hi, w_hi) + _dot(h_lo, w_hi) + _dot(h_hi, w_lo)
    e = jnp.exp(logits - jnp.max(logits, axis=-1, keepdims=True))
    aff_ref[...] = e / jnp.sum(e, axis=-1, keepdims=True)


def _out_projection(o_a, o_b, o_c, x, w_out, layer, mods, mod_row, g_post, g_pre, w_router, tm):
    T, D = x.shape
    E = w_router.shape[2]
    row = lambda a: pl.BlockSpec((tm, a.shape[1]), lambda i: (i, 0))
    mod = lambda col: _mod_spec(mods, layer, mod_row, col)
    return pl.pallas_call(
        _outproj_kernel,
        grid=(T // tm,),
        in_specs=[row(o_a), row(o_b), row(o_c), row(x), _of_layer(w_out, layer),
                  mod(2), mod(3), mod(4), _of_layer(g_post, layer), _of_layer(g_pre, layer),
                  _of_layer(w_router, layer)],
        out_specs=[pl.BlockSpec((tm, D), lambda i: (i, 0)),
                   pl.BlockSpec((tm, D), lambda i: (i, 0)),
                   pl.BlockSpec((tm, E), lambda i: (i, 0))],
        out_shape=[jax.ShapeDtypeStruct((T, D), F32), jax.ShapeDtypeStruct((T, D), BF16),
                   jax.ShapeDtypeStruct((T, E), F32)],
        compiler_params=_params("arbitrary"),
        name="out_projection",
    )(o_a, o_b, o_c, x, w_out, mods, mods, mods, g_post, g_pre, w_router)


def _select_kernel(a_ref, ut_ref, pos_ref, cnt_ref, *, cap):
    a = a_ref[...]
    R, T = a.shape
    lo = jnp.zeros((R, 1), jnp.int32)
    for bit in range(30, -1, -1):
        cand = lo | (1 << bit)
        cnt = jnp.sum(jnp.where(a >= lax.bitcast_convert_type(cand, F32), 1.0, 0.0), axis=1, keepdims=True)
        lo = jnp.where(cnt >= cap, cand, lo)
    thr = lax.bitcast_convert_type(lo, F32)
    gt = a > thr
    eq = a == thr
    need = cap - jnp.sum(jnp.where(gt, 1.0, 0.0), axis=1, keepdims=True)
    ut = ut_ref[...]
    nblk = T // LANES
    tie_carry = jnp.zeros((R, 1), F32)
    pos_carry = jnp.zeros((R, 1), F32)
    for n in range(nblk):
        sl = slice(n * LANES, (n + 1) * LANES)
        eq_b = jnp.where(eq[:, sl], 1.0, 0.0)
        tie_rank = _dot(eq_b.astype(BF16), ut) + tie_carry
        tie_carry = tie_carry + jnp.sum(eq_b, axis=1, keepdims=True)
        sel = jnp.where(gt[:, sl], 1.0, jnp.where(tie_rank < need, eq_b, 0.0))
        pos = _dot(sel.astype(BF16), ut) + pos_carry
        cnt_ref[:, n:n + 1] = pos_carry.astype(jnp.int32)
        pos_carry = pos_carry + jnp.sum(sel, axis=1, keepdims=True)
        pos_ref[:, sl] = jnp.where(sel > 0.5, pos, -1.0).astype(jnp.int32)


def _select(aff_rows, cap, ut):
    R, T = aff_rows.shape
    tr = _pick(R, (128, 64, 32, 16, 8))
    return pl.pallas_call(
        functools.partial(_select_kernel, cap=cap),
        grid=(R // tr,),
        in_specs=[pl.BlockSpec((tr, T), lambda i: (i, 0)), pl.BlockSpec((LANES, LANES), lambda i: (0, 0))],
        out_specs=[pl.BlockSpec((tr, T), lambda i: (i, 0)), pl.BlockSpec((tr, T // LANES), lambda i: (i, 0))],
        out_shape=[jax.ShapeDtypeStruct((R, T), jnp.int32), jax.ShapeDtypeStruct((R, T // LANES), jnp.int32)],
        compiler_params=_params("arbitrary"),
        name="expert_select",
    )(aff_rows, ut)


def _route(aff, n_sets, set_len, cap, ut):
    E = aff.shape[1]
    rows = aff.reshape(n_sets, set_len, E).transpose(0, 2, 1).reshape(n_sets * E, set_len)
    pos, cnt = _select(rows, cap, ut)
    pos = pos.reshape(n_sets, E, set_len)
    return pos, pos.transpose(0, 2, 1).reshape(n_sets * set_len, E), cnt, rows


def _slot_tables(experts, cap):
    pair_expert = np.arange(experts * cap) // cap
    rep = (pair_expert[:, None] == np.arange(experts)[None, :]).astype(np.float32)
    slot = (np.arange(experts * cap) % cap).astype(np.float32)
    return jnp.asarray(rep, BF16), jnp.asarray(rep.T, BF16), jnp.asarray(slot[:, None]), jnp.asarray(slot[None, :])


def _gather_ctx_kernel(h_ref, pos_ref, aff_ref, rep_ref, slot_ref, xs_ref, gs_ref, *, experts, cap):
    rep = rep_ref[...]
    pos_rep = _dot(rep, pos_ref[...].astype(F32).astype(BF16))
    hit = pos_rep == slot_ref[...]
    xs = _dot(jnp.where(hit, 1.0, 0.0).astype(BF16), h_ref[...]).astype(BF16)
    xs_ref[...] = xs.reshape(experts, cap, xs.shape[1])
    gs = jnp.sum(jnp.where(hit, _dot_wide_rhs(rep, aff_ref[...]), 0.0), axis=1, keepdims=True)
    gs_ref[...] = gs.reshape(experts, cap, 1)


def _gather_lat_kernel(cnt_ref, h_ref, pos_ref, aff_ref, xs_ref, gs_ref, acc_scr, gacc_scr, *, cap, experts):
    row = pl.program_id(0) * experts + pl.program_id(1)
    n_blocks = pos_ref.shape[0]
    step = GATHER_TOKENS // LANES
    for i in range(cap // SLOT_TILE):
        first_slot = i * SLOT_TILE
        starts = [cnt_ref[row, n * step] for n in range(n_blocks)]
        n_lo = sum(jnp.where(s <= first_slot, 1, 0) for s in starts) - 1
        n_hi = sum(jnp.where(s < first_slot + SLOT_TILE, 1, 0) for s in starts) - 1
        acc_scr[...] = jnp.zeros(acc_scr.shape, F32)
        gacc_scr[...] = jnp.zeros(gacc_scr.shape, F32)

        def body(n, carry, first_slot=first_slot):
            toks = h_ref[pl.ds(pl.multiple_of(n * GATHER_TOKENS, GATHER_TOKENS), GATHER_TOKENS), :]
            hit = _iota((SLOT_TILE, GATHER_TOKENS), 0) + first_slot == pos_ref[n]
            acc_scr[...] += _dot(jnp.where(hit, 1.0, 0.0).astype(BF16), toks)
            gacc_scr[...] += jnp.sum(jnp.where(hit, aff_ref[n], 0.0), axis=1, keepdims=True)
            return carry

        lax.fori_loop(n_lo, n_hi + 1, body, 0)
        xs_ref[first_slot:first_slot + SLOT_TILE, :] = acc_scr[...].astype(BF16)
        gs_ref[first_slot:first_slot + SLOT_TILE, :] = gacc_scr[...]


def _expert_up_kernel(xc_ref, xl_ref, wg_ref, wu_ref, hid_ref):
    wg = wg_ref[...].astype(BF16)
    wu = wu_ref[...].astype(BF16)
    rc = xc_ref.shape[0]
    for ref, rows in ((xc_ref, slice(0, rc)), (xl_ref, slice(rc, hid_ref.shape[0]))):
        x = ref[...]
        g = _dot(x, wg)
        hid_ref[rows, :] = (g * jax.nn.sigmoid(g) * _dot(x, wu)).astype(BF16)


def _expert_down_kernel(hid_ref, gsc_ref, gsl_ref, wd_ref, y_ref):
    wd = wd_ref[...].astype(BF16)
    rc = gsc_ref.shape[0]
    for gs_ref, rows in ((gsc_ref, slice(0, rc)), (gsl_ref, slice(rc, y_ref.shape[0]))):
        y_ref[rows, :] = (_dot(hid_ref[rows, :], wd) * gs_ref[...]).astype(BF16)


def _slot_one_hot(pos_col, cap):
    tm = pos_col.shape[0]
    return jnp.where(_iota((tm, cap), 1).astype(F32) == pos_col, 1.0, 0.0).astype(BF16)


def _col_chunks(width):
    cw = _pick(width, (2 * LANES, LANES))
    return [slice(n * cw, (n + 1) * cw) for n in range(width // cw)]


def _ffn_residual(x, ffn, g2, g_post):
    return x + g2 * (_rms(ffn) * g_post)


def _combine_ctx_kernel(x_ref, y_ref, pos_ref, rep_ref, slot_ref, g2_ref, gpost_ref, o_ref, *, experts, cap):
    pos_exp = _dot(pos_ref[...].astype(F32).astype(BF16), rep_ref[...])
    one_hot = jnp.where(pos_exp == slot_ref[...], 1.0, 0.0).astype(BF16)
    y = y_ref[...].reshape(experts * cap, y_ref.shape[2])
    o_ref[...] = _ffn_residual(x_ref[...], _dot(one_hot, y), g2_ref[...], gpost_ref[...])


def _combine_lat_kernel(x_ref, y_ref, pos_ref, g2_ref, gpost_ref, o_ref, acc_scr, *, cap, group):
    eg = pl.program_id(2)

    @pl.when(eg == 0)
    def _():
        acc_scr[...] = jnp.zeros(acc_scr.shape, F32)

    lane = _iota(pos_ref.shape, 1)
    pos = pos_ref[...].astype(F32)
    one_hots = []
    for k in range(group):
        pos_col = jnp.sum(jnp.where(lane == eg * group + k, pos, 0.0), axis=1, keepdims=True)
        one_hots.append(_slot_one_hot(pos_col, cap))
    for cs in _col_chunks(acc_scr.shape[1]):
        acc_scr[:, cs] += sum(_dot(oh, y_ref[k, :, cs]) for k, oh in enumerate(one_hots))

    @pl.when(eg == pl.num_programs(2) - 1)
    def _():
        o_ref[...] = _ffn_residual(x_ref[...], acc_scr[...], g2_ref[...], gpost_ref[...])


def _expert_ffn(x_c, h2_c, aff_c, x_l, h2_l, aff_l, mods, g_post, w_gate, w_up, w_down, layer, ut, dims):
    batch, seq, dec_batch, dec_seq = dims
    D = x_c.shape[1]
    _, E, _, F = w_gate.shape
    cap_c = CAPACITY_FACTOR * seq // E
    cap_l = CAPACITY_FACTOR * dec_seq // E
    rows_c = batch * cap_c
    rows_l = dec_batch * cap_l
    rows = rows_c + rows_l
    assert rows_c % cap_l == 0
    lat_row_blk = rows_c // cap_l

    assert cap_c <= 256 and cap_l % SLOT_TILE == 0 and dec_seq % GATHER_TOKENS == 0 and E % COMBINE_GROUP == 0
    pos_c, post_c, _, affr_c = _route(aff_c, batch, seq, cap_c, ut)
    pos_l, post_l, cnt_l, affr_l = _route(aff_l, dec_batch, dec_seq, cap_l, ut)
    rep, rep_t, slot_col, slot_row = _slot_tables(E, cap_c)

    set_rows = pl.BlockSpec((None, E, seq), lambda b: (b, 0, 0))
    xs_c, gs_c = pl.pallas_call(
        functools.partial(_gather_ctx_kernel, experts=E, cap=cap_c),
        grid=(batch,),
        in_specs=[pl.BlockSpec((seq, D), lambda b: (b, 0)), set_rows, set_rows, _full(rep), _full(slot_col)],
        out_specs=[pl.BlockSpec((E, cap_c, D), lambda b: (0, b, 0)), pl.BlockSpec((E, cap_c, 1), lambda b: (0, b, 0))],
        out_shape=[jax.ShapeDtypeStruct((E, rows_c, D), BF16), jax.ShapeDtypeStruct((E, rows_c, 1), F32)],
        compiler_params=_params("arbitrary"),
        name="expert_gather_context",
    )(h2_c, pos_c, affr_c.reshape(batch, E, seq), rep, slot_col)
    n_tok_blk = dec_seq // GATHER_TOKENS
    blocked = pl.BlockSpec((None, None, n_tok_blk, 1, GATHER_TOKENS), lambda b, e, cnt: (b, e, 0, 0, 0))
    xs_l, gs_l = pl.pallas_call(
        functools.partial(_gather_lat_kernel, cap=cap_l, experts=E),
        grid_spec=pltpu.PrefetchScalarGridSpec(
            num_scalar_prefetch=1,
            grid=(dec_batch, E),
            in_specs=[pl.BlockSpec((dec_seq, D), lambda b, e, cnt: (b, 0)), blocked, blocked],
            out_specs=[pl.BlockSpec((None, cap_l, D), lambda b, e, cnt: (e, b, 0)),
                       pl.BlockSpec((None, cap_l, 1), lambda b, e, cnt: (e, b, 0))],
            scratch_shapes=[pltpu.VMEM((SLOT_TILE, D), F32), pltpu.VMEM((SLOT_TILE, 1), F32)]),
        out_shape=[jax.ShapeDtypeStruct((E, rows_l, D), BF16), jax.ShapeDtypeStruct((E, rows_l, 1), F32)],
        compiler_params=_params("arbitrary", "arbitrary"),
        name="expert_gather_latent",
    )(cnt_l, h2_l, pos_l.reshape(dec_batch, E, n_tok_blk, 1, GATHER_TOKENS),
      affr_l.reshape(dec_batch, E, n_tok_blk, 1, GATHER_TOKENS))

    tf = _pick(F, (512, 256, 128))
    hid = pl.pallas_call(
        _expert_up_kernel,
        grid=(E, F // tf),
        in_specs=[pl.BlockSpec((None, rows_c, D), lambda e, n: (e, 0, 0)),
                  pl.BlockSpec((None, rows_l, D), lambda e, n: (e, 0, 0)),
                  pl.BlockSpec((None, None, D, tf), lambda e, n: (layer, e, 0, n)),
                  pl.BlockSpec((None, None, D, tf), lambda e, n: (layer, e, 0, n))],
        out_specs=pl.BlockSpec((None, rows, tf), lambda e, n: (e, 0, n)),
        out_shape=jax.ShapeDtypeStruct((E, rows, F), BF16),
        compiler_params=_params("arbitrary", "arbitrary"),
        name="expert_up",
    )(xs_c, xs_l, w_gate, w_up)
    td = _pick(D, (512, 256, 128))
    y = pl.pallas_call(
        _expert_down_kernel,
        grid=(E, D // td),
        in_specs=[pl.BlockSpec((None, rows, F), lambda e, n: (e, 0, 0)),
                  pl.BlockSpec((None, rows_c, 1), lambda e, n: (e, 0, 0)),
                  pl.BlockSpec((None, rows_l, 1), lambda e, n: (e, 0, 0)),
                  pl.BlockSpec((None, None, F, td), lambda e, n: (layer, e, 0, n))],
        out_specs=pl.BlockSpec((None, rows, td), lambda e, n: (e, 0, n)),
        out_shape=jax.ShapeDtypeStruct((E, rows, D), BF16),
        compiler_params=_params("arbitrary", "arbitrary"),
        name="expert_down",
    )(hid, gs_c, gs_l, w_down)

    out_c = pl.pallas_call(
        functools.partial(_combine_ctx_kernel, experts=E, cap=cap_c),
        grid=(batch,),
        in_specs=[pl.BlockSpec((seq, D), lambda b: (b, 0)),
                  pl.BlockSpec((E, cap_c, D), lambda b: (0, b, 0)),
                  pl.BlockSpec((seq, E), lambda b: (b, 0)),
                  _full(rep_t), _full(slot_row),
                  pl.BlockSpec((None, None, 1, D), lambda b: (layer, 0, 0, N_MOD - 1)),
                  _of_layer(g_post, layer)],
        out_specs=pl.BlockSpec((seq, D), lambda b: (b, 0)),
        out_shape=jax.ShapeDtypeStruct(x_c.shape, F32),
        compiler_params=_params("arbitrary"),
        name="expert_combine_context",
    )(x_c, y, post_c, rep_t, slot_row, mods, g_post)
    tm = _pick(dec_seq, (512, 256, 128))
    nt = dec_seq // tm
    tok = lambda b, i, e: (b * nt + i, 0)
    out_l = pl.pallas_call(
        functools.partial(_combine_lat_kernel, cap=cap_l, group=COMBINE_GROUP),
        grid=(dec_batch, nt, E // COMBINE_GROUP),
        in_specs=[pl.BlockSpec((tm, D), tok),
                  pl.BlockSpec((COMBINE_GROUP, cap_l, D), lambda b, i, e: (e, lat_row_blk + b, 0)),
                  pl.BlockSpec((tm, E), tok),
                  pl.BlockSpec((None, None, 1, D), lambda b, i, e: (layer, 1 + b, 0, N_MOD - 1)),
                  _of_layer(g_post, layer)],
        out_specs=pl.BlockSpec((tm, D), tok),
        out_shape=jax.ShapeDtypeStruct(x_l.shape, F32),
        scratch_shapes=[pltpu.VMEM((tm, D), F32)],
        compiler_params=_params("arbitrary", "arbitrary", "arbitrary"),
        name="expert_combine_latent",
    )(x_l, y, post_l, mods, g_post)
    return out_c, out_l


def _cache_pack_kernel(*refs, depth):
    k_out, v_out = refs[2 * depth:]
    for li in range(depth):
        @pl.when(pl.program_id(0) == li)
        def _(li=li):
            k_out[...] = refs[2 * li][...]
            v_out[...] = refs[2 * li + 1][...]


def _cache_pack(zs, batch, seq, width):
    depth = len(zs)
    in_specs, args = [], []
    for z in zs:
        in_specs += [pl.BlockSpec((seq, width), lambda l, b: (b, 1)), pl.BlockSpec((seq, width), lambda l, b: (b, 2))]
        args += [z, z]
    out_spec = pl.BlockSpec((None, None, seq, width), lambda l, b: (b, l, 0, 0))
    shape = jax.ShapeDtypeStruct((batch, depth, seq, width), F32)
    return pl.pallas_call(
        functools.partial(_cache_pack_kernel, depth=depth),
        grid=(depth, batch), in_specs=in_specs, out_specs=[out_spec, out_spec], out_shape=[shape, shape],
        compiler_params=_params("arbitrary", "arbitrary"), name="cache_pack",
    )(*args)


def _state_pack_kernel(*refs, depth, pairs):
    o_ref = refs[depth]
    for li in range(depth):
        for d in range(2):
            for p in range(pairs):
                flipped = refs[li][d, p].T
                for j in range(LANES // HEAD_DIM):
                    sl = slice(j * HEAD_DIM, (j + 1) * HEAD_DIM)
                    o_ref[li, d, (LANES // HEAD_DIM) * p + j] = flipped[sl, sl]


def _state_pack(states):
    depth = len(states)
    B, _, pairs, _, _ = states[0].shape
    heads = pairs * (LANES // HEAD_DIM)
    spec = pl.BlockSpec((None, 2, pairs, LANES, LANES), lambda b: (b, 0, 0, 0, 0))
    return pl.pallas_call(
        functools.partial(_state_pack_kernel, depth=depth, pairs=pairs),
        grid=(B,), in_specs=[spec] * depth,
        out_specs=pl.BlockSpec((None, depth, 2, heads, HEAD_DIM, HEAD_DIM), lambda b: (b, 0, 0, 0, 0, 0)),
        out_shape=jax.ShapeDtypeStruct((B, depth, 2, heads, HEAD_DIM, HEAD_DIM), F32),
        compiler_params=_params("arbitrary"), name="state_pack",
    )(*states)


def kernel(x_prompt, x_sample, cache_k, cache_v, state_wkv, c, c_ctx, w_ada, b_ada, g_pre_mix, g_post_mix, g_pre_ffn, g_post_ffn, w_in, na_rpb, sg_gain, sg_w, sg_b, rw_w0, rw_w2, rw_a0, rw_a2, rw_g2, rw_kk, rw_ka, rw_rk, rw_ln_w, rw_ln_b, w_out, w_router, w_gate, w_up, w_down):
    batch, seq, D = x_prompt.shape
    dec_batch, dec_seq, _ = x_sample.shape
    depth = w_ada.shape[0]
    past = cache_k.shape[2]
    na_heads = cache_k.shape[3]
    naw = na_heads * HEAD_DIM
    sgw = sg_gain.shape[1]
    rw_heads = state_wkv.shape[3]
    rww = rw_heads * HEAD_DIM
    lora_dims = (rw_w2.shape[2], rw_a2.shape[2], rw_g2.shape[1])
    n_ctx = batch * seq
    n_lat = dec_batch * dec_seq
    assert 1 + dec_batch <= MOD_ROWS and rww % LANES == 0
    assert seq % RW_CHUNK == 0 and dec_seq % RW_CHUNK == 0 and dec_seq % GRID_W == 0
    sg_off = 3 * naw
    rw_off = sg_off + 2 * sgw
    lora_off = rw_off + 3 * rww
    assert sg_off % sgw == 0 and lora_off % sum(lora_dims) == 0
    col_u = sg_off // sgw
    lora_blk = lora_off // sum(lora_dims)

    pair_ones = jnp.asarray(np.kron(np.eye(LANES // HEAD_DIM), np.ones((HEAD_DIM, HEAD_DIM))), BF16)
    tril = np.tril(np.ones((RW_CHUNK, RW_CHUNK)))
    tri = jnp.asarray(np.stack([tril, tril.T]), BF16)
    ut = jnp.asarray(np.triu(np.ones((LANES, LANES)), 1), BF16)
    kr = min(WIN_ROWS, dec_seq // GRID_W)

    cond = jnp.zeros((MOD_ROWS, D), F32).at[0].set(c_ctx).at[1:1 + dec_batch].set(c)
    mods = _modulation(cond, w_ada, b_ada.reshape(depth, 1, N_MOD * D)).reshape(depth, MOD_ROWS, 1, N_MOD * D)
    w_in_b = _to_bf16(w_in)
    w_out_b = _to_bf16(w_out)

    vec = lambda a: a.reshape(depth, 1, a.shape[-1])
    g_pre_mix, g_post_mix, g_pre_ffn, g_post_ffn = vec(g_pre_mix), vec(g_post_mix), vec(g_pre_ffn), vec(g_post_ffn)
    sg_args = (vec(sg_gain), sg_w, jnp.swapaxes(sg_b, 1, 2))
    rw_args = (rw_w0, rw_w2, rw_a0, rw_a2, vec(rw_kk), vec(rw_ka), pair_ones, tri)
    fin_args = (rw_g2, vec(rw_ln_w), vec(rw_ln_b), rw_rk.reshape(depth, 1, rww), pair_ones)
    bias = _window_bias(na_rpb, kr)
    k_past = cache_k.reshape(dec_batch, depth, past, naw)
    v_past = cache_v.reshape(dec_batch, depth, past, naw)
    h0_lat = _pair_states(jnp.swapaxes(state_wkv, -1, -2).reshape((dec_batch * depth,) + state_wkv.shape[2:]))
    h0_lat = h0_lat.reshape((dec_batch, depth) + h0_lat.shape[1:])
    h0_ctx = jnp.zeros((batch, 2, rww // LANES, LANES, LANES), F32)

    tm_c = _pick(n_ctx, (512, 256, 128))
    tm_l = _pick(dec_seq, (512, 256, 128))
    ctx_row = lambda i: 0
    lat_row = lambda i: 1 + (i * tm_l) // dec_seq

    x_c = x_prompt.reshape(n_ctx, D)
    x_l = x_sample.reshape(n_lat, D)
    zs, states = [], []
    for l in range(depth):
        def mixers(x, tm, mod_row, attend, h0, n_seq, seq_len):
            z = _in_projection(x, mods, mod_row, g_pre_mix, w_in_b, l, tm)
            o_a = attend(z)
            o_b = _spatial_gating(z, col_u, *sg_args, l)
            pre = _rwkv_prepare(z, rw_off, lora_blk, lora_dims, *rw_args, l)
            o_f, o_bk, s_fin = _rwkv_core(pre, h0, n_seq, seq_len)
            o_c = _rwkv_finish(o_f, o_bk, z, rw_off, lora_blk, lora_dims, *fin_args, l)
            x, h2, aff = _out_projection(o_a, o_b, o_c, x, w_out_b, l, mods, mod_row,
                                         g_post_mix, g_pre_ffn, w_router, tm)
            return z, x, h2, aff, s_fin

        z_c, x_c, h2_c, aff_c, s_ctx = mixers(
            x_c, tm_c, ctx_row, lambda z: _attention_context(z, seq, naw), h0_ctx, batch, seq)
        zs.append(z_c)
        states.append(s_ctx)
        _, x_l, h2_l, aff_l, _ = mixers(
            x_l, tm_l, lat_row, lambda z: _attention_latent(z, k_past, v_past, bias, l, dec_seq, naw),
            h0_lat[:, l], dec_batch, dec_seq)

        x_c, x_l = _expert_ffn(x_c, h2_c, aff_c, x_l, h2_l, aff_l, mods, g_post_ffn,
                               w_gate, w_up, w_down, l, ut, (batch, seq, dec_batch, dec_seq))

    new_k, new_v = _cache_pack(zs, batch, seq, naw)
    cache_shape = (batch, depth, seq, na_heads, HEAD_DIM)
    return (x_c.reshape(batch, seq, D), x_l.reshape(dec_batch, dec_seq, D),
            new_k.reshape(cache_shape), new_v.reshape(cache_shape), _state_pack(states))
```

```python
import functools
import math

import numpy as np
import jax
import jax.numpy as jnp
from jax import lax
from jax.experimental import pallas as pl
from jax.experimental.pallas import tpu as pltpu

F32 = jnp.float32
BF16 = jnp.bfloat16
HIGHEST = lax.Precision.HIGHEST

HEAD_DIM = 64
LANES = 128
GRID_W = 64
WIN_ROWS = 8
WIN_COLS = 16
SG_CHUNK = 128
SG_GROUP_DIM = 128
N_MOD = 6
CAPACITY_FACTOR = 2
NORM_EPS = 1e-6
GN_EPS = 64e-5
KK_EPS = 1e-12
RW_CHUNK = 64
RW_PAIR_GROUP = 8
MASK_NEG = -1e30
MOD_ROWS = 8
VMEM_LIMIT_BYTES = 56 * 1024 * 1024
MAX_WEIGHT_COLS = 3072
GATHER_TOKENS = 256
SLOT_TILE = 128
COMBINE_GROUP = 4


def _pick(n, prefs):
    for p in prefs:
        if n % p == 0:
            return p
    raise ValueError(f"no tile in {prefs} divides {n}")


def _col_tile(n, cap):
    tiles = [t for t in range(LANES, min(n, cap) + 1, LANES) if n % t == 0]
    return tiles[-1]


def _params(*sem):
    return pltpu.CompilerParams(dimension_semantics=sem, vmem_limit_bytes=VMEM_LIMIT_BYTES)


def _rms(x):
    return x * lax.rsqrt(jnp.mean(x * x, axis=-1, keepdims=True) + NORM_EPS)


def _dot(a, b):
    return jnp.dot(a, b, preferred_element_type=F32)


def _dot_nt(a, b):
    return lax.dot_general(a, b, (((1,), (1,)), ((), ())), preferred_element_type=F32)


def _dot_tn(a, b):
    return lax.dot_general(a, b, (((0,), (0,)), ((), ())), preferred_element_type=F32)


def _dot_hi(a, b):
    return jnp.dot(a, b, preferred_element_type=F32, precision=HIGHEST)


def _split2(a):
    hi = a.astype(BF16)
    return hi, (a - hi.astype(F32)).astype(BF16)


def _dot_wide_lhs(a, b):
    hi, lo = _split2(a)
    return _dot(hi, b) + _dot(lo, b)


def _dot_wide_rhs(a, b):
    hi, lo = _split2(b)
    return _dot(a, hi) + _dot(a, lo)


def _lock_step(gens):
    results = [None] * len(gens)
    live = list(enumerate(gens))
    while live:
        still = []
        for i, g in live:
            try:
                next(g)
                still.append((i, g))
            except StopIteration as done:
                results[i] = done.value
        live = still
    return results


def _iota(shape, axis):
    return lax.broadcasted_iota(jnp.int32, shape, axis)


def _full(a):
    return pl.BlockSpec(a.shape, lambda *_: (0,) * a.ndim)


def _of_layer(a, layer, block=None, index=None):
    shape = tuple(a.shape[1:]) if block is None else tuple(block)
    if index is None:
        return pl.BlockSpec((None,) + shape, lambda *_: (layer,) + (0,) * len(shape))
    return pl.BlockSpec((None,) + shape, lambda *g: (layer,) + tuple(index(*g)))


def _cast_kernel(w_ref, o_ref):
    o_ref[...] = w_ref[...].astype(BF16)


def _to_bf16(w):
    L, R, C = w.shape
    tr = _pick(R, (256, 128, 64, 32, 16))
    spec = pl.BlockSpec((None, tr, C), lambda l, i: (l, i, 0))
    return pl.pallas_call(
        _cast_kernel, grid=(L, R // tr), in_specs=[spec], out_specs=spec,
        out_shape=jax.ShapeDtypeStruct(w.shape, BF16),
        compiler_params=_params("arbitrary", "arbitrary"), name="cast_weights",
    )(w)


def _mod_kernel(c_ref, w_ref, b_ref, o_ref):
    c = c_ref[...]
    s = (c * jax.nn.sigmoid(c)).astype(BF16)
    o_ref[...] = _dot(s, w_ref[...].astype(BF16)) + b_ref[...]


def _modulation(cond, w_ada, b_ada):
    L, D, N = w_ada.shape
    tn = _pick(N, (1024, 512, 256, 128))
    return pl.pallas_call(
        _mod_kernel,
        grid=(L, N // tn),
        in_specs=[
            pl.BlockSpec((MOD_ROWS, D), lambda l, j: (0, 0)),
            pl.BlockSpec((None, D, tn), lambda l, j: (l, 0, j)),
            pl.BlockSpec((None, 1, tn), lambda l, j: (l, 0, j)),
        ],
        out_specs=pl.BlockSpec((None, MOD_ROWS, tn), lambda l, j: (l, 0, j)),
        out_shape=jax.ShapeDtypeStruct((L, MOD_ROWS, N), F32),
        compiler_params=_params("arbitrary", "arbitrary"),
        name="modulation",
    )(cond, w_ada, b_ada)


def _inproj_kernel(x_ref, sh_ref, sc_ref, g_ref, w_ref, z_ref):
    h = _rms(x_ref[...]) * g_ref[...] * (1.0 + sc_ref[...]) + sh_ref[...]
    z_ref[...] = _dot(h.astype(BF16), w_ref[...])


def _mod_spec(mods, layer, mod_row, col):
    D = mods.shape[3] // N_MOD
    return pl.BlockSpec((None, None, 1, D), lambda *g: (layer, mod_row(g[-1]), 0, col))


def _in_projection(x, mods, mod_row, g, w, layer, tm):
    T, D = x.shape
    N = w.shape[2]
    tn = _col_tile(N, MAX_WEIGHT_COLS)
    return pl.pallas_call(
        _inproj_kernel,
        grid=(N // tn, T // tm),
        in_specs=[
            pl.BlockSpec((tm, D), lambda j, i: (i, 0)),
            _mod_spec(mods, layer, mod_row, 0),
            _mod_spec(mods, layer, mod_row, 1),
            _of_layer(g, layer),
            pl.BlockSpec((None, D, tn), lambda j, i: (layer, 0, j)),
        ],
        out_specs=pl.BlockSpec((tm, tn), lambda j, i: (i, j)),
        out_shape=jax.ShapeDtypeStruct((T, N), F32),
        compiler_params=_params("arbitrary", "arbitrary"),
        name="in_projection",
    )(x, mods, mods, g, w)


def _attn_head(q, keys, values, biases):
    s = [_dot_nt(q, k) if b is None else _dot_nt(q, k) + b for k, b in zip(keys, biases)]
    yield
    m = functools.reduce(jnp.maximum, [jnp.max(x, axis=-1, keepdims=True) for x in s])
    p = [jnp.exp(x - m) for x in s]
    l = sum(jnp.sum(x, axis=-1, keepdims=True) for x in p)
    o = sum(_dot(x.astype(BF16), v) for x, v in zip(p, values))
    yield
    return o / l


def _attn_ctx_kernel(q_ref, k_ref, v_ref, o_ref, *, heads):
    scale = HEAD_DIM ** -0.5
    gens = []
    for h in range(heads):
        sl = slice(h * HEAD_DIM, (h + 1) * HEAD_DIM)
        q = (q_ref[:, sl] * scale).astype(BF16)
        gens.append(_attn_head(q, [k_ref[:, sl].astype(BF16)], [v_ref[:, sl].astype(BF16)], [None]))
    o_ref[...] = jnp.concatenate(_lock_step(gens), axis=-1)


def _attention_context(z, seq, width):
    T = z.shape[0]
    col = lambda j: pl.BlockSpec((seq, width), lambda b: (b, j))
    return pl.pallas_call(
        functools.partial(_attn_ctx_kernel, heads=width // HEAD_DIM),
        grid=(T // seq,),
        in_specs=[col(0), col(1), col(2)],
        out_specs=col(0),
        out_shape=jax.ShapeDtypeStruct((T, width), F32),
        compiler_params=_params("arbitrary"),
        name="attention_context",
    )(z, z, z)


def _attn_lat_kernel(q_ref, k_ref, v_ref, kc_ref, vc_ref, bias_ref, o_ref, *, heads, rows, kr):
    scale = HEAD_DIM ** -0.5
    r = pl.program_id(1)
    r0 = jnp.clip(r - kr // 2, 0, rows - kr)
    start = pl.multiple_of(r0 * GRID_W, GRID_W)
    k_win = k_ref[pl.ds(start, kr * GRID_W), :]
    v_win = v_ref[pl.ds(start, kr * GRID_W), :]
    gens = []
    for h in range(heads):
        sl = slice(h * HEAD_DIM, (h + 1) * HEAD_DIM)
        q = (q_ref[:, sl] * scale).astype(BF16)
        gens.append(_attn_head(q, [k_win[:, sl].astype(BF16), kc_ref[:, sl].astype(BF16)],
                               [v_win[:, sl].astype(BF16), vc_ref[:, sl].astype(BF16)], [bias_ref[h], None]))
    o_ref[...] = jnp.concatenate(_lock_step(gens), axis=-1)


def _toeplitz_kernel(rpb_ref, pick_ref, valid_ref, o_ref):
    o_ref[...] = jnp.where(valid_ref[...] > 0.5, _dot_hi(rpb_ref[...], pick_ref[...]), MASK_NEG)


def _window_bias(rpb, kr):
    rpb = rpb.reshape((-1,) + rpb.shape[2:])
    delta = np.arange(kr)[:, None]
    i = np.arange(kr)[None, :]
    row_off = (WIN_ROWS - 1) - delta + i
    q = np.arange(GRID_W)[:, None]
    kc = np.arange(GRID_W)[None, :]
    c0 = np.clip(q - WIN_COLS // 2, 0, GRID_W - WIN_COLS)
    valid = (kc >= c0) & (kc < c0 + WIN_COLS)
    col_off = kc - q + (WIN_COLS - 1)
    pick = (col_off[None] == np.arange(2 * WIN_COLS - 1)[:, None, None]) & valid[None]
    H, n_ro, n_co = rpb.shape
    toep = pl.pallas_call(
        _toeplitz_kernel,
        out_shape=jax.ShapeDtypeStruct((H * n_ro, GRID_W * GRID_W), F32),
        name="window_bias",
    )(rpb.astype(F32).reshape(H * n_ro, n_co), jnp.asarray(pick.reshape(n_co, -1), F32),
      jnp.asarray(valid.reshape(1, -1), F32))
    toep = toep.reshape(H, n_ro, GRID_W, GRID_W)
    per_delta = []
    for dl in range(kr):
        lo = int(row_off[dl, 0])
        win = toep[:, lo:lo + kr]
        per_delta.append(win.transpose(0, 2, 1, 3).reshape(rpb.shape[0], GRID_W, kr * GRID_W))
    return jnp.stack(per_delta, axis=0)


def _attention_latent(z, k_ctx, v_ctx, bias, layer, dec_seq, width):
    B, _, P, _ = k_ctx.shape
    rows = dec_seq // GRID_W
    kr = bias.shape[0]
    heads = width // HEAD_DIM

    def bias_index(b, r):
        return (r - jnp.clip(r - kr // 2, 0, rows - kr), layer, 0, 0)

    q_spec = pl.BlockSpec((GRID_W, width), lambda b, r: (b * rows + r, 0))
    return pl.pallas_call(
        functools.partial(_attn_lat_kernel, heads=heads, rows=rows, kr=kr),
        grid=(B, rows),
        in_specs=[
            q_spec,
            pl.BlockSpec((dec_seq, width), lambda b, r: (b, 1)),
            pl.BlockSpec((dec_seq, width), lambda b, r: (b, 2)),
            pl.BlockSpec((None, None, P, width), lambda b, r: (b, layer, 0, 0)),
            pl.BlockSpec((None, None, P, width), lambda b, r: (b, layer, 0, 0)),
            pl.BlockSpec((None, heads, GRID_W, kr * GRID_W), bias_index),
        ],
        out_specs=q_spec,
        out_shape=jax.ShapeDtypeStruct((B * dec_seq, width), F32),
        compiler_params=_params("arbitrary", "arbitrary"),
        name="attention_latent",
    )(z, z, z, k_ctx, v_ctx, bias)


def _sgu_kernel(u_ref, v_ref, gain_ref, w_ref, b_ref, o_ref, *, groups, chunks):
    for c in range(chunks):
        rows = slice(c * SG_CHUNK, (c + 1) * SG_CHUNK)
        for g in range(groups):
            sl = slice(g * SG_GROUP_DIM, (g + 1) * SG_GROUP_DIM)
            vn = _rms(v_ref[rows, sl]) * gain_ref[:, sl]
            mixed = _dot(w_ref[g].astype(BF16), vn.astype(BF16)) + b_ref[:, g:g + 1]
            o_ref[rows, sl] = u_ref[rows, sl] * mixed


def _spatial_gating(z, col_u, gain, w_s, b_t, layer):
    T = z.shape[0]
    G = w_s.shape[1]
    W = G * SG_GROUP_DIM
    tm = _pick(T, (512, 256, 128))
    return pl.pallas_call(
        functools.partial(_sgu_kernel, groups=G, chunks=tm // SG_CHUNK),
        grid=(T // tm,),
        in_specs=[pl.BlockSpec((tm, W), lambda i: (i, col_u)),
                  pl.BlockSpec((tm, W), lambda i: (i, col_u + 1)),
                  _of_layer(gain, layer), _of_layer(w_s, layer), _of_layer(b_t, layer)],
        out_specs=pl.BlockSpec((tm, W), lambda i: (i, 0)),
        out_shape=jax.ShapeDtypeStruct((T, W), F32),
        compiler_params=_params("arbitrary"),
        name="spatial_gating",
    )(z, z, gain, w_s, b_t)


def _softplus(y):
    return jnp.maximum(y, 0.0) + jnp.log(1.0 + jnp.exp(-jnp.abs(y)))


def _rwkv_pre_kernel(r_ref, k_ref, v_ref, lora_ref, w0_ref, w2_ref, a0_ref, a2_ref, kkw_ref, ka_ref,
                     bd_ref, tri_ref,
                     at_f, rt_f, bh_f, kh_f, wc_f, at_b, rt_b, bh_b, kh_b, wc_b, v_o, *, pairs, chunks, lora_w, lora_a):
    C = RW_CHUNK
    r = r_ref[...]
    k = k_ref[...]
    bd = bd_ref[...]
    kk = k * kkw_ref[...]
    kk2 = kk * kk
    ssq = jnp.concatenate([_dot_wide_lhs(kk2[:, p * LANES:(p + 1) * LANES], bd) for p in range(pairs)], axis=-1)
    kk = kk * lax.rsqrt(ssq + KK_EPS)
    tw = jnp.tanh(lora_ref[:, :lora_w]).astype(BF16)
    xa = lora_ref[:, lora_w:lora_w + lora_a].astype(BF16)
    v_bf = v_ref[...].astype(BF16)
    for p in range(pairs):
        v_o[p] = v_bf[:, p * LANES:(p + 1) * LANES]
    outs = ((at_f, rt_f, bh_f, kh_f, wc_f), (at_b, rt_b, bh_b, kh_b, wc_b))
    for d in range(2):
        at_o, rt_o, bh_o, kh_o, wc_o = outs[d]
        w_log = -_softplus(-(w0_ref[d:d + 1, :] + _dot(tw, w2_ref[d].astype(BF16)))) - 0.5
        logw = -jnp.exp(w_log)
        a_rate = jax.nn.sigmoid(a0_ref[d:d + 1, :] + _dot(xa, a2_ref[d].astype(BF16)))
        k_d = k * (1.0 + (a_rate - 1.0) * ka_ref[...])
        b = kk * a_rate
        tri = tri_ref[d]
        tot_row = C - 1 if d == 0 else 0
        for c in range(chunks):
            rows = slice(c * C, (c + 1) * C)
            lw = logw[rows]
            cum = _dot_wide_rhs(tri, lw)
            tot = cum[tot_row:tot_row + 1, :]
            e_dn = jnp.exp(tot - cum)
            at = -kk[rows] * jnp.exp(cum - lw - tot)
            rt = r[rows] * jnp.exp(cum - tot)
            bh = b[rows] * e_dn
            kh = k_d[rows] * e_dn
            wc = jnp.exp(tot)
            for p in range(pairs):
                sl = slice(p * LANES, (p + 1) * LANES)
                at_o[p, rows, :] = at[:, sl].astype(BF16)
                rt_o[p, rows, :] = rt[:, sl].astype(BF16)
                bh_o[p, rows, :] = bh[:, sl].astype(BF16)
                kh_o[p, rows, :] = kh[:, sl].astype(BF16)
                wc_o[c, p] = wc[:, sl]


def _rwkv_cols(z_cols, width):
    bw = math.gcd(z_cols, width)
    assert bw % LANES == 0
    return bw, z_cols // bw, (z_cols + width) // bw, (z_cols + 2 * width) // bw


def _rwkv_prepare(z, r_off, lora_blk, lora_dims, w0, w2, a0, a2, kkw, ka, bd, tri, layer):
    T = z.shape[0]
    RW = w0.shape[-1]
    lora_w, lora_a, lora_g = lora_dims
    lora_n = lora_w + lora_a + lora_g
    bw, cr, ck, cv = _rwkv_cols(r_off, RW)
    pairs = bw // LANES
    C = RW_CHUNK
    tm = _pick(T, (256, 128, 64))
    tok = lambda c0: pl.BlockSpec((tm, bw), lambda i, h: (i, c0 + h))
    par2 = lambda a: _of_layer(a, layer, (a.shape[1], bw), lambda i, h: (0, h))
    par3 = lambda a: _of_layer(a, layer, (a.shape[1], a.shape[2], bw), lambda i, h: (0, 0, h))
    packed = pl.BlockSpec((pairs, tm, LANES), lambda i, h: (h, i, 0))
    wc_spec = pl.BlockSpec((tm // C, pairs, 1, LANES), lambda i, h: (i, h, 0, 0))
    packed_shape = jax.ShapeDtypeStruct((RW // LANES, T, LANES), BF16)
    wc_shape = jax.ShapeDtypeStruct((T // C, RW // LANES, 1, LANES), F32)
    return pl.pallas_call(
        functools.partial(_rwkv_pre_kernel, pairs=pairs, chunks=tm // C, lora_w=lora_w, lora_a=lora_a),
        grid=(T // tm, RW // bw),
        in_specs=[tok(cr), tok(ck), tok(cv),
                  pl.BlockSpec((tm, lora_n), lambda i, h: (i, lora_blk)),
                  par2(w0), par3(w2), par2(a0), par3(a2), par2(kkw), par2(ka), _full(bd), _full(tri)],
        out_specs=[packed, packed, packed, packed, wc_spec, packed, packed, packed, packed, wc_spec, packed],
        out_shape=[packed_shape] * 4 + [wc_shape] + [packed_shape] * 4 + [wc_shape] + [packed_shape],
        compiler_params=_params("arbitrary", "arbitrary"),
        name="rwkv_prepare",
    )(z, z, z, z, w0, w2, a0, a2, kkw, ka, bd, tri)


def _rwkv_pair_chunk(at, rt, bh, kh, v, hs, masks):
    C = RW_CHUNK
    strict, incl, eye, same_head, lane_lo = masks
    zero = jnp.zeros_like(at)
    stack = lambda x: jnp.concatenate([jnp.where(lane_lo, x, zero), jnp.where(lane_lo, zero, x)], axis=0)
    twice = lambda x: jnp.concatenate([x, x], axis=0)
    atm, rtm, vm = stack(at), stack(rt), stack(v)
    bk = jnp.concatenate([twice(bh), twice(kh)], axis=0)
    a = _dot_nt(jnp.concatenate([atm, rtm], axis=0), bk)
    yield
    l = jnp.where(strict, a[:2 * C, :2 * C], 0.0)
    a_ak = jnp.where(strict, a[:2 * C, 2 * C:], 0.0).astype(BF16)
    a_rb = jnp.where(incl, a[2 * C:, :2 * C], 0.0).astype(BF16)
    a_rk = jnp.where(incl, a[2 * C:, 2 * C:], 0.0).astype(BF16)
    hs_b = hs.astype(BF16)
    x2 = _dot(jnp.concatenate([atm, a_ak], axis=1), jnp.concatenate([hs_b, vm], axis=0)).astype(BF16)
    t = eye + l
    pw = l.astype(BF16)
    pw = _dot(pw, pw).astype(BF16)
    yield
    for _ in range(int(math.log2(C)) - 2):
        both = _dot(jnp.concatenate([t.astype(BF16), pw], axis=0), pw)
        yield
        t = t + both[:2 * C]
        pw = both[2 * C:].astype(BF16)
    t = t + _dot(t.astype(BF16), pw)
    yield
    u = _dot(t.astype(BF16), x2).astype(BF16)
    yield
    o = _dot(jnp.concatenate([rtm, a_rb, a_rk], axis=1), jnp.concatenate([hs_b, u, vm], axis=0))
    h_new = hs + jnp.where(same_head, _dot_tn(bk, jnp.concatenate([u, vm], axis=0)), 0.0)
    return o[:C] + o[C:], h_new


def _rwkv_core_kernel(at_f, rt_f, bh_f, kh_f, v_f, wc_f, at_b, rt_b, bh_b, kh_b, v_b, wc_b, h0_ref,
                      of_ref, ob_ref, hs_ref, *, pairs, group):
    C = RW_CHUNK

    @pl.when(pl.program_id(1) == 0)
    def _():
        hs_ref[...] = h0_ref[...]

    row = _iota((2 * C, 2 * C), 0)
    col = _iota((2 * C, 2 * C), 1)
    same = (row // C) == (col // C)
    rt_, ct_ = row % C, col % C
    eye = jnp.where(row == col, 1.0, 0.0).astype(F32)
    lrow = _iota((LANES, LANES), 0)
    lcol = _iota((LANES, LANES), 1)
    same_head = (lrow // HEAD_DIM) == (lcol // HEAD_DIM)
    lane_lo = _iota((C, LANES), 1) < HEAD_DIM
    masks = ((same & (ct_ < rt_), same & (ct_ <= rt_), eye, same_head, lane_lo),
             (same & (ct_ > rt_), same & (ct_ >= rt_), eye, same_head, lane_lo))
    dirs = ((at_f, rt_f, bh_f, kh_f, v_f, wc_f, of_ref), (at_b, rt_b, bh_b, kh_b, v_b, wc_b, ob_ref))

    for g0 in range(0, pairs, group):
        jobs = [(d, p) for p in range(g0, g0 + group) for d in range(2)]
        gens = []
        for d, p in jobs:
            at_r, rt_r, bh_r, kh_r, v_r, wc_r, _ = dirs[d]
            wcol = jnp.sum(jnp.where(lrow == lcol, jnp.broadcast_to(wc_r[p], (LANES, LANES)), 0.0),
                           axis=1, keepdims=True)
            gens.append(_rwkv_pair_chunk(at_r[p], rt_r[p], bh_r[p], kh_r[p], v_r[p], hs_ref[d, p] * wcol, masks[d]))
        for (d, p), (o, h_new) in zip(jobs, _lock_step(gens)):
            hs_ref[d, p] = h_new
            dirs[d][-1][p] = o


def _rwkv_operands(r, k, v, lora, w0, w2, a0, a2, kkw, ka, bd, tri, d, lora_w, lora_a):
    C, RW = r.shape
    kk = k * kkw
    kk2 = kk * kk
    ssq = jnp.concatenate([_dot_wide_lhs(kk2[:, p * LANES:(p + 1) * LANES], bd) for p in range(RW // LANES)], axis=-1)
    kk = kk * lax.rsqrt(ssq + KK_EPS)
    tw = jnp.tanh(lora[:, :lora_w]).astype(BF16)
    xa = lora[:, lora_w:lora_w + lora_a].astype(BF16)
    w_log = -_softplus(-(w0[d:d + 1, :] + _dot(tw, w2[d].astype(BF16)))) - 0.5
    logw = -jnp.exp(w_log)
    a_rate = jax.nn.sigmoid(a0[d:d + 1, :] + _dot(xa, a2[d].astype(BF16)))
    k_d = k * (1.0 + (a_rate - 1.0) * ka)
    cum = _dot_wide_rhs(tri[d], logw)
    tot_row = C - 1 if d == 0 else 0
    tot = cum[tot_row:tot_row + 1, :]
    e_dn = jnp.exp(tot - cum)
    at = -kk * jnp.exp(cum - logw - tot)
    rt = r * jnp.exp(cum - tot)
    bh = kk * a_rate * e_dn
    kh = k_d * e_dn
    return tuple(x.astype(BF16) for x in (at, rt, bh, kh, v)) + (jnp.exp(tot),)


def _rwkv_fused_kernel(*refs, pairs, group, n_col, lora_w, lora_a):
    C = RW_CHUNK
    n_tok = 3 * n_col + 1
    tok_refs = (refs[:n_tok], refs[n_tok:2 * n_tok])
    w0_ref, w2_ref, a0_ref, a2_ref, kkw_ref, ka_ref, bd_ref, tri_ref, h0_ref = refs[2 * n_tok:2 * n_tok + 9]
    of_ref, ob_ref, hs_ref = refs[2 * n_tok + 9:]

    @pl.when(pl.program_id(1) == 0)
    def _():
        hs_ref[...] = h0_ref[...]

    row = _iota((2 * C, 2 * C), 0)
    col = _iota((2 * C, 2 * C), 1)
    same = (row // C) == (col // C)
    rt_, ct_ = row % C, col % C
    eye = jnp.where(row == col, 1.0, 0.0).astype(F32)
    lrow = _iota((LANES, LANES), 0)
    lcol = _iota((LANES, LANES), 1)
    same_head = (lrow // HEAD_DIM) == (lcol // HEAD_DIM)
    lane_lo = _iota((C, LANES), 1) < HEAD_DIM
    masks = ((same & (ct_ < rt_), same & (ct_ <= rt_), eye, same_head, lane_lo),
             (same & (ct_ > rt_), same & (ct_ >= rt_), eye, same_head, lane_lo))
    outs = (of_ref, ob_ref)

    params = (w0_ref[...], w2_ref[...], a0_ref[...], a2_ref[...], kkw_ref[...], ka_ref[...], bd_ref[...], tri_ref[...])
    operands = []
    for d in range(2):
        t = tok_refs[d]
        cat = lambda rs: jnp.concatenate([x[...] for x in rs], axis=-1)
        r, k, v = cat(t[:n_col]), cat(t[n_col:2 * n_col]), cat(t[2 * n_col:3 * n_col])
        operands.append(_rwkv_operands(r, k, v, t[3 * n_col][...], *params, d, lora_w, lora_a))

    for g0 in range(0, pairs, group):
        jobs = [(d, p) for p in range(g0, g0 + group) for d in range(2)]
        gens = []
        for d, p in jobs:
            sl = slice(p * LANES, (p + 1) * LANES)
            at, rt, bh, kh, v, wc = (x[:, sl] for x in operands[d])
            wcol = jnp.sum(jnp.where(lrow == lcol, jnp.broadcast_to(wc, (LANES, LANES)), 0.0), axis=1, keepdims=True)
            gens.append(_rwkv_pair_chunk(at, rt, bh, kh, v, hs_ref[d, p] * wcol, masks[d]))
        for (d, p), (o, h_new) in zip(jobs, _lock_step(gens)):
            hs_ref[d, p] = h_new
            outs[d][p] = o


def _rwkv_scan_fused(z, r_off, lora_blk, lora_dims, w0, w2, a0, a2, kkw, ka, bd, tri, layer, h0, n_seq, seq_len):
    T = z.shape[0]
    RW = w0.shape[-1]
    lora_w, lora_a, lora_g = lora_dims
    bw, cr, ck, cv = _rwkv_cols(r_off, RW)
    n_col = RW // bw
    pairs = RW // LANES
    C = RW_CHUNK
    nch = seq_len // C
    fwd = lambda s, c: s * nch + c
    bwd = lambda s, c: s * nch + (nch - 1 - c)

    def tok_specs(f):
        cols = [c0 + h for c0 in (cr, ck, cv) for h in range(n_col)]
        specs = [pl.BlockSpec((C, bw), lambda s, c, j=j: (f(s, c), j)) for j in cols]
        return specs + [pl.BlockSpec((C, sum(lora_dims)), lambda s, c: (f(s, c), lora_blk))]

    st = pl.BlockSpec((None, 2, pairs, LANES, LANES), lambda s, c: (s, 0, 0, 0, 0))
    pk = lambda f: pl.BlockSpec((pairs, C, LANES), lambda s, c: (0, f(s, c), 0))
    o_shape = jax.ShapeDtypeStruct((pairs, T, LANES), F32)
    n_tok = 3 * n_col + 1
    return pl.pallas_call(
        functools.partial(_rwkv_fused_kernel, pairs=pairs, group=_pick(pairs, (RW_PAIR_GROUP, 4, 2, 1)),
                          n_col=n_col, lora_w=lora_w, lora_a=lora_a),
        grid=(n_seq, nch),
        in_specs=tok_specs(fwd) + tok_specs(bwd)
        + [_of_layer(a, layer) for a in (w0, w2, a0, a2, kkw, ka)] + [_full(bd), _full(tri), st],
        out_specs=[pk(fwd), pk(bwd), st],
        out_shape=[o_shape, o_shape, jax.ShapeDtypeStruct(h0.shape, F32)],
        compiler_params=_params("arbitrary", "arbitrary"),
        name="rwkv_scan",
    )(*([z] * (2 * n_tok)), w0, w2, a0, a2, kkw, ka, bd, tri, h0)


def _pair_states(h):
    n, _, H, _, _ = h.shape
    hp = h.reshape(n, 2, H // 2, 2, HEAD_DIM, HEAD_DIM)
    z = jnp.zeros_like(hp[:, :, :, 0])
    top = jnp.concatenate([hp[:, :, :, 0], z], axis=-1)
    bot = jnp.concatenate([z, hp[:, :, :, 1]], axis=-1)
    return jnp.concatenate([top, bot], axis=-2)


def _rwkv_core(pre, h0, n_seq, seq_len):
    at_f, rt_f, bh_f, kh_f, wc_f, at_b, rt_b, bh_b, kh_b, wc_b, v = pre
    pairs, T, _ = at_f.shape
    C = RW_CHUNK
    nch = seq_len // C
    fwd = lambda s, c: s * nch + c
    bwd = lambda s, c: s * nch + (nch - 1 - c)
    pk = lambda f: pl.BlockSpec((pairs, C, LANES), lambda s, c: (0, f(s, c), 0))
    wc = lambda f: pl.BlockSpec((None, pairs, 1, LANES), lambda s, c: (f(s, c), 0, 0, 0))
    st = pl.BlockSpec((None, 2, pairs, LANES, LANES), lambda s, c: (s, 0, 0, 0, 0))
    o_shape = jax.ShapeDtypeStruct((pairs, T, LANES), F32)
    return pl.pallas_call(
        functools.partial(_rwkv_core_kernel, pairs=pairs, group=_pick(pairs, (RW_PAIR_GROUP, 4, 2, 1))),
        grid=(n_seq, nch),
        in_specs=[pk(fwd), pk(fwd), pk(fwd), pk(fwd), pk(fwd), wc(fwd),
                  pk(bwd), pk(bwd), pk(bwd), pk(bwd), pk(bwd), wc(bwd), st],
        out_specs=[pk(fwd), pk(bwd), st],
        out_shape=[o_shape, o_shape, jax.ShapeDtypeStruct(h0.shape, F32)],
        compiler_params=_params("arbitrary", "arbitrary"),
        name="rwkv_scan",
    )(at_f, rt_f, bh_f, kh_f, v, wc_f, at_b, rt_b, bh_b, kh_b, v, wc_b, h0)


def _rwkv_post_kernel(of_ref, ob_ref, r_ref, k_ref, v_ref, lora_ref, g2_ref, lnw_ref, lnb_ref, rk_ref, bd_ref,
                      o_ref, *, pairs, lora_g):
    bd = bd_ref[...]
    n_lora = lora_ref.shape[1]
    sg = jax.nn.sigmoid(lora_ref[:, n_lora - lora_g:]).astype(BF16)
    inv = 1.0 / HEAD_DIM
    for p in range(pairs):
        sl = slice(p * LANES, (p + 1) * LANES)
        o = of_ref[p] + ob_ref[p]
        mu = _dot_wide_lhs(o, bd) * inv
        dlt = o - mu
        var = _dot_wide_lhs(dlt * dlt, bd) * inv
        on = dlt * lax.rsqrt(var + GN_EPS) * lnw_ref[:, sl] + lnb_ref[:, sl]
        bonus = _dot_wide_lhs(r_ref[:, sl] * k_ref[:, sl] * rk_ref[:, sl], bd) * v_ref[:, sl]
        gate = _dot(sg, g2_ref[:, sl].astype(BF16))
        o_ref[:, sl] = (on + bonus) * gate


def _rwkv_finish(o_f, o_b, z, r_off, lora_blk, lora_dims, g2, lnw, lnb, rk, bd, layer):
    T = z.shape[0]
    RW = g2.shape[-1]
    lora_n = sum(lora_dims)
    bw, cr, ck, cv = _rwkv_cols(r_off, RW)
    pairs = bw // LANES
    tm = _pick(T, (256, 128, 64))
    tok = lambda c0: pl.BlockSpec((tm, bw), lambda i, h: (i, c0 + h))
    par2 = lambda a: _of_layer(a, layer, (a.shape[1], bw), lambda i, h: (0, h))
    packed = pl.BlockSpec((pairs, tm, LANES), lambda i, h: (h, i, 0))
    return pl.pallas_call(
        functools.partial(_rwkv_post_kernel, pairs=pairs, lora_g=lora_dims[2]),
        grid=(T // tm, RW // bw),
        in_specs=[packed, packed, tok(cr), tok(ck), tok(cv),
                  pl.BlockSpec((tm, lora_n), lambda i, h: (i, lora_blk)),
                  par2(g2), par2(lnw), par2(lnb), par2(rk), _full(bd)],
        out_specs=pl.BlockSpec((tm, bw), lambda i, h: (i, h)),
        out_shape=jax.ShapeDtypeStruct((T, RW), F32),
        compiler_params=_params("arbitrary", "arbitrary"),
        name="rwkv_finish",
    )(o_f, o_b, z, z, z, z, g2, lnw, lnb, rk, bd)


def _outproj_kernel(oa_ref, ob_ref, oc_ref, x_ref, w_ref, g1_ref, sh2_ref, sc2_ref, gpost_ref, gpre_ref, wr_ref,
                    xo_ref, h2_ref, aff_ref):
    cat = jnp.concatenate([oa_ref[...], ob_ref[...], oc_ref[...]], axis=-1).astype(BF16)
    mixed = _dot(cat, w_ref[...])
    x = x_ref[...] + g1_ref[...] * (_rms(mixed) * gpost_ref[...])
    xo_ref[...] = x
    h2 = _rms(x) * gpre_ref[...] * (1.0 + sc2_ref[...]) + sh2_ref[...]
    h2_ref[...] = h2.astype(BF16)
    h_hi, h_lo = _split2(h2)
    w_hi, w_lo = _split2(wr_ref[...])
    logits = _dot(h_hi, w_hi) + _dot(h_lo, w_hi) + _dot(h_hi, w_lo)
    e = jnp.exp(logits - jnp.max(logits, axis=-1, keepdims=True))
    aff_ref[...] = e / jnp.sum(e, axis=-1, keepdims=True)


def _out_projection(o_a, o_b, o_c, x, w_out, layer, mods, mod_row, g_post, g_pre, w_router, tm):
    T, D = x.shape
    E = w_router.shape[2]
    row = lambda a: pl.BlockSpec((tm, a.shape[1]), lambda i: (i, 0))
    mod = lambda col: _mod_spec(mods, layer, mod_row, col)
    return pl.pallas_call(
        _outproj_kernel,
        grid=(T // tm,),
        in_specs=[row(o_a), row(o_b), row(o_c), row(x), _of_layer(w_out, layer),
                  mod(2), mod(3), mod(4), _of_layer(g_post, layer), _of_layer(g_pre, layer),
                  _of_layer(w_router, layer)],
        out_specs=[pl.BlockSpec((tm, D), lambda i: (i, 0)),
                   pl.BlockSpec((tm, D), lambda i: (i, 0)),
                   pl.BlockSpec((tm, E), lambda i: (i, 0))],
        out_shape=[jax.ShapeDtypeStruct((T, D), F32), jax.ShapeDtypeStruct((T, D), BF16),
                   jax.ShapeDtypeStruct((T, E), F32)],
        compiler_params=_params("arbitrary"),
        name="out_projection",
    )(o_a, o_b, o_c, x, w_out, mods, mods, mods, g_post, g_pre, w_router)


def _select_kernel(a_ref, ut_ref, pos_ref, cnt_ref, *, cap):
    a = a_ref[...]
    R, T = a.shape
    lo = jnp.zeros((R, 1), jnp.int32)
    for bit in range(30, -1, -1):
        cand = lo | (1 << bit)
        cnt = jnp.sum(jnp.where(a >= lax.bitcast_convert_type(cand, F32), 1.0, 0.0), axis=1, keepdims=True)
        lo = jnp.where(cnt >= cap, cand, lo)
    thr = lax.bitcast_convert_type(lo, F32)
    gt = a > thr
    eq = a == thr
    need = cap - jnp.sum(jnp.where(gt, 1.0, 0.0), axis=1, keepdims=True)
    ut = ut_ref[...]
    nblk = T // LANES
    tie_carry = jnp.zeros((R, 1), F32)
    pos_carry = jnp.zeros((R, 1), F32)
    for n in range(nblk):
        sl = slice(n * LANES, (n + 1) * LANES)
        eq_b = jnp.where(eq[:, sl], 1.0, 0.0)
        tie_rank = _dot(eq_b.astype(BF16), ut) + tie_carry
        tie_carry = tie_carry + jnp.sum(eq_b, axis=1, keepdims=True)
        sel = jnp.where(gt[:, sl], 1.0, jnp.where(tie_rank < need, eq_b, 0.0))
        pos = _dot(sel.astype(BF16), ut) + pos_carry
        cnt_ref[:, n:n + 1] = pos_carry.astype(jnp.int32)
        pos_carry = pos_carry + jnp.sum(sel, axis=1, keepdims=True)
        pos_ref[:, sl] = jnp.where(sel > 0.5, pos, -1.0).astype(jnp.int32)


def _select(aff_rows, cap, ut):
    R, T = aff_rows.shape
    tr = _pick(R, (128, 64, 32, 16, 8))
    return pl.pallas_call(
        functools.partial(_select_kernel, cap=cap),
        grid=(R // tr,),
        in_specs=[pl.BlockSpec((tr, T), lambda i: (i, 0)), pl.BlockSpec((LANES, LANES), lambda i: (0, 0))],
        out_specs=[pl.BlockSpec((tr, T), lambda i: (i, 0)), pl.BlockSpec((tr, T // LANES), lambda i: (i, 0))],
        out_shape=[jax.ShapeDtypeStruct((R, T), jnp.int32), jax.ShapeDtypeStruct((R, T // LANES), jnp.int32)],
        compiler_params=_params("arbitrary"),
        name="expert_select",
    )(aff_rows, ut)


def _route(aff, n_sets, set_len, cap, ut):
    E = aff.shape[1]
    rows = aff.reshape(n_sets, set_len, E).transpose(0, 2, 1).reshape(n_sets * E, set_len)
    pos, cnt = _select(rows, cap, ut)
    pos = pos.reshape(n_sets, E, set_len)
    return pos, pos.transpose(0, 2, 1).reshape(n_sets * set_len, E), cnt, rows


def _slot_tables(experts, cap):
    pair_expert = np.arange(experts * cap) // cap
    rep = (pair_expert[:, None] == np.arange(experts)[None, :]).astype(np.float32)
    slot = (np.arange(experts * cap) % cap).astype(np.float32)
    return jnp.asarray(rep, BF16), jnp.asarray(rep.T, BF16), jnp.asarray(slot[:, None]), jnp.asarray(slot[None, :])


def _gather_ctx_kernel(h_ref, pos_ref, aff_ref, rep_ref, slot_ref, xs_ref, gs_ref, *, experts, cap):
    rep = rep_ref[...]
    pos_rep = _dot(rep, pos_ref[...].astype(F32).astype(BF16))
    hit = pos_rep == slot_ref[...]
    xs = _dot(jnp.where(hit, 1.0, 0.0).astype(BF16), h_ref[...]).astype(BF16)
    xs_ref[...] = xs.reshape(experts, cap, xs.shape[1])
    gs = jnp.sum(jnp.where(hit, _dot_wide_rhs(rep, aff_ref[...]), 0.0), axis=1, keepdims=True)
    gs_ref[...] = gs.reshape(experts, cap, 1)


def _gather_lat_kernel(cnt_ref, h_ref, pos_ref, aff_ref, xs_ref, gs_ref, acc_scr, gacc_scr, *, cap, experts):
    row = pl.program_id(0) * experts + pl.program_id(1)
    n_blocks = pos_ref.shape[0]
    step = GATHER_TOKENS // LANES
    for i in range(cap // SLOT_TILE):
        first_slot = i * SLOT_TILE
        starts = [cnt_ref[row, n * step] for n in range(n_blocks)]
        n_lo = sum(jnp.where(s <= first_slot, 1, 0) for s in starts) - 1
        n_hi = sum(jnp.where(s < first_slot + SLOT_TILE, 1, 0) for s in starts) - 1
        acc_scr[...] = jnp.zeros(acc_scr.shape, F32)
        gacc_scr[...] = jnp.zeros(gacc_scr.shape, F32)

        def body(n, carry, first_slot=first_slot):
            toks = h_ref[pl.ds(pl.multiple_of(n * GATHER_TOKENS, GATHER_TOKENS), GATHER_TOKENS), :]
            hit = _iota((SLOT_TILE, GATHER_TOKENS), 0) + first_slot == pos_ref[n]
            acc_scr[...] += _dot(jnp.where(hit, 1.0, 0.0).astype(BF16), toks)
            gacc_scr[...] += jnp.sum(jnp.where(hit, aff_ref[n], 0.0), axis=1, keepdims=True)
            return carry

        lax.fori_loop(n_lo, n_hi + 1, body, 0)
        xs_ref[first_slot:first_slot + SLOT_TILE, :] = acc_scr[...].astype(BF16)
        gs_ref[first_slot:first_slot + SLOT_TILE, :] = gacc_scr[...]


def _expert_up_kernel(xc_ref, xl_ref, wg_ref, wu_ref, hid_ref):
    wg = wg_ref[...].astype(BF16)
    wu = wu_ref[...].astype(BF16)
    rc = xc_ref.shape[0]
    for ref, rows in ((xc_ref, slice(0, rc)), (xl_ref, slice(rc, hid_ref.shape[0]))):
        x = ref[...]
        g = _dot(x, wg)
        hid_ref[rows, :] = (g * jax.nn.sigmoid(g) * _dot(x, wu)).astype(BF16)


def _expert_down_kernel(hid_ref, gsc_ref, gsl_ref, wd_ref, y_ref):
    wd = wd_ref[...].astype(BF16)
    rc = gsc_ref.shape[0]
    for gs_ref, rows in ((gsc_ref, slice(0, rc)), (gsl_ref, slice(rc, y_ref.shape[0]))):
        y_ref[rows, :] = (_dot(hid_ref[rows, :], wd) * gs_ref[...]).astype(BF16)


def _slot_one_hot(pos_col, cap):
    tm = pos_col.shape[0]
    return jnp.where(_iota((tm, cap), 1).astype(F32) == pos_col, 1.0, 0.0).astype(BF16)


def _col_chunks(width):
    cw = _pick(width, (2 * LANES, LANES))
    return [slice(n * cw, (n + 1) * cw) for n in range(width // cw)]


def _ffn_residual(x, ffn, g2, g_post):
    return x + g2 * (_rms(ffn) * g_post)


def _combine_ctx_kernel(x_ref, y_ref, pos_ref, rep_ref, slot_ref, g2_ref, gpost_ref, o_ref, *, experts, cap):
    pos_exp = _dot(pos_ref[...].astype(F32).astype(BF16), rep_ref[...])
    one_hot = jnp.where(pos_exp == slot_ref[...], 1.0, 0.0).astype(BF16)
    y = y_ref[...].reshape(experts * cap, y_ref.shape[2])
    o_ref[...] = _ffn_residual(x_ref[...], _dot(one_hot, y), g2_ref[...], gpost_ref[...])


def _combine_lat_kernel(x_ref, y_ref, pos_ref, g2_ref, gpost_ref, o_ref, acc_scr, *, cap, group):
    eg = pl.program_id(2)

    @pl.when(eg == 0)
    def _():
        acc_scr[...] = jnp.zeros(acc_scr.shape, F32)

    lane = _iota(pos_ref.shape, 1)
    pos = pos_ref[...].astype(F32)
    one_hots = []
    for k in range(group):
        pos_col = jnp.sum(jnp.where(lane == eg * group + k, pos, 0.0), axis=1, keepdims=True)
        one_hots.append(_slot_one_hot(pos_col, cap))
    for cs in _col_chunks(acc_scr.shape[1]):
        acc_scr[:, cs] += sum(_dot(oh, y_ref[k, :, cs]) for k, oh in enumerate(one_hots))

    @pl.when(eg == pl.num_programs(2) - 1)
    def _():
        o_ref[...] = _ffn_residual(x_ref[...], acc_scr[...], g2_ref[...], gpost_ref[...])


def _expert_ffn(x_c, h2_c, aff_c, x_l, h2_l, aff_l, mods, g_post, w_gate, w_up, w_down, layer, ut, dims):
    batch, seq, dec_batch, dec_seq = dims
    D = x_c.shape[1]
    _, E, _, F = w_gate.shape
    cap_c = CAPACITY_FACTOR * seq // E
    cap_l = CAPACITY_FACTOR * dec_seq // E
    rows_c = batch * cap_c
    rows_l = dec_batch * cap_l
    rows = rows_c + rows_l
    assert rows_c % cap_l == 0
    lat_row_blk = rows_c // cap_l

    assert cap_c <= 256 and cap_l % SLOT_TILE == 0 and dec_seq % GATHER_TOKENS == 0 and E % COMBINE_GROUP == 0
    pos_c, post_c, _, affr_c = _route(aff_c, batch, seq, cap_c, ut)
    pos_l, post_l, cnt_l, affr_l = _route(aff_l, dec_batch, dec_seq, cap_l, ut)
    rep, rep_t, slot_col, slot_row = _slot_tables(E, cap_c)

    set_rows = pl.BlockSpec((None, E, seq), lambda b: (b, 0, 0))
    xs_c, gs_c = pl.pallas_call(
        functools.partial(_gather_ctx_kernel, experts=E, cap=cap_c),
        grid=(batch,),
        in_specs=[pl.BlockSpec((seq, D), lambda b: (b, 0)), set_rows, set_rows, _full(rep), _full(slot_col)],
        out_specs=[pl.BlockSpec((E, cap_c, D), lambda b: (0, b, 0)), pl.BlockSpec((E, cap_c, 1), lambda b: (0, b, 0))],
        out_shape=[jax.ShapeDtypeStruct((E, rows_c, D), BF16), jax.ShapeDtypeStruct((E, rows_c, 1), F32)],
        compiler_params=_params("arbitrary"),
        name="expert_gather_context",
    )(h2_c, pos_c, affr_c.reshape(batch, E, seq), rep, slot_col)
    n_tok_blk = dec_seq // GATHER_TOKENS
    blocked = pl.BlockSpec((None, None, n_tok_blk, 1, GATHER_TOKENS), lambda b, e, cnt: (b, e, 0, 0, 0))
    xs_l, gs_l = pl.pallas_call(
        functools.partial(_gather_lat_kernel, cap=cap_l, experts=E),
        grid_spec=pltpu.PrefetchScalarGridSpec(
            num_scalar_prefetch=1,
            grid=(dec_batch, E),
            in_specs=[pl.BlockSpec((dec_seq, D), lambda b, e, cnt: (b, 0)), blocked, blocked],
            out_specs=[pl.BlockSpec((None, cap_l, D), lambda b, e, cnt: (e, b, 0)),
                       pl.BlockSpec((None, cap_l, 1), lambda b, e, cnt: (e, b, 0))],
            scratch_shapes=[pltpu.VMEM((SLOT_TILE, D), F32), pltpu.VMEM((SLOT_TILE, 1), F32)]),
        out_shape=[jax.ShapeDtypeStruct((E, rows_l, D), BF16), jax.ShapeDtypeStruct((E, rows_l, 1), F32)],
        compiler_params=_params("arbitrary", "arbitrary"),
        name="expert_gather_latent",
    )(cnt_l, h2_l, pos_l.reshape(dec_batch, E, n_tok_blk, 1, GATHER_TOKENS),
      affr_l.reshape(dec_batch, E, n_tok_blk, 1, GATHER_TOKENS))

    tf = _pick(F, (512, 256, 128))
    hid = pl.pallas_call(
        _expert_up_kernel,
        grid=(E, F // tf),
        in_specs=[pl.BlockSpec((None, rows_c, D), lambda e, n: (e, 0, 0)),
                  pl.BlockSpec((None, rows_l, D), lambda e, n: (e, 0, 0)),
                  pl.BlockSpec((None, None, D, tf), lambda e, n: (layer, e, 0, n)),
                  pl.BlockSpec((None, None, D, tf), lambda e, n: (layer, e, 0, n))],
        out_specs=pl.BlockSpec((None, rows, tf), lambda e, n: (e, 0, n)),
        out_shape=jax.ShapeDtypeStruct((E, rows, F), BF16),
        compiler_params=_params("arbitrary", "arbitrary"),
        name="expert_up",
    )(xs_c, xs_l, w_gate, w_up)
    td = _pick(D, (512, 256, 128))
    y = pl.pallas_call(
        _expert_down_kernel,
        grid=(E, D // td),
        in_specs=[pl.BlockSpec((None, rows, F), lambda e, n: (e, 0, 0)),
                  pl.BlockSpec((None, rows_c, 1), lambda e, n: (e, 0, 0)),
                  pl.BlockSpec((None, rows_l, 1), lambda e, n: (e, 0, 0)),
                  pl.BlockSpec((None, None, F, td), lambda e, n: (layer, e, 0, n))],
        out_specs=pl.BlockSpec((None, rows, td), lambda e, n: (e, 0, n)),
        out_shape=jax.ShapeDtypeStruct((E, rows, D), BF16),
        compiler_params=_params("arbitrary", "arbitrary"),
        name="expert_down",
    )(hid, gs_c, gs_l, w_down)

    out_c = pl.pallas_call(
        functools.partial(_combine_ctx_kernel, experts=E, cap=cap_c),
        grid=(batch,),
        in_specs=[pl.BlockSpec((seq, D), lambda b: (b, 0)),
                  pl.BlockSpec((E, cap_c, D), lambda b: (0, b, 0)),
                  pl.BlockSpec((seq, E), lambda b: (b, 0)),
                  _full(rep_t), _full(slot_row),
                  pl.BlockSpec((None, None, 1, D), lambda b: (layer, 0, 0, N_MOD - 1)),
                  _of_layer(g_post, layer)],
        out_specs=pl.BlockSpec((seq, D), lambda b: (b, 0)),
        out_shape=jax.ShapeDtypeStruct(x_c.shape, F32),
        compiler_params=_params("arbitrary"),
        name="expert_combine_context",
    )(x_c, y, post_c, rep_t, slot_row, mods, g_post)
    tm = _pick(dec_seq, (512, 256, 128))
    nt = dec_seq // tm
    tok = lambda b, i, e: (b * nt + i, 0)
    out_l = pl.pallas_call(
        functools.partial(_combine_lat_kernel, cap=cap_l, group=COMBINE_GROUP),
        grid=(dec_batch, nt, E // COMBINE_GROUP),
        in_specs=[pl.BlockSpec((tm, D), tok),
                  pl.BlockSpec((COMBINE_GROUP, cap_l, D), lambda b, i, e: (e, lat_row_blk + b, 0)),
                  pl.BlockSpec((tm, E), tok),
                  pl.BlockSpec((None, None, 1, D), lambda b, i, e: (layer, 1 + b, 0, N_MOD - 1)),
                  _of_layer(g_post, layer)],
        out_specs=pl.BlockSpec((tm, D), tok),
        out_shape=jax.ShapeDtypeStruct(x_l.shape, F32),
        scratch_shapes=[pltpu.VMEM((tm, D), F32)],
        compiler_params=_params("arbitrary", "arbitrary", "arbitrary"),
        name="expert_combine_latent",
    )(x_l, y, post_l, mods, g_post)
    return out_c, out_l


def _cache_pack_kernel(*refs, depth):
    k_out, v_out = refs[2 * depth:]
    for li in range(depth):
        @pl.when(pl.program_id(0) == li)
        def _(li=li):
            k_out[...] = refs[2 * li][...]
            v_out[...] = refs[2 * li + 1][...]


def _cache_pack(zs, batch, seq, width):
    depth = len(zs)
    in_specs, args = [], []
    for z in zs:
        in_specs += [pl.BlockSpec((seq, width), lambda l, b: (b, 1)), pl.BlockSpec((seq, width), lambda l, b: (b, 2))]
        args += [z, z]
    out_spec = pl.BlockSpec((None, None, seq, width), lambda l, b: (b, l, 0, 0))
    shape = jax.ShapeDtypeStruct((batch, depth, seq, width), F32)
    return pl.pallas_call(
        functools.partial(_cache_pack_kernel, depth=depth),
        grid=(depth, batch), in_specs=in_specs, out_specs=[out_spec, out_spec], out_shape=[shape, shape],
        compiler_params=_params("arbitrary", "arbitrary"), name="cache_pack",
    )(*args)


def _state_pack_kernel(*refs, depth, pairs):
    o_ref = refs[depth]
    for li in range(depth):
        for d in range(2):
            for p in range(pairs):
                flipped = refs[li][d, p].T
                for j in range(LANES // HEAD_DIM):
                    sl = slice(j * HEAD_DIM, (j + 1) * HEAD_DIM)
                    o_ref[li, d, (LANES // HEAD_DIM) * p + j] = flipped[sl, sl]


def _state_pack(states):
    depth = len(states)
    B, _, pairs, _, _ = states[0].shape
    heads = pairs * (LANES // HEAD_DIM)
    spec = pl.BlockSpec((None, 2, pairs, LANES, LANES), lambda b: (b, 0, 0, 0, 0))
    return pl.pallas_call(
        functools.partial(_state_pack_kernel, depth=depth, pairs=pairs),
        grid=(B,), in_specs=[spec] * depth,
        out_specs=pl.BlockSpec((None, depth, 2, heads, HEAD_DIM, HEAD_DIM), lambda b: (b, 0, 0, 0, 0, 0)),
        out_shape=jax.ShapeDtypeStruct((B, depth, 2, heads, HEAD_DIM, HEAD_DIM), F32),
        compiler_params=_params("arbitrary"), name="state_pack",
    )(*states)


def kernel(x_prompt, x_sample, cache_k, cache_v, state_wkv, c, c_ctx, w_ada, b_ada, g_pre_mix, g_post_mix, g_pre_ffn, g_post_ffn, w_in, na_rpb, sg_gain, sg_w, sg_b, rw_w0, rw_w2, rw_a0, rw_a2, rw_g2, rw_kk, rw_ka, rw_rk, rw_ln_w, rw_ln_b, w_out, w_router, w_gate, w_up, w_down):
    batch, seq, D = x_prompt.shape
    dec_batch, dec_seq, _ = x_sample.shape
    depth = w_ada.shape[0]
    past = cache_k.shape[2]
    na_heads = cache_k.shape[3]
    naw = na_heads * HEAD_DIM
    sgw = sg_gain.shape[1]
    rw_heads = state_wkv.shape[3]
    rww = rw_heads * HEAD_DIM
    lora_dims = (rw_w2.shape[2], rw_a2.shape[2], rw_g2.shape[1])
    n_ctx = batch * seq
    n_lat = dec_batch * dec_seq
    assert 1 + dec_batch <= MOD_ROWS and rww % LANES == 0
    assert seq % RW_CHUNK == 0 and dec_seq % RW_CHUNK == 0 and dec_seq % GRID_W == 0
    sg_off = 3 * naw
    rw_off = sg_off + 2 * sgw
    lora_off = rw_off + 3 * rww
    assert sg_off % sgw == 0 and lora_off % sum(lora_dims) == 0
    col_u = sg_off // sgw
    lora_blk = lora_off // sum(lora_dims)

    pair_ones = jnp.asarray(np.kron(np.eye(LANES // HEAD_DIM), np.ones((HEAD_DIM, HEAD_DIM))), BF16)
    tril = np.tril(np.ones((RW_CHUNK, RW_CHUNK)))
    tri = jnp.asarray(np.stack([tril, tril.T]), BF16)
    ut = jnp.asarray(np.triu(np.ones((LANES, LANES)), 1), BF16)
    kr = min(WIN_ROWS, dec_seq // GRID_W)

    cond = jnp.zeros((MOD_ROWS, D), F32).at[0].set(c_ctx).at[1:1 + dec_batch].set(c)
    mods = _modulation(cond, w_ada, b_ada.reshape(depth, 1, N_MOD * D)).reshape(depth, MOD_ROWS, 1, N_MOD * D)
    w_in_b = _to_bf16(w_in)
    w_out_b = _to_bf16(w_out)

    vec = lambda a: a.reshape(depth, 1, a.shape[-1])
    g_pre_mix, g_post_mix, g_pre_ffn, g_post_ffn = vec(g_pre_mix), vec(g_post_mix), vec(g_pre_ffn), vec(g_post_ffn)
    sg_args = (vec(sg_gain), sg_w, jnp.swapaxes(sg_b, 1, 2))
    rw_args = (rw_w0, rw_w2, rw_a0, rw_a2, vec(rw_kk), vec(rw_ka), pair_ones, tri)
    fin_args = (rw_g2, vec(rw_ln_w), vec(rw_ln_b), rw_rk.reshape(depth, 1, rww), pair_ones)
    bias = _window_bias(na_rpb, kr)
    k_past = cache_k.reshape(dec_batch, depth, past, naw)
    v_past = cache_v.reshape(dec_batch, depth, past, naw)
    h0_lat = _pair_states(jnp.swapaxes(state_wkv, -1, -2).reshape((dec_batch * depth,) + state_wkv.shape[2:]))
    h0_lat = h0_lat.reshape((dec_batch, depth) + h0_lat.shape[1:])
    h0_ctx = jnp.zeros((batch, 2, rww // LANES, LANES, LANES), F32)

    tm_c = _pick(n_ctx, (512, 256, 128))
    tm_l = _pick(dec_seq, (512, 256, 128))
    ctx_row = lambda i: 0
    lat_row = lambda i: 1 + (i * tm_l) // dec_seq

    x_c = x_prompt.reshape(n_ctx, D)
    x_l = x_sample.reshape(n_lat, D)
    zs, states = [], []
    for l in range(depth):
        def mixers(x, tm, mod_row, attend, h0, n_seq, seq_len):
            z = _in_projection(x, mods, mod_row, g_pre_mix, w_in_b, l, tm)
            o_a = attend(z)
            o_b = _spatial_gating(z, col_u, *sg_args, l)
            o_f, o_bk, s_fin = _rwkv_scan_fused(z, rw_off, lora_blk, lora_dims, *rw_args, l, h0, n_seq, seq_len)
            o_c = _rwkv_finish(o_f, o_bk, z, rw_off, lora_blk, lora_dims, *fin_args, l)
            x, h2, aff = _out_projection(o_a, o_b, o_c, x, w_out_b, l, mods, mod_row,
                                         g_post_mix, g_pre_ffn, w_router, tm)
            return z, x, h2, aff, s_fin

        z_c, x_c, h2_c, aff_c, s_ctx = mixers(
            x_c, tm_c, ctx_row, lambda z: _attention_context(z, seq, naw), h0_ctx, batch, seq)
        zs.append(z_c)
        states.append(s_ctx)
        _, x_l, h2_l, aff_l, _ = mixers(
            x_l, tm_l, lat_row, lambda z: _attention_latent(z, k_past, v_past, bias, l, dec_seq, naw),
            h0_lat[:, l], dec_batch, dec_seq)

        x_c, x_l = _expert_ffn(x_c, h2_c, aff_c, x_l, h2_l, aff_l, mods, g_post_ffn,
                               w_gate, w_up, w_down, l, ut, (batch, seq, dec_batch, dec_seq))

    new_k, new_v = _cache_pack(zs, batch, seq, naw)
    cache_shape = (batch, depth, seq, na_heads, HEAD_DIM)
    return (x_c.reshape(batch, seq, D), x_l.reshape(dec_batch, dec_seq, D),
            new_k.reshape(cache_shape), new_v.reshape(cache_shape), _state_pack(states))
```

```python
import functools
import math

import numpy as np
import jax
import jax.numpy as jnp
from jax import lax
from jax.experimental import pallas as pl
from jax.experimental.pallas import tpu as pltpu

F32 = jnp.float32
BF16 = jnp.bfloat16
HIGHEST = lax.Precision.HIGHEST

HEAD_DIM = 64
LANES = 128
GRID_W = 64
WIN_ROWS = 8
WIN_COLS = 16
SG_CHUNK = 128
SG_GROUP_DIM = 128
N_MOD = 6
CAPACITY_FACTOR = 2
NORM_EPS = 1e-6
GN_EPS = 64e-5
KK_EPS = 1e-12
RW_CHUNK = 64
RW_PAIR_GROUP = 8
MASK_NEG = -1e30
MOD_ROWS = 8
VMEM_LIMIT_BYTES = 56 * 1024 * 1024
MAX_WEIGHT_COLS = 3072
GATHER_TOKENS = 256
SLOT_TILE = 128
COMBINE_GROUP = 4


def _pick(n, prefs):
    for p in prefs:
        if n % p == 0:
            return p
    raise ValueError(f"no tile in {prefs} divides {n}")


def _col_tile(n, cap):
    tiles = [t for t in range(LANES, min(n, cap) + 1, LANES) if n % t == 0]
    return tiles[-1]


def _params(*sem):
    return pltpu.CompilerParams(dimension_semantics=sem, vmem_limit_bytes=VMEM_LIMIT_BYTES)


def _rms(x):
    return x * lax.rsqrt(jnp.mean(x * x, axis=-1, keepdims=True) + NORM_EPS)


def _dot(a, b):
    return jnp.dot(a, b, preferred_element_type=F32)


def _dot_nt(a, b):
    return lax.dot_general(a, b, (((1,), (1,)), ((), ())), preferred_element_type=F32)


def _dot_tn(a, b):
    return lax.dot_general(a, b, (((0,), (0,)), ((), ())), preferred_element_type=F32)


def _dot_hi(a, b):
    return jnp.dot(a, b, preferred_element_type=F32, precision=HIGHEST)


def _split2(a):
    hi = a.astype(BF16)
    return hi, (a - hi.astype(F32)).astype(BF16)


def _dot_wide_lhs(a, b):
    hi, lo = _split2(a)
    return _dot(hi, b) + _dot(lo, b)


def _dot_wide_rhs(a, b):
    hi, lo = _split2(b)
    return _dot(a, hi) + _dot(a, lo)


def _lock_step(gens):
    results = [None] * len(gens)
    live = list(enumerate(gens))
    while live:
        still = []
        for i, g in live:
            try:
                next(g)
                still.append((i, g))
            except StopIteration as done:
                results[i] = done.value
        live = still
    return results


def _iota(shape, axis):
    return lax.broadcasted_iota(jnp.int32, shape, axis)


def _full(a):
    return pl.BlockSpec(a.shape, lambda *_: (0,) * a.ndim)


def _of_layer(a, layer, block=None, index=None):
    shape = tuple(a.shape[1:]) if block is None else tuple(block)
    if index is None:
        return pl.BlockSpec((None,) + shape, lambda *_: (layer,) + (0,) * len(shape))
    return pl.BlockSpec((None,) + shape, lambda *g: (layer,) + tuple(index(*g)))


def _cast_kernel(w_ref, o_ref):
    o_ref[...] = w_ref[...].astype(BF16)


def _to_bf16(w):
    L, R, C = w.shape
    tr = _pick(R, (256, 128, 64, 32, 16))
    spec = pl.BlockSpec((None, tr, C), lambda l, i: (l, i, 0))
    return pl.pallas_call(
        _cast_kernel, grid=(L, R // tr), in_specs=[spec], out_specs=spec,
        out_shape=jax.ShapeDtypeStruct(w.shape, BF16),
        compiler_params=_params("arbitrary", "arbitrary"), name="cast_weights",
    )(w)


def _mod_kernel(c_ref, w_ref, b_ref, o_ref):
    c = c_ref[...]
    s = (c * jax.nn.sigmoid(c)).astype(BF16)
    o_ref[...] = _dot(s, w_ref[...].astype(BF16)) + b_ref[...]


def _modulation(cond, w_ada, b_ada):
    L, D, N = w_ada.shape
    tn = _pick(N, (1024, 512, 256, 128))
    return pl.pallas_call(
        _mod_kernel,
        grid=(L, N // tn),
        in_specs=[
            pl.BlockSpec((MOD_ROWS, D), lambda l, j: (0, 0)),
            pl.BlockSpec((None, D, tn), lambda l, j: (l, 0, j)),
            pl.BlockSpec((None, 1, tn), lambda l, j: (l, 0, j)),
        ],
        out_specs=pl.BlockSpec((None, MOD_ROWS, tn), lambda l, j: (l, 0, j)),
        out_shape=jax.ShapeDtypeStruct((L, MOD_ROWS, N), F32),
        compiler_params=_params("arbitrary", "arbitrary"),
        name="modulation",
    )(cond, w_ada, b_ada)


def _inproj_kernel(x_ref, sh_ref, sc_ref, g_ref, w_ref, z_ref):
    h = _rms(x_ref[...]) * g_ref[...] * (1.0 + sc_ref[...]) + sh_ref[...]
    z_ref[...] = _dot(h.astype(BF16), w_ref[...])


def _mod_spec(mods, layer, mod_row, col):
    D = mods.shape[3] // N_MOD
    return pl.BlockSpec((None, None, 1, D), lambda *g: (layer, mod_row(g[-1]), 0, col))


def _in_projection(x, mods, mod_row, g, w, layer, tm):
    T, D = x.shape
    N = w.shape[2]
    tn = _col_tile(N, MAX_WEIGHT_COLS)
    return pl.pallas_call(
        _inproj_kernel,
        grid=(N // tn, T // tm),
        in_specs=[
            pl.BlockSpec((tm, D), lambda j, i: (i, 0)),
            _mod_spec(mods, layer, mod_row, 0),
            _mod_spec(mods, layer, mod_row, 1),
            _of_layer(g, layer),
            pl.BlockSpec((None, D, tn), lambda j, i: (layer, 0, j)),
        ],
        out_specs=pl.BlockSpec((tm, tn), lambda j, i: (i, j)),
        out_shape=jax.ShapeDtypeStruct((T, N), F32),
        compiler_params=_params("arbitrary", "arbitrary"),
        name="in_projection",
    )(x, mods, mods, g, w)


def _attn_head(q, keys, values, biases):
    s = [_dot_nt(q, k) if b is None else _dot_nt(q, k) + b for k, b in zip(keys, biases)]
    yield
    m = functools.reduce(jnp.maximum, [jnp.max(x, axis=-1, keepdims=True) for x in s])
    p = [jnp.exp(x - m) for x in s]
    l = sum(jnp.sum(x, axis=-1, keepdims=True) for x in p)
    o = sum(_dot(x.astype(BF16), v) for x, v in zip(p, values))
    yield
    return o / l


def _attn_ctx_kernel(q_ref, k_ref, v_ref, o_ref, *, heads):
    scale = HEAD_DIM ** -0.5
    gens = []
    for h in range(heads):
        sl = slice(h * HEAD_DIM, (h + 1) * HEAD_DIM)
        q = (q_ref[:, sl] * scale).astype(BF16)
        gens.append(_attn_head(q, [k_ref[:, sl].astype(BF16)], [v_ref[:, sl].astype(BF16)], [None]))
    o_ref[...] = jnp.concatenate(_lock_step(gens), axis=-1)


def _attention_context(z, seq, width):
    T = z.shape[0]
    col = lambda j: pl.BlockSpec((seq, width), lambda b: (b, j))
    return pl.pallas_call(
        functools.partial(_attn_ctx_kernel, heads=width // HEAD_DIM),
        grid=(T // seq,),
        in_specs=[col(0), col(1), col(2)],
        out_specs=col(0),
        out_shape=jax.ShapeDtypeStruct((T, width), F32),
        compiler_params=_params("arbitrary"),
        name="attention_context",
    )(z, z, z)


def _attn_lat_kernel(q_ref, k_ref, v_ref, kc_ref, vc_ref, bias_ref, o_ref, *, heads, rows, kr):
    scale = HEAD_DIM ** -0.5
    r = pl.program_id(1)
    r0 = jnp.clip(r - kr // 2, 0, rows - kr)
    start = pl.multiple_of(r0 * GRID_W, GRID_W)
    k_win = k_ref[pl.ds(start, kr * GRID_W), :]
    v_win = v_ref[pl.ds(start, kr * GRID_W), :]
    gens = []
    for h in range(heads):
        sl = slice(h * HEAD_DIM, (h + 1) * HEAD_DIM)
        q = (q_ref[:, sl] * scale).astype(BF16)
        gens.append(_attn_head(q, [k_win[:, sl].astype(BF16), kc_ref[:, sl].astype(BF16)],
                               [v_win[:, sl].astype(BF16), vc_ref[:, sl].astype(BF16)], [bias_ref[h], None]))
    o_ref[...] = jnp.concatenate(_lock_step(gens), axis=-1)


def _toeplitz_kernel(rpb_ref, pick_ref, valid_ref, o_ref):
    o_ref[...] = jnp.where(valid_ref[...] > 0.5, _dot_hi(rpb_ref[...], pick_ref[...]), MASK_NEG)


def _window_bias(rpb, kr):
    rpb = rpb.reshape((-1,) + rpb.shape[2:])
    delta = np.arange(kr)[:, None]
    i = np.arange(kr)[None, :]
    row_off = (WIN_ROWS - 1) - delta + i
    q = np.arange(GRID_W)[:, None]
    kc = np.arange(GRID_W)[None, :]
    c0 = np.clip(q - WIN_COLS // 2, 0, GRID_W - WIN_COLS)
    valid = (kc >= c0) & (kc < c0 + WIN_COLS)
    col_off = kc - q + (WIN_COLS - 1)
    pick = (col_off[None] == np.arange(2 * WIN_COLS - 1)[:, None, None]) & valid[None]
    H, n_ro, n_co = rpb.shape
    toep = pl.pallas_call(
        _toeplitz_kernel,
        out_shape=jax.ShapeDtypeStruct((H * n_ro, GRID_W * GRID_W), F32),
        name="window_bias",
    )(rpb.astype(F32).reshape(H * n_ro, n_co), jnp.asarray(pick.reshape(n_co, -1), F32),
      jnp.asarray(valid.reshape(1, -1), F32))
    toep = toep.reshape(H, n_ro, GRID_W, GRID_W)
    per_delta = []
    for dl in range(kr):
        lo = int(row_off[dl, 0])
        win = toep[:, lo:lo + kr]
        per_delta.append(win.transpose(0, 2, 1, 3).reshape(rpb.shape[0], GRID_W, kr * GRID_W))
    return jnp.stack(per_delta, axis=0)


def _attention_latent(z, k_ctx, v_ctx, bias, layer, dec_seq, width):
    B, _, P, _ = k_ctx.shape
    rows = dec_seq // GRID_W
    kr = bias.shape[0]
    heads = width // HEAD_DIM

    def bias_index(b, r):
        return (r - jnp.clip(r - kr // 2, 0, rows - kr), layer, 0, 0)

    q_spec = pl.BlockSpec((GRID_W, width), lambda b, r: (b * rows + r, 0))
    return pl.pallas_call(
        functools.partial(_attn_lat_kernel, heads=heads, rows=rows, kr=kr),
        grid=(B, rows),
        in_specs=[
            q_spec,
            pl.BlockSpec((dec_seq, width), lambda b, r: (b, 1)),
            pl.BlockSpec((dec_seq, width), lambda b, r: (b, 2)),
            pl.BlockSpec((None, None, P, width), lambda b, r: (b, layer, 0, 0)),
            pl.BlockSpec((None, None, P, width), lambda b, r: (b, layer, 0, 0)),
            pl.BlockSpec((None, heads, GRID_W, kr * GRID_W), bias_index),
        ],
        out_specs=q_spec,
        out_shape=jax.ShapeDtypeStruct((B * dec_seq, width), F32),
        compiler_params=_params("arbitrary", "arbitrary"),
        name="attention_latent",
    )(z, z, z, k_ctx, v_ctx, bias)


def _sgu_kernel(u_ref, v_ref, gain_ref, w_ref, b_ref, o_ref, *, groups, chunks):
    for c in range(chunks):
        rows = slice(c * SG_CHUNK, (c + 1) * SG_CHUNK)
        for g in range(groups):
            sl = slice(g * SG_GROUP_DIM, (g + 1) * SG_GROUP_DIM)
            vn = _rms(v_ref[rows, sl]) * gain_ref[:, sl]
            mixed = _dot(w_ref[g].astype(BF16), vn.astype(BF16)) + b_ref[:, g:g + 1]
            o_ref[rows, sl] = u_ref[rows, sl] * mixed


def _spatial_gating(z, col_u, gain, w_s, b_t, layer):
    T = z.shape[0]
    G = w_s.shape[1]
    W = G * SG_GROUP_DIM
    tm = _pick(T, (512, 256, 128))
    return pl.pallas_call(
        functools.partial(_sgu_kernel, groups=G, chunks=tm // SG_CHUNK),
        grid=(T // tm,),
        in_specs=[pl.BlockSpec((tm, W), lambda i: (i, col_u)),
                  pl.BlockSpec((tm, W), lambda i: (i, col_u + 1)),
                  _of_layer(gain, layer), _of_layer(w_s, layer), _of_layer(b_t, layer)],
        out_specs=pl.BlockSpec((tm, W), lambda i: (i, 0)),
        out_shape=jax.ShapeDtypeStruct((T, W), F32),
        compiler_params=_params("arbitrary"),
        name="spatial_gating",
    )(z, z, gain, w_s, b_t)


def _softplus(y):
    return jnp.maximum(y, 0.0) + jnp.log(1.0 + jnp.exp(-jnp.abs(y)))


def _rwkv_cols(z_cols, width):
    bw = math.gcd(z_cols, width)
    assert bw % LANES == 0
    return bw, z_cols // bw, (z_cols + width) // bw, (z_cols + 2 * width) // bw


def _rwkv_pair_chunk(at, rt, bh, kh, v, hs, masks):
    C = RW_CHUNK
    strict, incl, eye, same_head, lane_lo = masks
    zero = jnp.zeros_like(at)
    stack = lambda x: jnp.concatenate([jnp.where(lane_lo, x, zero), jnp.where(lane_lo, zero, x)], axis=0)
    twice = lambda x: jnp.concatenate([x, x], axis=0)
    atm, rtm, vm = stack(at), stack(rt), stack(v)
    bk = jnp.concatenate([twice(bh), twice(kh)], axis=0)
    a = _dot_nt(jnp.concatenate([atm, rtm], axis=0), bk)
    yield
    l = jnp.where(strict, a[:2 * C, :2 * C], 0.0)
    a_ak = jnp.where(strict, a[:2 * C, 2 * C:], 0.0).astype(BF16)
    a_rb = jnp.where(incl, a[2 * C:, :2 * C], 0.0).astype(BF16)
    a_rk = jnp.where(incl, a[2 * C:, 2 * C:], 0.0).astype(BF16)
    hs_b = hs.astype(BF16)
    x2 = _dot(jnp.concatenate([atm, a_ak], axis=1), jnp.concatenate([hs_b, vm], axis=0)).astype(BF16)
    t = eye + l
    pw = l.astype(BF16)
    pw = _dot(pw, pw).astype(BF16)
    yield
    for _ in range(int(math.log2(C)) - 2):
        both = _dot(jnp.concatenate([t.astype(BF16), pw], axis=0), pw)
        yield
        t = t + both[:2 * C]
        pw = both[2 * C:].astype(BF16)
    t = t + _dot(t.astype(BF16), pw)
    yield
    u = _dot(t.astype(BF16), x2).astype(BF16)
    yield
    o = _dot(jnp.concatenate([rtm, a_rb, a_rk], axis=1), jnp.concatenate([hs_b, u, vm], axis=0))
    h_new = hs + jnp.where(same_head, _dot_tn(bk, jnp.concatenate([u, vm], axis=0)), 0.0)
    return o[:C] + o[C:], h_new


def _rwkv_operands(r, k, v, lora, w0, w2, a0, a2, kkw, ka, bd, tri, d, lora_w, lora_a):
    C, RW = r.shape
    kk = k * kkw
    kk2 = kk * kk
    ssq = jnp.concatenate([_dot_wide_lhs(kk2[:, p * LANES:(p + 1) * LANES], bd) for p in range(RW // LANES)], axis=-1)
    kk = kk * lax.rsqrt(ssq + KK_EPS)
    tw = jnp.tanh(lora[:, :lora_w]).astype(BF16)
    xa = lora[:, lora_w:lora_w + lora_a].astype(BF16)
    w_log = -_softplus(-(w0[d:d + 1, :] + _dot(tw, w2[d].astype(BF16)))) - 0.5
    logw = -jnp.exp(w_log)
    a_rate = jax.nn.sigmoid(a0[d:d + 1, :] + _dot(xa, a2[d].astype(BF16)))
    k_d = k * (1.0 + (a_rate - 1.0) * ka)
    cum = _dot_wide_rhs(tri[d], logw)
    tot_row = C - 1 if d == 0 else 0
    tot = cum[tot_row:tot_row + 1, :]
    e_dn = jnp.exp(tot - cum)
    at = -kk * jnp.exp(cum - logw - tot)
    rt = r * jnp.exp(cum - tot)
    bh = kk * a_rate * e_dn
    kh = k_d * e_dn
    return tuple(x.astype(BF16) for x in (at, rt, bh, kh, v)) + (jnp.exp(tot),)


def _rwkv_fused_kernel(*refs, pairs, group, n_col, lora_w, lora_a):
    C = RW_CHUNK
    n_tok = 3 * n_col + 1
    tok_refs = (refs[:n_tok], refs[n_tok:2 * n_tok])
    w0_ref, w2_ref, a0_ref, a2_ref, kkw_ref, ka_ref, bd_ref, tri_ref, h0_ref = refs[2 * n_tok:2 * n_tok + 9]
    of_ref, ob_ref, hs_ref = refs[2 * n_tok + 9:]

    @pl.when(pl.program_id(1) == 0)
    def _():
        hs_ref[...] = h0_ref[...]

    row = _iota((2 * C, 2 * C), 0)
    col = _iota((2 * C, 2 * C), 1)
    same = (row // C) == (col // C)
    rt_, ct_ = row % C, col % C
    eye = jnp.where(row == col, 1.0, 0.0).astype(F32)
    lrow = _iota((LANES, LANES), 0)
    lcol = _iota((LANES, LANES), 1)
    same_head = (lrow // HEAD_DIM) == (lcol // HEAD_DIM)
    lane_lo = _iota((C, LANES), 1) < HEAD_DIM
    masks = ((same & (ct_ < rt_), same & (ct_ <= rt_), eye, same_head, lane_lo),
             (same & (ct_ > rt_), same & (ct_ >= rt_), eye, same_head, lane_lo))
    outs = (of_ref, ob_ref)

    params = (w0_ref[...], w2_ref[...], a0_ref[...], a2_ref[...], kkw_ref[...], ka_ref[...], bd_ref[...], tri_ref[...])
    operands = []
    for d in range(2):
        t = tok_refs[d]
        cat = lambda rs: jnp.concatenate([x[...] for x in rs], axis=-1)
        r, k, v = cat(t[:n_col]), cat(t[n_col:2 * n_col]), cat(t[2 * n_col:3 * n_col])
        operands.append(_rwkv_operands(r, k, v, t[3 * n_col][...], *params, d, lora_w, lora_a))

    for g0 in range(0, pairs, group):
        jobs = [(d, p) for p in range(g0, g0 + group) for d in range(2)]
        gens = []
        for d, p in jobs:
            sl = slice(p * LANES, (p + 1) * LANES)
            at, rt, bh, kh, v, wc = (x[:, sl] for x in operands[d])
            wcol = jnp.sum(jnp.where(lrow == lcol, jnp.broadcast_to(wc, (LANES, LANES)), 0.0), axis=1, keepdims=True)
            gens.append(_rwkv_pair_chunk(at, rt, bh, kh, v, hs_ref[d, p] * wcol, masks[d]))
        for (d, p), (o, h_new) in zip(jobs, _lock_step(gens)):
            hs_ref[d, p] = h_new
            outs[d][p] = o


def _rwkv_scan_fused(z, r_off, lora_blk, lora_dims, w0, w2, a0, a2, kkw, ka, bd, tri, layer, h0, n_seq, seq_len):
    T = z.shape[0]
    RW = w0.shape[-1]
    lora_w, lora_a, lora_g = lora_dims
    bw, cr, ck, cv = _rwkv_cols(r_off, RW)
    n_col = RW // bw
    pairs = RW // LANES
    C = RW_CHUNK
    nch = seq_len // C
    fwd = lambda s, c: s * nch + c
    bwd = lambda s, c: s * nch + (nch - 1 - c)

    def tok_specs(f):
        cols = [c0 + h for c0 in (cr, ck, cv) for h in range(n_col)]
        specs = [pl.BlockSpec((C, bw), lambda s, c, j=j: (f(s, c), j)) for j in cols]
        return specs + [pl.BlockSpec((C, sum(lora_dims)), lambda s, c: (f(s, c), lora_blk))]

    st = pl.BlockSpec((None, 2, pairs, LANES, LANES), lambda s, c: (s, 0, 0, 0, 0))
    pk = lambda f: pl.BlockSpec((pairs, C, LANES), lambda s, c: (0, f(s, c), 0))
    o_shape = jax.ShapeDtypeStruct((pairs, T, LANES), F32)
    n_tok = 3 * n_col + 1
    return pl.pallas_call(
        functools.partial(_rwkv_fused_kernel, pairs=pairs, group=_pick(pairs, (RW_PAIR_GROUP, 4, 2, 1)),
                          n_col=n_col, lora_w=lora_w, lora_a=lora_a),
        grid=(n_seq, nch),
        in_specs=tok_specs(fwd) + tok_specs(bwd)
        + [_of_layer(a, layer) for a in (w0, w2, a0, a2, kkw, ka)] + [_full(bd), _full(tri), st],
        out_specs=[pk(fwd), pk(bwd), st],
        out_shape=[o_shape, o_shape, jax.ShapeDtypeStruct(h0.shape, F32)],
        compiler_params=_params("arbitrary", "arbitrary"),
        name="rwkv_scan",
    )(*([z] * (2 * n_tok)), w0, w2, a0, a2, kkw, ka, bd, tri, h0)


def _pair_states(h):
    n, _, H, _, _ = h.shape
    hp = h.reshape(n, 2, H // 2, 2, HEAD_DIM, HEAD_DIM)
    z = jnp.zeros_like(hp[:, :, :, 0])
    top = jnp.concatenate([hp[:, :, :, 0], z], axis=-1)
    bot = jnp.concatenate([z, hp[:, :, :, 1]], axis=-1)
    return jnp.concatenate([top, bot], axis=-2)


def _rwkv_post_kernel(of_ref, ob_ref, r_ref, k_ref, v_ref, lora_ref, g2_ref, lnw_ref, lnb_ref, rk_ref, bd_ref,
                      o_ref, *, pairs, lora_g):
    bd = bd_ref[...]
    n_lora = lora_ref.shape[1]
    sg = jax.nn.sigmoid(lora_ref[:, n_lora - lora_g:]).astype(BF16)
    inv = 1.0 / HEAD_DIM
    for p in range(pairs):
        sl = slice(p * LANES, (p + 1) * LANES)
        o = of_ref[p] + ob_ref[p]
        mu = _dot_wide_lhs(o, bd) * inv
        dlt = o - mu
        var = _dot_wide_lhs(dlt * dlt, bd) * inv
        on = dlt * lax.rsqrt(var + GN_EPS) * lnw_ref[:, sl] + lnb_ref[:, sl]
        bonus = _dot_wide_lhs(r_ref[:, sl] * k_ref[:, sl] * rk_ref[:, sl], bd) * v_ref[:, sl]
        gate = _dot(sg, g2_ref[:, sl].astype(BF16))
        o_ref[:, sl] = (on + bonus) * gate


def _rwkv_finish(o_f, o_b, z, r_off, lora_blk, lora_dims, g2, lnw, lnb, rk, bd, layer):
    T = z.shape[0]
    RW = g2.shape[-1]
    lora_n = sum(lora_dims)
    bw, cr, ck, cv = _rwkv_cols(r_off, RW)
    pairs = bw // LANES
    tm = _pick(T, (256, 128, 64))
    tok = lambda c0: pl.BlockSpec((tm, bw), lambda i, h: (i, c0 + h))
    par2 = lambda a: _of_layer(a, layer, (a.shape[1], bw), lambda i, h: (0, h))
    packed = pl.BlockSpec((pairs, tm, LANES), lambda i, h: (h, i, 0))
    return pl.pallas_call(
        functools.partial(_rwkv_post_kernel, pairs=pairs, lora_g=lora_dims[2]),
        grid=(T // tm, RW // bw),
        in_specs=[packed, packed, tok(cr), tok(ck), tok(cv),
                  pl.BlockSpec((tm, lora_n), lambda i, h: (i, lora_blk)),
                  par2(g2), par2(lnw), par2(lnb), par2(rk), _full(bd)],
        out_specs=pl.BlockSpec((tm, bw), lambda i, h: (i, h)),
        out_shape=jax.ShapeDtypeStruct((T, RW), F32),
        compiler_params=_params("arbitrary", "arbitrary"),
        name="rwkv_finish",
    )(o_f, o_b, z, z, z, z, g2, lnw, lnb, rk, bd)


def _outproj_kernel(oa_ref, ob_ref, oc_ref, x_ref, w_ref, g1_ref, sh2_ref, sc2_ref, gpost_ref, gpre_ref, wr_ref,
                    xo_ref, h2_ref, aff_ref):
    cat = jnp.concatenate([oa_ref[...], ob_ref[...], oc_ref[...]], axis=-1).astype(BF16)
    mixed = _dot(cat, w_ref[...])
    x = x_ref[...] + g1_ref[...] * (_rms(mixed) * gpost_ref[...])
    xo_ref[...] = x
    h2 = _rms(x) * gpre_ref[...] * (1.0 + sc2_ref[...]) + sh2_ref[...]
    h2_ref[...] = h2.astype(BF16)
    h_hi, h_lo = _split2(h2)
    w_hi, w_lo = _split2(wr_ref[...])
    logits = _dot(h_hi, w_hi) + _dot(h_lo, w_hi) + _dot(h_hi, w_lo)
    e = jnp.exp(logits - jnp.max(logits, axis=-1, keepdims=True))
    aff_ref[...] = e / jnp.sum(e, axis=-1, keepdims=True)


def _out_projection(o_a, o_b, o_c, x, w_out, layer, mods, mod_row, g_post, g_pre, w_router, tm):
    T, D = x.shape
    E = w_router.shape[2]
    row = lambda a: pl.BlockSpec((tm, a.shape[1]), lambda i: (i, 0))
    mod = lambda col: _mod_spec(mods, layer, mod_row, col)
    return pl.pallas_call(
        _outproj_kernel,
        grid=(T // tm,),
        in_specs=[row(o_a), row(o_b), row(o_c), row(x), _of_layer(w_out, layer),
                  mod(2), mod(3), mod(4), _of_layer(g_post, layer), _of_layer(g_pre, layer),
                  _of_layer(w_router, layer)],
        out_specs=[pl.BlockSpec((tm, D), lambda i: (i, 0)),
                   pl.BlockSpec((tm, D), lambda i: (i, 0)),
                   pl.BlockSpec((tm, E), lambda i: (i, 0))],
        out_shape=[jax.ShapeDtypeStruct((T, D), F32), jax.ShapeDtypeStruct((T, D), BF16),
                   jax.ShapeDtypeStruct((T, E), F32)],
        compiler_params=_params("arbitrary"),
        name="out_projection",
    )(o_a, o_b, o_c, x, w_out, mods, mods, mods, g_post, g_pre, w_router)


def _select_kernel(a_ref, ut_ref, pos_ref, cnt_ref, *, cap):
    a = a_ref[...]
    R, T = a.shape
    lo = jnp.zeros((R, 1), jnp.int32)
    for bit in range(30, -1, -1):
        cand = lo | (1 << bit)
        cnt = jnp.sum(jnp.where(a >= lax.bitcast_convert_type(cand, F32), 1.0, 0.0), axis=1, keepdims=True)
        lo = jnp.where(cnt >= cap, cand, lo)
    thr = lax.bitcast_convert_type(lo, F32)
    gt = a > thr
    eq = a == thr
    need = cap - jnp.sum(jnp.where(gt, 1.0, 0.0), axis=1, keepdims=True)
    ut = ut_ref[...]
    nblk = T // LANES
    tie_carry = jnp.zeros((R, 1), F32)
    pos_carry = jnp.zeros((R, 1), F32)
    for n in range(nblk):
        sl = slice(n * LANES, (n + 1) * LANES)
        eq_b = jnp.where(eq[:, sl], 1.0, 0.0)
        tie_rank = _dot(eq_b.astype(BF16), ut) + tie_carry
        tie_carry = tie_carry + jnp.sum(eq_b, axis=1, keepdims=True)
        sel = jnp.where(gt[:, sl], 1.0, jnp.where(tie_rank < need, eq_b, 0.0))
        pos = _dot(sel.astype(BF16), ut) + pos_carry
        cnt_ref[:, n:n + 1] = pos_carry.astype(jnp.int32)
        pos_carry = pos_carry + jnp.sum(sel, axis=1, keepdims=True)
        pos_ref[:, sl] = jnp.where(sel > 0.5, pos, -1.0).astype(jnp.int32)


def _select(aff_rows, cap, ut):
    R, T = aff_rows.shape
    tr = _pick(R, (128, 64, 32, 16, 8))
    return pl.pallas_call(
        functools.partial(_select_kernel, cap=cap),
        grid=(R // tr,),
        in_specs=[pl.BlockSpec((tr, T), lambda i: (i, 0)), pl.BlockSpec((LANES, LANES), lambda i: (0, 0))],
        out_specs=[pl.BlockSpec((tr, T), lambda i: (i, 0)), pl.BlockSpec((tr, T // LANES), lambda i: (i, 0))],
        out_shape=[jax.ShapeDtypeStruct((R, T), jnp.int32), jax.ShapeDtypeStruct((R, T // LANES), jnp.int32)],
        compiler_params=_params("arbitrary"),
        name="expert_select",
    )(aff_rows, ut)


def _route(aff, n_sets, set_len, cap, ut):
    E = aff.shape[1]
    rows = aff.reshape(n_sets, set_len, E).transpose(0, 2, 1).reshape(n_sets * E, set_len)
    pos, cnt = _select(rows, cap, ut)
    pos = pos.reshape(n_sets, E, set_len)
    return pos, pos.transpose(0, 2, 1).reshape(n_sets * set_len, E), cnt, rows


def _slot_tables(experts, cap):
    pair_expert = np.arange(experts * cap) // cap
    rep = (pair_expert[:, None] == np.arange(experts)[None, :]).astype(np.float32)
    slot = (np.arange(experts * cap) % cap).astype(np.float32)
    return jnp.asarray(rep, BF16), jnp.asarray(rep.T, BF16), jnp.asarray(slot[:, None]), jnp.asarray(slot[None, :])


def _gather_ctx_kernel(h_ref, pos_ref, aff_ref, rep_ref, slot_ref, xs_ref, gs_ref, *, experts, cap):
    rep = rep_ref[...]
    pos_rep = _dot(rep, pos_ref[...].astype(F32).astype(BF16))
    hit = pos_rep == slot_ref[...]
    xs = _dot(jnp.where(hit, 1.0, 0.0).astype(BF16), h_ref[...]).astype(BF16)
    xs_ref[...] = xs.reshape(experts, cap, xs.shape[1])
    gs = jnp.sum(jnp.where(hit, _dot_wide_rhs(rep, aff_ref[...]), 0.0), axis=1, keepdims=True)
    gs_ref[...] = gs.reshape(experts, cap, 1)


def _gather_lat_kernel(cnt_ref, h_ref, pos_ref, aff_ref, xs_ref, gs_ref, acc_scr, gacc_scr, *, cap, experts):
    row = pl.program_id(0) * experts + pl.program_id(1)
    n_blocks = pos_ref.shape[0]
    step = GATHER_TOKENS // LANES
    for i in range(cap // SLOT_TILE):
        first_slot = i * SLOT_TILE
        starts = [cnt_ref[row, n * step] for n in range(n_blocks)]
        n_lo = sum(jnp.where(s <= first_slot, 1, 0) for s in starts) - 1
        n_hi = sum(jnp.where(s < first_slot + SLOT_TILE, 1, 0) for s in starts) - 1
        acc_scr[...] = jnp.zeros(acc_scr.shape, F32)
        gacc_scr[...] = jnp.zeros(gacc_scr.shape, F32)

        def body(n, carry, first_slot=first_slot):
            toks = h_ref[pl.ds(pl.multiple_of(n * GATHER_TOKENS, GATHER_TOKENS), GATHER_TOKENS), :]
            hit = _iota((SLOT_TILE, GATHER_TOKENS), 0) + first_slot == pos_ref[n]
            acc_scr[...] += _dot(jnp.where(hit, 1.0, 0.0).astype(BF16), toks)
            gacc_scr[...] += jnp.sum(jnp.where(hit, aff_ref[n], 0.0), axis=1, keepdims=True)
            return carry

        lax.fori_loop(n_lo, n_hi + 1, body, 0)
        xs_ref[first_slot:first_slot + SLOT_TILE, :] = acc_scr[...].astype(BF16)
        gs_ref[first_slot:first_slot + SLOT_TILE, :] = gacc_scr[...]


def _expert_up_kernel(xc_ref, xl_ref, wg_ref, wu_ref, hid_ref):
    wg = wg_ref[...].astype(BF16)
    wu = wu_ref[...].astype(BF16)
    rc = xc_ref.shape[0]
    for ref, rows in ((xc_ref, slice(0, rc)), (xl_ref, slice(rc, hid_ref.shape[0]))):
        x = ref[...]
        g = _dot(x, wg)
        hid_ref[rows, :] = (g * jax.nn.sigmoid(g) * _dot(x, wu)).astype(BF16)


def _expert_down_kernel(hid_ref, gsc_ref, gsl_ref, wd_ref, y_ref):
    wd = wd_ref[...].astype(BF16)
    rc = gsc_ref.shape[0]
    for gs_ref, rows in ((gsc_ref, slice(0, rc)), (gsl_ref, slice(rc, y_ref.shape[0]))):
        y_ref[rows, :] = (_dot(hid_ref[rows, :], wd) * gs_ref[...]).astype(BF16)


def _slot_one_hot(pos_col, cap):
    tm = pos_col.shape[0]
    return jnp.where(_iota((tm, cap), 1).astype(F32) == pos_col, 1.0, 0.0).astype(BF16)


def _col_chunks(width):
    cw = _pick(width, (2 * LANES, LANES))
    return [slice(n * cw, (n + 1) * cw) for n in range(width // cw)]


def _ffn_residual(x, ffn, g2, g_post):
    return x + g2 * (_rms(ffn) * g_post)


def _combine_ctx_kernel(x_ref, y_ref, pos_ref, rep_ref, slot_ref, g2_ref, gpost_ref, o_ref, *, experts, cap):
    pos_exp = _dot(pos_ref[...].astype(F32).astype(BF16), rep_ref[...])
    one_hot = jnp.where(pos_exp == slot_ref[...], 1.0, 0.0).astype(BF16)
    y = y_ref[...].reshape(experts * cap, y_ref.shape[2])
    o_ref[...] = _ffn_residual(x_ref[...], _dot(one_hot, y), g2_ref[...], gpost_ref[...])


def _combine_lat_kernel(x_ref, y_ref, pos_ref, g2_ref, gpost_ref, o_ref, acc_scr, *, cap, group):
    eg = pl.program_id(2)

    @pl.when(eg == 0)
    def _():
        acc_scr[...] = jnp.zeros(acc_scr.shape, F32)

    lane = _iota(pos_ref.shape, 1)
    pos = pos_ref[...].astype(F32)
    one_hots = []
    for k in range(group):
        pos_col = jnp.sum(jnp.where(lane == eg * group + k, pos, 0.0), axis=1, keepdims=True)
        one_hots.append(_slot_one_hot(pos_col, cap))
    for cs in _col_chunks(acc_scr.shape[1]):
        acc_scr[:, cs] += sum(_dot(oh, y_ref[k, :, cs]) for k, oh in enumerate(one_hots))

    @pl.when(eg == pl.num_programs(2) - 1)
    def _():
        o_ref[...] = _ffn_residual(x_ref[...], acc_scr[...], g2_ref[...], gpost_ref[...])


def _expert_ffn(x_c, h2_c, aff_c, x_l, h2_l, aff_l, mods, g_post, w_gate, w_up, w_down, layer, ut, dims):
    batch, seq, dec_batch, dec_seq = dims
    D = x_c.shape[1]
    _, E, _, F = w_gate.shape
    cap_c = CAPACITY_FACTOR * seq // E
    cap_l = CAPACITY_FACTOR * dec_seq // E
    rows_c = batch * cap_c
    rows_l = dec_batch * cap_l
    rows = rows_c + rows_l
    assert rows_c % cap_l == 0
    lat_row_blk = rows_c // cap_l

    assert cap_c <= 256 and cap_l % SLOT_TILE == 0 and dec_seq % GATHER_TOKENS == 0 and E % COMBINE_GROUP == 0
    pos_c, post_c, _, affr_c = _route(aff_c, batch, seq, cap_c, ut)
    pos_l, post_l, cnt_l, affr_l = _route(aff_l, dec_batch, dec_seq, cap_l, ut)
    rep, rep_t, slot_col, slot_row = _slot_tables(E, cap_c)

    set_rows = pl.BlockSpec((None, E, seq), lambda b: (b, 0, 0))
    xs_c, gs_c = pl.pallas_call(
        functools.partial(_gather_ctx_kernel, experts=E, cap=cap_c),
        grid=(batch,),
        in_specs=[pl.BlockSpec((seq, D), lambda b: (b, 0)), set_rows, set_rows, _full(rep), _full(slot_col)],
        out_specs=[pl.BlockSpec((E, cap_c, D), lambda b: (0, b, 0)), pl.BlockSpec((E, cap_c, 1), lambda b: (0, b, 0))],
        out_shape=[jax.ShapeDtypeStruct((E, rows_c, D), BF16), jax.ShapeDtypeStruct((E, rows_c, 1), F32)],
        compiler_params=_params("arbitrary"),
        name="expert_gather_context",
    )(h2_c, pos_c, affr_c.reshape(batch, E, seq), rep, slot_col)
    n_tok_blk = dec_seq // GATHER_TOKENS
    blocked = pl.BlockSpec((None, None, n_tok_blk, 1, GATHER_TOKENS), lambda b, e, cnt: (b, e, 0, 0, 0))
    xs_l, gs_l = pl.pallas_call(
        functools.partial(_gather_lat_kernel, cap=cap_l, experts=E),
        grid_spec=pltpu.PrefetchScalarGridSpec(
            num_scalar_prefetch=1,
            grid=(dec_batch, E),
            in_specs=[pl.BlockSpec((dec_seq, D), lambda b, e, cnt: (b, 0)), blocked, blocked],
            out_specs=[pl.BlockSpec((None, cap_l, D), lambda b, e, cnt: (e, b, 0)),
                       pl.BlockSpec((None, cap_l, 1), lambda b, e, cnt: (e, b, 0))],
            scratch_shapes=[pltpu.VMEM((SLOT_TILE, D), F32), pltpu.VMEM((SLOT_TILE, 1), F32)]),
        out_shape=[jax.ShapeDtypeStruct((E, rows_l, D), BF16), jax.ShapeDtypeStruct((E, rows_l, 1), F32)],
        compiler_params=_params("arbitrary", "arbitrary"),
        name="expert_gather_latent",
    )(cnt_l, h2_l, pos_l.reshape(dec_batch, E, n_tok_blk, 1, GATHER_TOKENS),
      affr_l.reshape(dec_batch, E, n_tok_blk, 1, GATHER_TOKENS))

    tf = _pick(F, (512, 256, 128))
    hid = pl.pallas_call(
        _expert_up_kernel,
        grid=(E, F // tf),
        in_specs=[pl.BlockSpec((None, rows_c, D), lambda e, n: (e, 0, 0)),
                  pl.BlockSpec((None, rows_l, D), lambda e, n: (e, 0, 0)),
                  pl.BlockSpec((None, None, D, tf), lambda e, n: (layer, e, 0, n)),
                  pl.BlockSpec((None, None, D, tf), lambda e, n: (layer, e, 0, n))],
        out_specs=pl.BlockSpec((None, rows, tf), lambda e, n: (e, 0, n)),
        out_shape=jax.ShapeDtypeStruct((E, rows, F), BF16),
        compiler_params=_params("arbitrary", "arbitrary"),
        name="expert_up",
    )(xs_c, xs_l, w_gate, w_up)
    td = _pick(D, (512, 256, 128))
    y = pl.pallas_call(
        _expert_down_kernel,
        grid=(E, D // td),
        in_specs=[pl.BlockSpec((None, rows, F), lambda e, n: (e, 0, 0)),
                  pl.BlockSpec((None, rows_c, 1), lambda e, n: (e, 0, 0)),
                  pl.BlockSpec((None, rows_l, 1), lambda e, n: (e, 0, 0)),
                  pl.BlockSpec((None, None, F, td), lambda e, n: (layer, e, 0, n))],
        out_specs=pl.BlockSpec((None, rows, td), lambda e, n: (e, 0, n)),
        out_shape=jax.ShapeDtypeStruct((E, rows, D), BF16),
        compiler_params=_params("arbitrary", "arbitrary"),
        name="expert_down",
    )(hid, gs_c, gs_l, w_down)

    out_c = pl.pallas_call(
        functools.partial(_combine_ctx_kernel, experts=E, cap=cap_c),
        grid=(batch,),
        in_specs=[pl.BlockSpec((seq, D), lambda b: (b, 0)),
                  pl.BlockSpec((E, cap_c, D), lambda b: (0, b, 0)),
                  pl.BlockSpec((seq, E), lambda b: (b, 0)),
                  _full(rep_t), _full(slot_row),
                  pl.BlockSpec((None, None, 1, D), lambda b: (layer, 0, 0, N_MOD - 1)),
                  _of_layer(g_post, layer)],
        out_specs=pl.BlockSpec((seq, D), lambda b: (b, 0)),
        out_shape=jax.ShapeDtypeStruct(x_c.shape, F32),
        compiler_params=_params("arbitrary"),
        name="expert_combine_context",
    )(x_c, y, post_c, rep_t, slot_row, mods, g_post)
    tm = _pick(dec_seq, (512, 256, 128))
    nt = dec_seq // tm
    tok = lambda b, i, e: (b * nt + i, 0)
    out_l = pl.pallas_call(
        functools.partial(_combine_lat_kernel, cap=cap_l, group=COMBINE_GROUP),
        grid=(dec_batch, nt, E // COMBINE_GROUP),
        in_specs=[pl.BlockSpec((tm, D), tok),
                  pl.BlockSpec((COMBINE_GROUP, cap_l, D), lambda b, i, e: (e, lat_row_blk + b, 0)),
                  pl.BlockSpec((tm, E), tok),
                  pl.BlockSpec((None, None, 1, D), lambda b, i, e: (layer, 1 + b, 0, N_MOD - 1)),
                  _of_layer(g_post, layer)],
        out_specs=pl.BlockSpec((tm, D), tok),
        out_shape=jax.ShapeDtypeStruct(x_l.shape, F32),
        scratch_shapes=[pltpu.VMEM((tm, D), F32)],
        compiler_params=_params("arbitrary", "arbitrary", "arbitrary"),
        name="expert_combine_latent",
    )(x_l, y, post_l, mods, g_post)
    return out_c, out_l


def _cache_pack_kernel(*refs, depth):
    k_out, v_out = refs[2 * depth:]
    for li in range(depth):
        @pl.when(pl.program_id(0) == li)
        def _(li=li):
            k_out[...] = refs[2 * li][...]
            v_out[...] = refs[2 * li + 1][...]


def _cache_pack(zs, batch, seq, width):
    depth = len(zs)
    in_specs, args = [], []
    for z in zs:
        in_specs += [pl.BlockSpec((seq, width), lambda l, b: (b, 1)), pl.BlockSpec((seq, width), lambda l, b: (b, 2))]
        args += [z, z]
    out_spec = pl.BlockSpec((None, None, seq, width), lambda l, b: (b, l, 0, 0))
    shape = jax.ShapeDtypeStruct((batch, depth, seq, width), F32)
    return pl.pallas_call(
        functools.partial(_cache_pack_kernel, depth=depth),
        grid=(depth, batch), in_specs=in_specs, out_specs=[out_spec, out_spec], out_shape=[shape, shape],
        compiler_params=_params("arbitrary", "arbitrary"), name="cache_pack",
    )(*args)


def _state_pack_kernel(*refs, depth, pairs):
    o_ref = refs[depth]
    for li in range(depth):
        for d in range(2):
            for p in range(pairs):
                flipped = refs[li][d, p].T
                for j in range(LANES // HEAD_DIM):
                    sl = slice(j * HEAD_DIM, (j + 1) * HEAD_DIM)
                    o_ref[li, d, (LANES // HEAD_DIM) * p + j] = flipped[sl, sl]


def _state_pack(states):
    depth = len(states)
    B, _, pairs, _, _ = states[0].shape
    heads = pairs * (LANES // HEAD_DIM)
    spec = pl.BlockSpec((None, 2, pairs, LANES, LANES), lambda b: (b, 0, 0, 0, 0))
    return pl.pallas_call(
        functools.partial(_state_pack_kernel, depth=depth, pairs=pairs),
        grid=(B,), in_specs=[spec] * depth,
        out_specs=pl.BlockSpec((None, depth, 2, heads, HEAD_DIM, HEAD_DIM), lambda b: (b, 0, 0, 0, 0, 0)),
        out_shape=jax.ShapeDtypeStruct((B, depth, 2, heads, HEAD_DIM, HEAD_DIM), F32),
        compiler_params=_params("arbitrary"), name="state_pack",
    )(*states)


def kernel(x_prompt, x_sample, cache_k, cache_v, state_wkv, c, c_ctx, w_ada, b_ada, g_pre_mix, g_post_mix, g_pre_ffn, g_post_ffn, w_in, na_rpb, sg_gain, sg_w, sg_b, rw_w0, rw_w2, rw_a0, rw_a2, rw_g2, rw_kk, rw_ka, rw_rk, rw_ln_w, rw_ln_b, w_out, w_router, w_gate, w_up, w_down):
    batch, seq, D = x_prompt.shape
    dec_batch, dec_seq, _ = x_sample.shape
    depth = w_ada.shape[0]
    past = cache_k.shape[2]
    na_heads = cache_k.shape[3]
    naw = na_heads * HEAD_DIM
    sgw = sg_gain.shape[1]
    rw_heads = state_wkv.shape[3]
    rww = rw_heads * HEAD_DIM
    lora_dims = (rw_w2.shape[2], rw_a2.shape[2], rw_g2.shape[1])
    n_ctx = batch * seq
    n_lat = dec_batch * dec_seq
    assert 1 + dec_batch <= MOD_ROWS and rww % LANES == 0
    assert seq % RW_CHUNK == 0 and dec_seq % RW_CHUNK == 0 and dec_seq % GRID_W == 0
    sg_off = 3 * naw
    rw_off = sg_off + 2 * sgw
    lora_off = rw_off + 3 * rww
    assert sg_off % sgw == 0 and lora_off % sum(lora_dims) == 0
    col_u = sg_off // sgw
    lora_blk = lora_off // sum(lora_dims)

    pair_ones = jnp.asarray(np.kron(np.eye(LANES // HEAD_DIM), np.ones((HEAD_DIM, HEAD_DIM))), BF16)
    tril = np.tril(np.ones((RW_CHUNK, RW_CHUNK)))
    tri = jnp.asarray(np.stack([tril, tril.T]), BF16)
    ut = jnp.asarray(np.triu(np.ones((LANES, LANES)), 1), BF16)
    kr = min(WIN_ROWS, dec_seq // GRID_W)

    cond = jnp.zeros((MOD_ROWS, D), F32).at[0].set(c_ctx).at[1:1 + dec_batch].set(c)
    mods = _modulation(cond, w_ada, b_ada.reshape(depth, 1, N_MOD * D)).reshape(depth, MOD_ROWS, 1, N_MOD * D)
    w_in_b = _to_bf16(w_in)
    w_out_b = _to_bf16(w_out)

    vec = lambda a: a.reshape(depth, 1, a.shape[-1])
    g_pre_mix, g_post_mix, g_pre_ffn, g_post_ffn = vec(g_pre_mix), vec(g_post_mix), vec(g_pre_ffn), vec(g_post_ffn)
    sg_args = (vec(sg_gain), sg_w, jnp.swapaxes(sg_b, 1, 2))
    rw_args = (rw_w0, rw_w2, rw_a0, rw_a2, vec(rw_kk), vec(rw_ka), pair_ones, tri)
    fin_args = (rw_g2, vec(rw_ln_w), vec(rw_ln_b), rw_rk.reshape(depth, 1, rww), pair_ones)
    bias = _window_bias(na_rpb, kr)
    k_past = cache_k.reshape(dec_batch, depth, past, naw)
    v_past = cache_v.reshape(dec_batch, depth, past, naw)
    h0_lat = _pair_states(jnp.swapaxes(state_wkv, -1, -2).reshape((dec_batch * depth,) + state_wkv.shape[2:]))
    h0_lat = h0_lat.reshape((dec_batch, depth) + h0_lat.shape[1:])
    h0_ctx = jnp.zeros((batch, 2, rww // LANES, LANES, LANES), F32)

    tm_c = _pick(n_ctx, (512, 256, 128))
    tm_l = _pick(dec_seq, (512, 256, 128))
    ctx_row = lambda i: 0
    lat_row = lambda i: 1 + (i * tm_l) // dec_seq

    x_c = x_prompt.reshape(n_ctx, D)
    x_l = x_sample.reshape(n_lat, D)
    zs, states = [], []
    for l in range(depth):
        def mixers(x, tm, mod_row, attend, h0, n_seq, seq_len):
            z = _in_projection(x, mods, mod_row, g_pre_mix, w_in_b, l, tm)
            o_a = attend(z)
            o_b = _spatial_gating(z, col_u, *sg_args, l)
            o_f, o_bk, s_fin = _rwkv_scan_fused(z, rw_off, lora_blk, lora_dims, *rw_args, l, h0, n_seq, seq_len)
            o_c = _rwkv_finish(o_f, o_bk, z, rw_off, lora_blk, lora_dims, *fin_args, l)
            x, h2, aff = _out_projection(o_a, o_b, o_c, x, w_out_b, l, mods, mod_row,
                                         g_post_mix, g_pre_ffn, w_router, tm)
            return z, x, h2, aff, s_fin

        z_c, x_c, h2_c, aff_c, s_ctx = mixers(
            x_c, tm_c, ctx_row, lambda z: _attention_context(z, seq, naw), h0_ctx, batch, seq)
        zs.append(z_c)
        states.append(s_ctx)
        _, x_l, h2_l, aff_l, _ = mixers(
            x_l, tm_l, lat_row, lambda z: _attention_latent(z, k_past, v_past, bias, l, dec_seq, naw),
            h0_lat[:, l], dec_batch, dec_seq)

        x_c, x_l = _expert_ffn(x_c, h2_c, aff_c, x_l, h2_l, aff_l, mods, g_post_ffn,
                               w_gate, w_up, w_down, l, ut, (batch, seq, dec_batch, dec_seq))

    new_k, new_v = _cache_pack(zs, batch, seq, naw)
    cache_shape = (batch, depth, seq, na_heads, HEAD_DIM)
    return (x_c.reshape(batch, seq, D), x_l.reshape(dec_batch, dec_seq, D),
            new_k.reshape(cache_shape), new_v.reshape(cache_shape), _state_pack(states))
```
